```python
import jax, jax.numpy as jnp
from jax import lax
import numpy as np

D_MODEL = 1024
BATCH = 32
SEQ = 256
DEPTH = 1
DEC_BATCH = 4
DEC_SEQ = 2048
PAST_LEN = 256

GRID_W = 64
EPS = 1e-6
D_SSD = 2 * D_MODEL
SSD_HEADDIM = 64
SSD_HEADS = D_SSD // SSD_HEADDIM
SSD_STATE = 128
SSD_GROUPS = 4
SSD_CONV = 3
SSD_CHUNK = 128
BC_W = SSD_GROUPS * SSD_STATE
CONV_CH = D_SSD + 2 * BC_W
D_SGU = D_MODEL
SGU_CHUNK = 128
SGU_GROUP_DIM = 128
SGU_GROUPS = D_SGU // SGU_GROUP_DIM
D_FF = 2816
FFN_CONV = 3
SPLIT_Z = D_SSD
SPLIT_XBC = SPLIT_Z + CONV_CH
SPLIT_DT = SPLIT_XBC + 2 * SSD_HEADS
SPLIT_U = SPLIT_DT + D_SGU
SPLIT_V = SPLIT_U + D_SGU
SPLIT_GA = SPLIT_V + D_MODEL
IN_COLS = SPLIT_GA + D_MODEL

kernel_name = "hybrid_ssd_sgu_convffn_flow_step"


def _rmsnorm(x, w):
    xf = x.astype(jnp.float32)
    y = xf * lax.rsqrt(jnp.mean(xf * xf, axis=-1, keepdims=True) + EPS)
    return (y * w.astype(jnp.float32)).astype(x.dtype)


def _layernorm(x, w, b):
    xf = x.astype(jnp.float32)
    mu = jnp.mean(xf, axis=-1, keepdims=True)
    var = jnp.mean(jnp.square(xf - mu), axis=-1, keepdims=True)
    y = (xf - mu) * lax.rsqrt(var + EPS)
    return (y * w.astype(jnp.float32) + b.astype(jnp.float32)).astype(x.dtype)


def _conv1d_centred(x, w, b):
    k = w.shape[0]
    y = lax.conv_general_dilated(x, w[:, None, :].astype(x.dtype), window_strides=(1,),
                                 padding=((k // 2, k // 2),), dimension_numbers=("NWC", "WIO", "NWC"),
                                 feature_group_count=x.shape[-1])
    return y + b


def _conv2d_grid(x, w, b):
    bsz, l, ch = x.shape
    rows = l // GRID_W
    xg = x.reshape(bsz, rows, GRID_W, ch)
    y = lax.conv_general_dilated(xg, w[:, :, None, :].astype(x.dtype), window_strides=(1, 1),
                                 padding=((1, 1), (1, 1)), dimension_numbers=("NHWC", "HWIO", "NHWC"),
                                 feature_group_count=ch)
    return y.reshape(bsz, l, ch) + b


def _ssd_chunked(xdt, a, B, C, h0):
    bsz, l, nh, p = xdt.shape
    g, n = B.shape[2], B.shape[3]
    r = nh // g
    q = SSD_CHUNK
    nc = l // q
    x = xdt.astype(jnp.float32).reshape(bsz, nc, q, g, r, p)
    Bc = B.astype(jnp.float32).reshape(bsz, nc, q, g, n)
    Cc = C.astype(jnp.float32).reshape(bsz, nc, q, g, n)
    a_cs = jnp.cumsum(jnp.transpose(a.astype(jnp.float32).reshape(bsz, nc, q, g, r), (0, 1, 3, 4, 2)), axis=-1)
    diff = a_cs[..., :, None] - a_cs[..., None, :]
    mask = jnp.tril(jnp.ones((q, q), dtype=bool))
    L = jnp.exp(jnp.where(mask, diff, -jnp.inf))
    CB = jnp.einsum("bcign,bcjgn->bcgij", Cc, Bc)
    y_diag = jnp.einsum("bcgrij,bcjgrp->bcigrp", CB[:, :, :, None] * L, x)
    decay_states = jnp.exp(a_cs[..., -1:] - a_cs)
    states = jnp.einsum("bcjgn,bcgrj,bcjgrp->bcgrpn", Bc, decay_states, x)
    chunk_decay = jnp.exp(a_cs[..., -1])

    def step(h, inp):
        dec, st = inp
        return dec[..., None, None] * h + st, h

    h_init = h0.astype(jnp.float32).reshape(bsz, g, r, p, n)
    h_final, h_prev = lax.scan(step, h_init, (jnp.moveaxis(chunk_decay, 1, 0), jnp.moveaxis(states, 1, 0)))
    h_prev = jnp.moveaxis(h_prev, 0, 1)
    y_off = jnp.einsum("bcign,bcgri,bcgrpn->bcigrp", Cc, jnp.exp(a_cs), h_prev)
    y = (y_diag + y_off).reshape(bsz, l, nh, p)
    return y, h_final.reshape(bsz, nh, p, n)


def _token_mixer(h, h0_f, h0_b, lp):
    bsz, l, _ = h.shape
    proj = h @ lp["w_in"]
    z, xbc, dt_raw, u, v, ga, gb = jnp.split(proj, [SPLIT_Z, SPLIT_XBC, SPLIT_DT, SPLIT_U, SPLIT_V, SPLIT_GA], axis=-1)

    xbc = jax.nn.silu(_conv1d_centred(xbc, lp["ssd_conv_w"], lp["ssd_conv_b"]))
    xs, Bm, Cm = jnp.split(xbc, [D_SSD, D_SSD + BC_W], axis=-1)
    xs = xs.reshape(bsz, l, SSD_HEADS, SSD_HEADDIM).astype(jnp.float32)
    Bm = Bm.reshape(bsz, l, SSD_GROUPS, SSD_STATE)
    Cm = Cm.reshape(bsz, l, SSD_GROUPS, SSD_STATE)
    dt = jax.nn.softplus(dt_raw.reshape(bsz, l, 2, SSD_HEADS).astype(jnp.float32)
                         + lp["ssd_dt_bias"].astype(jnp.float32))
    A = -jnp.exp(lp["ssd_a_log"].astype(jnp.float32))
    y_f, hf = _ssd_chunked(xs * dt[:, :, 0, :, None], dt[:, :, 0] * A[0], Bm, Cm, h0_f)
    flip = lambda t: t[:, ::-1]
    y_b, hb = _ssd_chunked(flip(xs * dt[:, :, 1, :, None]), flip(dt[:, :, 1] * A[1]), flip(Bm), flip(Cm), h0_b)
    y = y_f + flip(y_b) + lp["ssd_d"].astype(jnp.float32)[:, None] * xs
    y = y.reshape(bsz, l, D_SSD) * jax.nn.silu(z.astype(jnp.float32))
    y_ssd = _rmsnorm(y, lp["ssd_norm"]).astype(h.dtype)

    vn = _layernorm(v, lp["sgu_norm_w"], lp["sgu_norm_b"])
    vc = vn.reshape(bsz, l // SGU_CHUNK, SGU_CHUNK, SGU_GROUPS, SGU_GROUP_DIM)
    s = jnp.einsum("gij,bcjgd->bcigd", lp["sgu_w"], vc) + jnp.transpose(lp["sgu_b"])[:, :, None]
    y_sgu = u * s.reshape(bsz, l, D_SGU)

    merged = jax.nn.sigmoid(ga) * (y_ssd @ lp["w_branch_ssd"]) + jax.nn.sigmoid(gb) * (y_sgu @ lp["w_branch_sgu"])
    return merged @ lp["w_out"], hf, hb


def _conv_ffn(h, on_grid, lp):
    up = h @ lp["ffn_w_up"]
    if on_grid:
        up = _conv2d_grid(up, lp["ffn_conv_w"], lp["ffn_conv_b"])
    else:
        up = _conv1d_centred(up, lp["ffn_conv_w"][FFN_CONV // 2], lp["ffn_conv_b"])
    a, val = jnp.split(up, 2, axis=-1)
    return (jax.nn.gelu(a) * val) @ lp["ffn_w_down"]


def _trunk_layer(x, mod, h0_f, h0_b, on_grid, lp):
    shift1, scale1, gate1, shift2, scale2, gate2 = jnp.split(mod[:, None, :], 6, axis=-1)
    h = _rmsnorm(x, lp["norm_mix_pre"]) * (1 + scale1) + shift1
    mix, hf, hb = _token_mixer(h, h0_f, h0_b, lp)
    x = x + gate1 * _rmsnorm(mix, lp["norm_mix_post"])
    h = _rmsnorm(x, lp["norm_ffn_pre"]) * (1 + scale2) + shift2
    x = x + gate2 * _rmsnorm(_conv_ffn(h, on_grid, lp), lp["norm_ffn_post"])
    return x, hf, hb


def setup_inputs(seed: int = 0) -> dict:
    key = jax.random.key(seed)
    ks = iter(jax.random.split(key, 40))
    nrm = lambda shape, s: jax.random.normal(next(ks), shape, jnp.float32) * s
    gain = lambda shape: 1.0 + nrm(shape, 0.02)
    dt0 = jnp.exp(jax.random.uniform(next(ks), (DEPTH, 2, SSD_HEADS), jnp.float32,
                                     np.log(1e-3).astype(np.float32), np.log(1e-1).astype(np.float32)))
    st_shape = (DEC_BATCH, DEPTH, SSD_HEADS, SSD_HEADDIM, SSD_STATE)
    return {
        "x_prompt": nrm((BATCH, SEQ, D_MODEL), 1.0),
        "x_sample": nrm((DEC_BATCH, DEC_SEQ, D_MODEL), 1.0),
        "state_ssd_fwd": nrm(st_shape, 0.5),
        "state_ssd_bwd": nrm(st_shape, 0.5),
        "c": nrm((DEC_BATCH, D_MODEL), 1.0),
        "c_ctx": nrm((D_MODEL,), 1.0),
        "w_mod": nrm((DEPTH, D_MODEL, 6 * D_MODEL), D_MODEL ** -0.5),
        "b_mod": nrm((DEPTH, 6 * D_MODEL), 0.02),
        "norm_mix_pre": gain((DEPTH, D_MODEL)),
        "norm_mix_post": gain((DEPTH, D_MODEL)),
        "norm_ffn_pre": gain((DEPTH, D_MODEL)),
        "norm_ffn_post": gain((DEPTH, D_MODEL)),
        "w_in": nrm((DEPTH, D_MODEL, IN_COLS), D_MODEL ** -0.5),
        "ssd_conv_w": nrm((DEPTH, SSD_CONV, CONV_CH), SSD_CONV ** -0.5),
        "ssd_conv_b": nrm((DEPTH, CONV_CH), 0.02),
        "ssd_a_log": jnp.log(jax.random.uniform(next(ks), (DEPTH, 2, SSD_HEADS), jnp.float32, 1.0, 16.0)),
        "ssd_dt_bias": dt0 + jnp.log(-jnp.expm1(-dt0)),
        "ssd_d": gain((DEPTH, SSD_HEADS)),
        "ssd_norm": gain((DEPTH, D_SSD)),
        "sgu_norm_w": gain((DEPTH, D_SGU)),
        "sgu_norm_b": nrm((DEPTH, D_SGU), 0.02),
        "sgu_w": nrm((DEPTH, SGU_GROUPS, SGU_CHUNK, SGU_CHUNK), SGU_CHUNK ** -0.5),
        "sgu_b": gain((DEPTH, SGU_GROUPS, SGU_CHUNK)),
        "w_branch_ssd": nrm((DEPTH, D_SSD, D_MODEL), D_SSD ** -0.5),
        "w_branch_sgu": nrm((DEPTH, D_SGU, D_MODEL), D_SGU ** -0.5),
        "w_out": nrm((DEPTH, D_MODEL, D_MODEL), D_MODEL ** -0.5),
        "ffn_w_up": nrm((DEPTH, D_MODEL, 2 * D_FF), D_MODEL ** -0.5),
        "ffn_conv_w": nrm((DEPTH, FFN_CONV, FFN_CONV, 2 * D_FF), 1.0 / FFN_CONV),
        "ffn_conv_b": nrm((DEPTH, 2 * D_FF), 0.02),
        "ffn_w_down": nrm((DEPTH, D_FF, D_MODEL), D_FF ** -0.5),
    }


def reference(x_prompt, x_sample, state_ssd_fwd, state_ssd_bwd, c, c_ctx, w_mod, b_mod,
              norm_mix_pre, norm_mix_post, norm_ffn_pre, norm_ffn_post, w_in, ssd_conv_w, ssd_conv_b,
              ssd_a_log, ssd_dt_bias, ssd_d, ssd_norm, sgu_norm_w, sgu_norm_b, sgu_w, sgu_b,
              w_branch_ssd, w_branch_sgu, w_out, ffn_w_up, ffn_conv_w, ffn_conv_b, ffn_w_down):
    xp, xs = x_prompt, x_sample
    new_f, new_b = [], []
    for i in range(DEPTH):
        lp = {
            "norm_mix_pre": norm_mix_pre[i], "norm_mix_post": norm_mix_post[i],
            "norm_ffn_pre": norm_ffn_pre[i], "norm_ffn_post": norm_ffn_post[i],
            "w_in": w_in[i], "ssd_conv_w": ssd_conv_w[i], "ssd_conv_b": ssd_conv_b[i],
            "ssd_a_log": ssd_a_log[i], "ssd_dt_bias": ssd_dt_bias[i], "ssd_d": ssd_d[i], "ssd_norm": ssd_norm[i],
            "sgu_norm_w": sgu_norm_w[i], "sgu_norm_b": sgu_norm_b[i], "sgu_w": sgu_w[i], "sgu_b": sgu_b[i],
            "w_branch_ssd": w_branch_ssd[i], "w_branch_sgu": w_branch_sgu[i], "w_out": w_out[i],
            "ffn_w_up": ffn_w_up[i], "ffn_conv_w": ffn_conv_w[i], "ffn_conv_b": ffn_conv_b[i],
            "ffn_w_down": ffn_w_down[i],
        }
        mod_ctx = jax.nn.silu(c_ctx)[None, :] @ w_mod[i] + b_mod[i]
        mod_lat = jax.nn.silu(c) @ w_mod[i] + b_mod[i]
        zero_state = jnp.zeros((xp.shape[0], SSD_HEADS, SSD_HEADDIM, SSD_STATE), jnp.float32)
        xp, hf, hb = _trunk_layer(xp, mod_ctx, zero_state, zero_state, False, lp)
        new_f.append(hf)
        new_b.append(hb)
        xs, _, _ = _trunk_layer(xs, mod_lat, state_ssd_fwd[:, i], state_ssd_bwd[:, i], True, lp)
    new_state_ssd_fwd = jnp.stack(new_f, axis=1).astype(x_prompt.dtype)
    new_state_ssd_bwd = jnp.stack(new_b, axis=1).astype(x_prompt.dtype)
    return (xp, xs, new_state_ssd_fwd, new_state_ssd_bwd)
```

```python
import functools

import jax
import jax.numpy as jnp
from jax import lax
from jax.experimental import pallas as pl
from jax.experimental.pallas import tpu as pltpu

F32 = jnp.float32
BF16 = jnp.bfloat16

D_MODEL = 1024
GRID_W = 64
EPS = 1e-6
D_SSD = 2 * D_MODEL
HEADDIM = 64
HEADS = D_SSD // HEADDIM
STATE = 128
GROUPS = 4
HEADS_PER_GROUP = HEADS // GROUPS
GROUP_W = HEADS_PER_GROUP * HEADDIM
BC_W = GROUPS * STATE
CONV_CH = D_SSD + 2 * BC_W
CHUNK = 128
SGU_GROUPS = 8
D_FF = 2816
SPLIT_XBC = D_SSD + CONV_CH
N_DT = 2 * HEADS
LANES = 128
PROJ_COLS = 9216
FF_BLK = 256
N_FF_BLK = D_FF // FF_BLK

VMEM_LIMIT_BYTES = 56 * 1024 * 1024


def _cparams(sem):
    return pltpu.CompilerParams(dimension_semantics=sem, vmem_limit_bytes=VMEM_LIMIT_BYTES)


def _sigmoid(x):
    return 1.0 / (1.0 + jnp.exp(-x))


def _silu(x):
    return x * _sigmoid(x)


def _dot(a, b):
    return jnp.dot(a, b, preferred_element_type=F32)


def _dot_nt(a, b):
    return lax.dot_general(a, b, (((1,), (1,)), ((), ())), preferred_element_type=F32)


def _mod_kernel(c_ref, w_ref, b_ref, o_ref):
    c = c_ref[...]
    o_ref[...] = jnp.dot(_silu(c), w_ref[...], preferred_element_type=F32,
                         precision=lax.Precision.HIGHEST) + b_ref[...]


def _mod_vectors(c_rows, w_mod, b_mod):
    rows = c_rows.shape[0]
    tn = 1024
    return pl.pallas_call(
        _mod_kernel,
        grid=(6 * D_MODEL // tn,),
        in_specs=[pl.BlockSpec((rows, D_MODEL), lambda j: (0, 0)),
                  pl.BlockSpec((D_MODEL, tn), lambda j: (0, j)),
                  pl.BlockSpec((1, tn), lambda j: (0, j))],
        out_specs=pl.BlockSpec((rows, tn), lambda j: (0, j)),
        out_shape=jax.ShapeDtypeStruct((rows, 6 * D_MODEL), F32),
        compiler_params=_cparams(("arbitrary",)),
        name="mod",
    )(c_rows, w_mod, b_mod.reshape(1, -1))


def _modulated_norm(x, nw, shift, scale):
    ms = jnp.mean(x * x, axis=-1, keepdims=True)
    return (x * lax.rsqrt(ms + EPS) * nw) * (1.0 + scale) + shift


def _inproj_kernel(x_ref, mod_ref, nw_ref, w_ref, wdt_ref, o_ref, dt_ref, h_scr):
    @pl.when(pl.program_id(1) == 0)
    def _():
        h = _modulated_norm(x_ref[...], nw_ref[...], mod_ref[0, 0:1, :], mod_ref[0, 1:2, :])
        hb = h.astype(BF16)
        h_scr[...] = hb
        dt_ref[...] = _dot(hb, wdt_ref[...])

    o_ref[...] = _dot(h_scr[...], w_ref[...]).astype(BF16)


def _inproj(x, mod, nw, w_main, w_dt, tokens_per_mod):
    t = x.shape[0]
    tm, tn = 1024, 1024
    mod_row = lambda i, j: ((i * tm) // tokens_per_mod, 0, 0)
    return pl.pallas_call(
        _inproj_kernel,
        grid=(t // tm, PROJ_COLS // tn),
        in_specs=[pl.BlockSpec((tm, D_MODEL), lambda i, j: (i, 0)),
                  pl.BlockSpec((1, 6, D_MODEL), mod_row),
                  pl.BlockSpec((1, D_MODEL), lambda i, j: (0, 0)),
                  pl.BlockSpec((D_MODEL, tn), lambda i, j: (0, j)),
                  pl.BlockSpec((D_MODEL, LANES), lambda i, j: (0, 0))],
        out_specs=[pl.BlockSpec((tm, tn), lambda i, j: (i, j)),
                   pl.BlockSpec((tm, LANES), lambda i, j: (i, 0))],
        out_shape=[jax.ShapeDtypeStruct((t, PROJ_COLS), BF16),
                   jax.ShapeDtypeStruct((t, LANES), F32)],
        scratch_shapes=[pltpu.VMEM((tm, D_MODEL), BF16)],
        compiler_params=_cparams(("arbitrary", "arbitrary")),
        name="inproj",
    )(x, mod, nw, w_main, w_dt)


def _ffn_up_kernel(x_ref, mod_ref, nw_ref, w_ref, o_ref, h_scr):
    @pl.when(pl.program_id(1) == 0)
    def _():
        h = _modulated_norm(x_ref[...], nw_ref[...], mod_ref[0, 3:4, :], mod_ref[0, 4:5, :])
        h_scr[...] = h.astype(BF16)

    o_ref[...] = _dot(h_scr[...], w_ref[...]).astype(BF16)


def _ffn_up(x, mod, nw, w_up, tokens_per_mod):
    t = x.shape[0]
    tm, tn = 1024, 512
    mod_row = lambda i, j: ((i * tm) // tokens_per_mod, 0, 0)
    return pl.pallas_call(
        _ffn_up_kernel,
        grid=(t // tm, 2 * D_FF // tn),
        in_specs=[pl.BlockSpec((tm, D_MODEL), lambda i, j: (i, 0)),
                  pl.BlockSpec((1, 6, D_MODEL), mod_row),
                  pl.BlockSpec((1, D_MODEL), lambda i, j: (0, 0)),
                  pl.BlockSpec((D_MODEL, tn), lambda i, j: (0, j))],
        out_specs=pl.BlockSpec((tm, tn), lambda i, j: (i, j)),
        out_shape=jax.ShapeDtypeStruct((t, 2 * D_FF), BF16),
        scratch_shapes=[pltpu.VMEM((tm, D_MODEL), BF16)],
        compiler_params=_cparams(("arbitrary", "arbitrary")),
        name="ffn_up",
    )(x, mod, nw, w_up)


def _softplus(x):
    return jnp.maximum(x, 0.0) + jnp.log1p(jnp.exp(-jnp.abs(x)))


def _cumsum_rows(a):
    n = a.shape[0]
    rid = lax.broadcasted_iota(jnp.int32, a.shape, 0)
    s = 1
    while s < n:
        a = a + jnp.where(rid >= s, pltpu.roll(a, s, axis=0), 0.0)
        s *= 2
    return a


def _rev_cumsum_rows(a):
    n = a.shape[0]
    rid = lax.broadcasted_iota(jnp.int32, a.shape, 0)
    s = 1
    while s < n:
        a = a + jnp.where(rid < n - s, pltpu.roll(a, n - s, axis=0), 0.0)
        s *= 2
    return a


def _expand_heads(w, e_ref):
    hi = w.astype(BF16)
    lo = (w - hi.astype(F32)).astype(BF16)
    e = e_ref[...]
    return _dot(hi, e) + _dot(lo, e)


def _conv3_silu(main_ref, prev_ref, next_ref, w_ref, b_ref, first, last):
    x = main_ref[...].astype(F32)
    n = x.shape[0]
    prow = prev_ref[...].astype(F32)[-1:, :]
    nrow = next_ref[...].astype(F32)[0:1, :]
    prow = jnp.where(first, 0.0, prow)
    nrow = jnp.where(last, 0.0, nrow)
    rid = lax.broadcasted_iota(jnp.int32, x.shape, 0)
    xm1 = jnp.where(rid == 0, prow, pltpu.roll(x, 1, axis=0))
    xp1 = jnp.where(rid == n - 1, nrow, pltpu.roll(x, n - 1, axis=0))
    w = w_ref[...]
    y = w[0:1, :] * xm1 + w[1:2, :] * x + w[2:3, :] * xp1 + b_ref[...]
    return _silu(y)


def _dt_and_decay_rates(dt_ref, dtb_ref, alog_ref):
    dt = _softplus(dt_ref[...] + dtb_ref[...])
    a = dt * (-jnp.exp(alog_ref[...]))
    return dt, a


def _chunk_state(b_f32, xd_bf):
    parts = []
    for g in range(GROUPS):
        bt = b_f32[:, g * STATE:(g + 1) * STATE].T.astype(BF16)
        parts.append(_dot(bt, xd_bf[:, g * GROUP_W:(g + 1) * GROUP_W]))
    return jnp.concatenate(parts, axis=1)


def _ssd_fwd_kernel(has_h0, *refs):
    (xm_ref, bcm_ref, xp_ref, bcp_ref, xn_ref, bcn_ref, dt_ref,
     cwx_ref, cwbc_ref, cbx_ref, cbbc_ref, dtb_ref, alog_ref, ef_ref) = refs[:14]
    pos = 14
    h0_ref = None
    if has_h0:
        h0_ref = refs[pos]
        pos += 1
    xs_ref, bc_ref, hprev_ref, hfin_ref, h_scr = refs[pos:pos + 5]

    c = pl.program_id(1)
    nchunks = pl.num_programs(1)
    first = c == 0
    last = c == nchunks - 1

    @pl.when(first)
    def _():
        if has_h0:
            h_scr[...] = h0_ref[0].T
        else:
            h_scr[...] = jnp.zeros_like(h_scr)

    xs = _conv3_silu(xm_ref, xp_ref, xn_ref, cwx_ref, cbx_ref, first, last)
    bc = _conv3_silu(bcm_ref, bcp_ref, bcn_ref, cwbc_ref, cbbc_ref, first, last)
    xs_ref[...] = xs.astype(BF16)
    bc_ref[...] = bc.astype(BF16)

    dt, a = _dt_and_decay_rates(dt_ref, dtb_ref, alog_ref)
    acs = _cumsum_rows(a)
    tot = acs[CHUNK - 1:CHUNK, :]
    w2 = dt * jnp.exp(tot - acs)
    stacked = jnp.concatenate([w2, jnp.broadcast_to(jnp.exp(tot), (8, LANES))], axis=0)
    ex = _expand_heads(stacked, ef_ref)
    w2e = ex[:CHUNK, :]
    cdec = ex[CHUNK:CHUNK + 1, :]

    h_prev = h_scr[...]
    hprev_ref[0] = h_prev.astype(BF16)
    s_loc = _chunk_state(bc[:, :BC_W], (xs * w2e).astype(BF16))
    h_new = cdec * h_prev + s_loc
    h_scr[...] = h_new

    @pl.when(last)
    def _():
        hfin_ref[0] = h_new.T


def _ssd_fwd(proj, dt_raw, h0, consts, nseq, nchunks):
    t = nseq * nchunks * CHUNK
    n16 = t // 16
    gidx = lambda b, c: b * nchunks + c
    in_specs = [
        pl.BlockSpec((CHUNK, D_SSD), lambda b, c: (gidx(b, c), 1)),
        pl.BlockSpec((CHUNK, 2 * BC_W), lambda b, c: (gidx(b, c), 4)),
        pl.BlockSpec((16, D_SSD), lambda b, c: (jnp.maximum(gidx(b, c) * 8 - 1, 0), 1)),
        pl.BlockSpec((16, 2 * BC_W), lambda b, c: (jnp.maximum(gidx(b, c) * 8 - 1, 0), 4)),
        pl.BlockSpec((16, D_SSD), lambda b, c: (jnp.minimum((gidx(b, c) + 1) * 8, n16 - 1), 1)),
        pl.BlockSpec((16, 2 * BC_W), lambda b, c: (jnp.minimum((gidx(b, c) + 1) * 8, n16 - 1), 4)),
        pl.BlockSpec((CHUNK, LANES), lambda b, c: (gidx(b, c), 0)),
        pl.BlockSpec((3, D_SSD), lambda b, c: (0, 0)),
        pl.BlockSpec((3, 2 * BC_W), lambda b, c: (0, 0)),
        pl.BlockSpec((1, D_SSD), lambda b, c: (0, 0)),
        pl.BlockSpec((1, 2 * BC_W), lambda b, c: (0, 0)),
        pl.BlockSpec((1, LANES), lambda b, c: (0, 0)),
        pl.BlockSpec((1, LANES), lambda b, c: (0, 0)),
        pl.BlockSpec((LANES, D_SSD), lambda b, c: (0, 0)),
    ]
    args = [proj, proj, proj, proj, proj, proj, dt_raw,
            consts["cw_x"], consts["cw_bc"], consts["cb_x"], consts["cb_bc"],
            consts["dt_bias"], consts["a_log"], consts["e_fwd"]]
    has_h0 = h0 is not None
    if has_h0:
        in_specs.append(pl.BlockSpec((1, D_SSD, STATE), lambda b, c: (b, 0, 0)))
        args.append(h0)
    return pl.pallas_call(
        functools.partial(_ssd_fwd_kernel, has_h0),
        grid=(nseq, nchunks),
        in_specs=in_specs,
        out_specs=[pl.BlockSpec((CHUNK, D_SSD), lambda b, c: (gidx(b, c), 0)),
                   pl.BlockSpec((CHUNK, 2 * BC_W), lambda b, c: (gidx(b, c), 0)),
                   pl.BlockSpec((1, STATE, D_SSD), lambda b, c: (gidx(b, c), 0, 0)),
                   pl.BlockSpec((1, D_SSD, STATE), lambda b, c: (b, 0, 0))],
        out_shape=[jax.ShapeDtypeStruct((t, D_SSD), BF16),
                   jax.ShapeDtypeStruct((t, 2 * BC_W), BF16),
                   jax.ShapeDtypeStruct((nseq * nchunks, STATE, D_SSD), BF16),
                   jax.ShapeDtypeStruct((nseq, D_SSD, STATE), F32)],
        scratch_shapes=[pltpu.VMEM((STATE, D_SSD), F32)],
        compiler_params=_cparams(("arbitrary", "arbitrary")),
        name="ssd_fwd",
    )(*args)


def _ssd_bwd_kernel(has_h0, *refs):
    (xs_ref, bc_ref, z_ref, dt_ref, hprev_ref, dtb_ref, alog_ref, dvec_ref, nw_ref,
     ef_ref, eb_ref) = refs[:11]
    pos = 11
    h0_ref = None
    if has_h0:
        h0_ref = refs[pos]
        pos += 1
    y_ref, hfin_ref, h_scr = refs[pos:pos + 3]

    c = pl.program_id(1)
    nchunks = pl.num_programs(1)

    @pl.when(c == 0)
    def _():
        if has_h0:
            h_scr[...] = h0_ref[0].T
        else:
            h_scr[...] = jnp.zeros_like(h_scr)

    xs_bf = xs_ref[...]
    xs = xs_bf.astype(F32)
    bc_bf = bc_ref[...]

    dt, a = _dt_and_decay_rates(dt_ref, dtb_ref, alog_ref)
    acs = _cumsum_rows(a)
    rcs = _rev_cumsum_rows(a)
    acs_t = acs.T
    rcs_t = rcs.T
    dt_t = dt.T

    ri = lax.broadcasted_iota(jnp.int32, (CHUNK, CHUNK), 0)
    ci = lax.broadcasted_iota(jnp.int32, (CHUNK, CHUNK), 1)
    lower = ri >= ci
    strict_lower = ri > ci
    strict_upper = ri < ci
    lane = lax.broadcasted_iota(jnp.int32, (CHUNK, LANES), 1)
    left = lane < HEADDIM

    h_f = hprev_ref[0]
    h_b = h_scr[...]
    h_b_bf = h_b.astype(BF16)

    y_parts = []
    for g in range(GROUPS):
        b_g = bc_bf[:, g * STATE:(g + 1) * STATE]
        c_g = bc_bf[:, BC_W + g * STATE:BC_W + (g + 1) * STATE]
        cb = _dot_nt(c_g, b_g)
        for k in range(HEADS_PER_GROUP // 2):
            ms = []
            for h in (g * HEADS_PER_GROUP + 2 * k, g * HEADS_PER_GROUP + 2 * k + 1):
                df = acs[:, h:h + 1] - acs_t[h:h + 1, :]
                db = rcs[:, HEADS + h:HEADS + h + 1] - rcs_t[HEADS + h:HEADS + h + 1, :]
                e = jnp.exp(jnp.where(lower, df, db))
                dtf = dt_t[h:h + 1, :]
                dtb = dt_t[HEADS + h:HEADS + h + 1, :]
                wdt = jnp.where(strict_lower, dtf, jnp.where(strict_upper, dtb, dtf + dtb))
                ms.append((cb * e * wdt).astype(BF16))
            pair = g * HEADS_PER_GROUP // 2 + k
            xp = xs_bf[:, pair * LANES:(pair + 1) * LANES]
            zero = jnp.zeros_like(xp)
            rhs = jnp.concatenate([jnp.where(left, xp, zero), jnp.where(left, zero, xp)], axis=0)
            y_parts.append(_dot(jnp.concatenate(ms, axis=1), rhs))
    y = jnp.concatenate(y_parts, axis=1)

    off_f = []
    off_b = []
    for g in range(GROUPS):
        c_g = bc_bf[:, BC_W + g * STATE:BC_W + (g + 1) * STATE]
        off_f.append(_dot(c_g, h_f[:, g * GROUP_W:(g + 1) * GROUP_W]))
        off_b.append(_dot(c_g, h_b_bf[:, g * GROUP_W:(g + 1) * GROUP_W]))
    rtot = rcs[0:1, :]
    w2b = dt * jnp.exp(rtot - rcs)
    ef = _expand_heads(jnp.exp(acs), ef_ref)
    ebx = _expand_heads(jnp.concatenate([jnp.exp(rcs), w2b, jnp.broadcast_to(jnp.exp(rtot), (8, LANES))], axis=0),
                        eb_ref)
    eb = ebx[:CHUNK, :]
    w2be = ebx[CHUNK:2 * CHUNK, :]
    cdec = ebx[2 * CHUNK:2 * CHUNK + 1, :]

    y = y + ef * jnp.concatenate(off_f, axis=1) + eb * jnp.concatenate(off_b, axis=1) + dvec_ref[...] * xs
    y = y * _silu(z_ref[...].astype(F32))
    ms_y = jnp.mean(y * y, axis=-1, keepdims=True)
    y_ref[...] = (y * lax.rsqrt(ms_y + EPS) * nw_ref[...]).astype(BF16)

    s_loc = _chunk_state(bc_bf[:, :BC_W].astype(F32), (xs * w2be).astype(BF16))
    h_new = cdec * h_b + s_loc
    h_scr[...] = h_new

    @pl.when(c == nchunks - 1)
    def _():
        hfin_ref[0] = h_new.T


def _ssd_bwd(xs, bc, proj, dt_raw, hprev, h0, consts, nseq, nchunks):
    t = nseq * nchunks * CHUNK
    gidx = lambda b, c: b * nchunks + (nchunks - 1 - c)
    in_specs = [
        pl.BlockSpec((CHUNK, D_SSD), lambda b, c: (gidx(b, c), 0)),
        pl.BlockSpec((CHUNK, 2 * BC_W), lambda b, c: (gidx(b, c), 0)),
        pl.BlockSpec((CHUNK, D_SSD), lambda b, c: (gidx(b, c), 0)),
        pl.BlockSpec((CHUNK, LANES), lambda b, c: (gidx(b, c), 0)),
        pl.BlockSpec((1, STATE, D_SSD), lambda b, c: (gidx(b, c), 0, 0)),
        pl.BlockSpec((1, LANES), lambda b, c: (0, 0)),
        pl.BlockSpec((1, LANES), lambda b, c: (0, 0)),
        pl.BlockSpec((1, D_SSD), lambda b, c: (0, 0)),
        pl.BlockSpec((1, D_SSD), lambda b, c: (0, 0)),
        pl.BlockSpec((LANES, D_SSD), lambda b, c: (0, 0)),
        pl.BlockSpec((LANES, D_SSD), lambda b, c: (0, 0)),
    ]
    args = [xs, bc, proj, dt_raw, hprev, consts["dt_bias"], consts["a_log"], consts["d_vec"],
            consts["ssd_norm"], consts["e_fwd"], consts["e_bwd"]]
    has_h0 = h0 is not None
    if has_h0:
        in_specs.append(pl.BlockSpec((1, D_SSD, STATE), lambda b, c: (b, 0, 0)))
        args.append(h0)
    return pl.pallas_call(
        functools.partial(_ssd_bwd_kernel, has_h0),
        grid=(nseq, nchunks),
        in_specs=in_specs,
        out_specs=[pl.BlockSpec((CHUNK, D_SSD), lambda b, c: (gidx(b, c), 0)),
                   pl.BlockSpec((1, D_SSD, STATE), lambda b, c: (b, 0, 0))],
        out_shape=[jax.ShapeDtypeStruct((t, D_SSD), BF16),
                   jax.ShapeDtypeStruct((nseq, D_SSD, STATE), F32)],
        scratch_shapes=[pltpu.VMEM((STATE, D_SSD), F32)],
        compiler_params=_cparams(("arbitrary", "arbitrary")),
        name="ssd_bwd",
    )(*args)


def _mix_kernel(y_ref, u_ref, v_ref, ga_ref, gb_ref, x_ref, mod_ref, lnw_ref, lnb_ref, ws_ref, bst_ref,
                wbs_ref, wbg_ref, wout_ref, npost_ref, o_ref, ysgu_scr):
    tm = x_ref.shape[0]
    v = v_ref[...].astype(F32)
    mu = jnp.mean(v, axis=-1, keepdims=True)
    vc = v - mu
    var = jnp.mean(vc * vc, axis=-1, keepdims=True)
    vn = (vc * lax.rsqrt(var + EPS) * lnw_ref[...] + lnb_ref[...]).astype(BF16)
    bst = bst_ref[...]
    for r in range(tm // CHUNK):
        rows = slice(r * CHUNK, (r + 1) * CHUNK)
        for g in range(SGU_GROUPS):
            cols = slice(g * LANES, (g + 1) * LANES)
            s = _dot(ws_ref[g], vn[rows, cols]) + bst[:, g:g + 1]
            ysgu_scr[rows, cols] = (u_ref[rows, cols].astype(F32) * s).astype(BF16)
    br_ssd = _dot(y_ref[...], wbs_ref[...])
    br_sgu = _dot(ysgu_scr[...], wbg_ref[...])
    merged = _sigmoid(ga_ref[...].astype(F32)) * br_ssd + _sigmoid(gb_ref[...].astype(F32)) * br_sgu
    mix = _dot(merged.astype(BF16), wout_ref[...])
    ms = jnp.mean(mix * mix, axis=-1, keepdims=True)
    o_ref[...] = x_ref[...] + mod_ref[0, 2:3, :] * (mix * lax.rsqrt(ms + EPS) * npost_ref[...])


def _mix(y_ssd, proj, x, mod, consts, tokens_per_mod):
    t = x.shape[0]
    tm = 512
    row = lambda i: (i, 0)
    const2 = lambda i: (0, 0)
    return pl.pallas_call(
        _mix_kernel,
        grid=(t // tm,),
        in_specs=[pl.BlockSpec((tm, D_SSD), row),
                  pl.BlockSpec((tm, D_MODEL), lambda i: (i, 5)),
                  pl.BlockSpec((tm, D_MODEL), lambda i: (i, 6)),
                  pl.BlockSpec((tm, D_MODEL), lambda i: (i, 7)),
                  pl.BlockSpec((tm, D_MODEL), lambda i: (i, 8)),
                  pl.BlockSpec((tm, D_MODEL), row),
                  pl.BlockSpec((1, 6, D_MODEL), lambda i: ((i * tm) // tokens_per_mod, 0, 0)),
                  pl.BlockSpec((1, D_MODEL), const2),
                  pl.BlockSpec((1, D_MODEL), const2),
                  pl.BlockSpec((SGU_GROUPS, CHUNK, CHUNK), lambda i: (0, 0, 0)),
                  pl.BlockSpec((CHUNK, SGU_GROUPS), const2),
                  pl.BlockSpec((D_SSD, D_MODEL), const2),
                  pl.BlockSpec((D_MODEL, D_MODEL), const2),
                  pl.BlockSpec((D_MODEL, D_MODEL), const2),
                  pl.BlockSpec((1, D_MODEL), const2)],
        out_specs=pl.BlockSpec((tm, D_MODEL), row),
        out_shape=jax.ShapeDtypeStruct((t, D_MODEL), F32),
        scratch_shapes=[pltpu.VMEM((tm, D_MODEL), BF16)],
        compiler_params=_cparams(("arbitrary",)),
        name="mix",
    )(y_ssd, proj, proj, proj, proj, x, mod, consts["sgu_norm_w"], consts["sgu_norm_b"], consts["sgu_w"],
      consts["sgu_bt"], consts["w_branch_ssd"], consts["w_branch_sgu"], consts["w_out"], consts["norm_mix_post"])


def _gelu_tanh(x):
    return 0.5 * x * (1.0 + jnp.tanh(0.7978845608028654 * (x + 0.044715 * x * x * x)))


def _grid_conv(main_ref, prev_ref, next_ref, w, b, first, last):
    x = main_ref[...].astype(F32)
    tm = x.shape[0]
    prev = jnp.where(first, 0.0, prev_ref[...].astype(F32))
    nxt = jnp.where(last, 0.0, next_ref[...].astype(F32))
    ext = jnp.concatenate([prev, x, nxt], axis=0)
    n = ext.shape[0]
    col = lax.broadcasted_iota(jnp.int32, ext.shape, 0) % GRID_W
    shifted = (jnp.where(col == 0, 0.0, pltpu.roll(ext, 1, axis=0)),
               ext,
               jnp.where(col == GRID_W - 1, 0.0, pltpu.roll(ext, n - 1, axis=0)))
    acc = None
    for dy in range(3):
        for dx in range(3):
            term = w[3 * dy + dx:3 * dy + dx + 1, :] * shifted[dx][dy * GRID_W:dy * GRID_W + tm, :]
            acc = term if acc is None else acc + term
    return acc + b


def _seq_conv(main_ref, w, b, seq_len):
    x = main_ref[...].astype(F32)
    n = x.shape[0]
    pos = lax.broadcasted_iota(jnp.int32, x.shape, 0) % seq_len
    xm1 = jnp.where(pos == 0, 0.0, pltpu.roll(x, 1, axis=0))
    xp1 = jnp.where(pos == seq_len - 1, 0.0, pltpu.roll(x, n - 1, axis=0))
    return w[3:4, :] * xm1 + w[4:5, :] * x + w[5:6, :] * xp1 + b


def _ffn_down_kernel(on_grid, seq_len, tiles_per_seq, *refs):
    if on_grid:
        (a_ref, v_ref, ap_ref, vp_ref, an_ref, vn_ref, wa_ref, wv_ref, ba_ref, bv_ref,
         wd_ref, x_ref, mod_ref, npost_ref, o_ref, acc_scr) = refs
    else:
        (a_ref, v_ref, wa_ref, wv_ref, ba_ref, bv_ref,
         wd_ref, x_ref, mod_ref, npost_ref, o_ref, acc_scr) = refs
    i = pl.program_id(0)
    k = pl.program_id(1)

    if on_grid:
        first = i % tiles_per_seq == 0
        last = i % tiles_per_seq == tiles_per_seq - 1
        a = _grid_conv(a_ref, ap_ref, an_ref, wa_ref[...], ba_ref[...], first, last)
        val = _grid_conv(v_ref, vp_ref, vn_ref, wv_ref[...], bv_ref[...], first, last)
    else:
        a = _seq_conv(a_ref, wa_ref[...], ba_ref[...], seq_len)
        val = _seq_conv(v_ref, wv_ref[...], bv_ref[...], seq_len)
    part = _dot((_gelu_tanh(a) * val).astype(BF16), wd_ref[...])

    @pl.when(k == 0)
    def _():
        acc_scr[...] = part

    @pl.when(k > 0)
    def _():
        acc_scr[...] += part

    @pl.when(k == N_FF_BLK - 1)
    def _():
        f = acc_scr[...]
        ms = jnp.mean(f * f, axis=-1, keepdims=True)
        o_ref[...] = x_ref[...] + mod_ref[0, 5:6, :] * (f * lax.rsqrt(ms + EPS) * npost_ref[...])


def _ffn_down(up, x, mod, consts, tokens_per_mod, on_grid, seq_len):
    t = x.shape[0]
    tm = 1024
    tiles_per_seq = max(seq_len // tm, 1)
    rows_per_tile = tm // GRID_W
    n_rows = t // GRID_W
    a_blk = lambda i, k: (i, k)
    v_blk = lambda i, k: (i, N_FF_BLK + k)
    in_specs = [pl.BlockSpec((tm, FF_BLK), a_blk), pl.BlockSpec((tm, FF_BLK), v_blk)]
    args = [up, up]
    if on_grid:
        prev_row = lambda i: jnp.maximum(i * rows_per_tile - 1, 0)
        next_row = lambda i: jnp.minimum((i + 1) * rows_per_tile, n_rows - 1)
        in_specs += [pl.BlockSpec((GRID_W, FF_BLK), lambda i, k: (prev_row(i), k)),
                     pl.BlockSpec((GRID_W, FF_BLK), lambda i, k: (prev_row(i), N_FF_BLK + k)),
                     pl.BlockSpec((GRID_W, FF_BLK), lambda i, k: (next_row(i), k)),
                     pl.BlockSpec((GRID_W, FF_BLK), lambda i, k: (next_row(i), N_FF_BLK + k))]
        args += [up, up, up, up]
    in_specs += [pl.BlockSpec((9, FF_BLK), lambda i, k: (0, k)),
                 pl.BlockSpec((9, FF_BLK), lambda i, k: (0, N_FF_BLK + k)),
                 pl.BlockSpec((1, FF_BLK), lambda i, k: (0, k)),
                 pl.BlockSpec((1, FF_BLK), lambda i, k: (0, N_FF_BLK + k)),
                 pl.BlockSpec((FF_BLK, D_MODEL), lambda i, k: (k, 0)),
                 pl.BlockSpec((tm, D_MODEL), lambda i, k: (i, 0)),
                 pl.BlockSpec((1, 6, D_MODEL), lambda i, k: ((i * tm) // tokens_per_mod, 0, 0)),
                 pl.BlockSpec((1, D_MODEL), lambda i, k: (0, 0))]
    args += [consts["ffn_conv_w"], consts["ffn_conv_w"], consts["ffn_conv_b"], consts["ffn_conv_b"],
             consts["ffn_w_down"], x, mod, consts["norm_ffn_post"]]
    return pl.pallas_call(
        functools.partial(_ffn_down_kernel, on_grid, seq_len, tiles_per_seq),
        grid=(t // tm, N_FF_BLK),
        in_specs=in_specs,
        out_specs=pl.BlockSpec((tm, D_MODEL), lambda i, k: (i, 0)),
        out_shape=jax.ShapeDtypeStruct((t, D_MODEL), F32),
        scratch_shapes=[pltpu.VMEM((tm, D_MODEL), F32)],
        compiler_params=_cparams(("arbitrary", "arbitrary")),
        name="ffn_down",
    )(*args)


def _trunk_path(x, mod, h0_f, h0_b, on_grid, consts):
    nseq, seq_len, _ = x.shape
    t = nseq * seq_len
    nchunks = seq_len // CHUNK
    tokens_per_mod = t // mod.shape[0]
    x2d = x.reshape(t, D_MODEL)

    proj, dt_raw = _inproj(x2d, mod, consts["norm_mix_pre"], consts["w_in_main"], consts["w_in_dt"], tokens_per_mod)
    xs, bc, hprev, hf = _ssd_fwd(proj, dt_raw, h0_f, consts, nseq, nchunks)
    y_ssd, hb = _ssd_bwd(xs, bc, proj, dt_raw, hprev, h0_b, consts, nseq, nchunks)
    x1 = _mix(y_ssd, proj, x2d, mod, consts, tokens_per_mod)
    up = _ffn_up(x1, mod, consts["norm_ffn_pre"], consts["ffn_w_up"], tokens_per_mod)
    x2 = _ffn_down(up, x1, mod, consts, tokens_per_mod, on_grid, seq_len)
    return x2.reshape(nseq, seq_len, D_MODEL), hf, hb


def _head_expansion(offset):
    rows = jnp.arange(LANES)[:, None]
    cols = jnp.arange(D_SSD)[None, :] // HEADDIM
    return (rows == cols + offset).astype(BF16)


def _layer_consts(i, p):
    w_in = p["w_in"][i]
    row = lambda v: v.reshape(1, -1).astype(F32)
    pad_lanes = lambda v: jnp.pad(v.reshape(1, -1).astype(F32), ((0, 0), (0, LANES - N_DT)))
    conv_w = p["ssd_conv_w"][i]
    conv_b = p["ssd_conv_b"][i]
    return {
        "norm_mix_pre": row(p["norm_mix_pre"][i]),
        "norm_mix_post": row(p["norm_mix_post"][i]),
        "norm_ffn_pre": row(p["norm_ffn_pre"][i]),
        "norm_ffn_post": row(p["norm_ffn_post"][i]),
        "w_in_main": jnp.concatenate([w_in[:, :SPLIT_XBC], w_in[:, SPLIT_XBC + N_DT:]], axis=1).astype(BF16),
        "w_in_dt": jnp.pad(w_in[:, SPLIT_XBC:SPLIT_XBC + N_DT], ((0, 0), (0, LANES - N_DT))).astype(BF16),
        "cw_x": conv_w[:, :D_SSD], "cw_bc": conv_w[:, D_SSD:],
        "cb_x": row(conv_b[:D_SSD]), "cb_bc": row(conv_b[D_SSD:]),
        "dt_bias": pad_lanes(p["ssd_dt_bias"][i]),
        "a_log": pad_lanes(p["ssd_a_log"][i]),
        "d_vec": row(jnp.repeat(p["ssd_d"][i], HEADDIM)),
        "ssd_norm": row(p["ssd_norm"][i]),
        "e_fwd": _head_expansion(0), "e_bwd": _head_expansion(HEADS),
        "sgu_norm_w": row(p["sgu_norm_w"][i]), "sgu_norm_b": row(p["sgu_norm_b"][i]),
        "sgu_w": p["sgu_w"][i].astype(BF16),
        "sgu_bt": jnp.transpose(p["sgu_b"][i]).astype(F32),
        "w_branch_ssd": p["w_branch_ssd"][i].astype(BF16),
        "w_branch_sgu": p["w_branch_sgu"][i].astype(BF16),
        "w_out": p["w_out"][i].astype(BF16),
        "ffn_w_up": p["ffn_w_up"][i].astype(BF16),
        "ffn_conv_w": p["ffn_conv_w"][i].reshape(9, 2 * D_FF).astype(F32),
        "ffn_conv_b": row(p["ffn_conv_b"][i]),
        "ffn_w_down": p["ffn_w_down"][i].astype(BF16),
    }


def kernel(x_prompt, x_sample, state_ssd_fwd, state_ssd_bwd, c, c_ctx, w_mod, b_mod, norm_mix_pre, norm_mix_post, norm_ffn_pre, norm_ffn_post, w_in, ssd_conv_w, ssd_conv_b, ssd_a_log, ssd_dt_bias, ssd_d, ssd_norm, sgu_norm_w, sgu_norm_b, sgu_w, sgu_b, w_branch_ssd, w_branch_sgu, w_out, ffn_w_up, ffn_conv_w, ffn_conv_b, ffn_w_down):
    params = dict(norm_mix_pre=norm_mix_pre, norm_mix_post=norm_mix_post, norm_ffn_pre=norm_ffn_pre,
                  norm_ffn_post=norm_ffn_post, w_in=w_in, ssd_conv_w=ssd_conv_w, ssd_conv_b=ssd_conv_b,
                  ssd_a_log=ssd_a_log, ssd_dt_bias=ssd_dt_bias, ssd_d=ssd_d, ssd_norm=ssd_norm,
                  sgu_norm_w=sgu_norm_w, sgu_norm_b=sgu_norm_b, sgu_w=sgu_w, sgu_b=sgu_b,
                  w_branch_ssd=w_branch_ssd, w_branch_sgu=w_branch_sgu, w_out=w_out, ffn_w_up=ffn_w_up,
                  ffn_conv_w=ffn_conv_w, ffn_conv_b=ffn_conv_b, ffn_w_down=ffn_w_down)
    depth = w_mod.shape[0]
    n_lat = c.shape[0]
    c_rows = jnp.concatenate([c_ctx[None, :], c, jnp.zeros((8 - 1 - n_lat, D_MODEL), F32)], axis=0)
    xp, xs = x_prompt, x_sample
    new_f, new_b = [], []
    for i in range(depth):
        consts = _layer_consts(i, params)
        mod = _mod_vectors(c_rows, w_mod[i], b_mod[i]).reshape(8, 6, D_MODEL)
        xp, hf, hb = _trunk_path(xp, mod[0:1], None, None, False, consts)
        new_f.append(hf.reshape(-1, HEADS, HEADDIM, STATE))
        new_b.append(hb.reshape(-1, HEADS, HEADDIM, STATE))
        xs, _, _ = _trunk_path(xs, mod[1:1 + n_lat],
                               state_ssd_fwd[:, i].reshape(n_lat, D_SSD, STATE),
                               state_ssd_bwd[:, i].reshape(n_lat, D_SSD, STATE), True, consts)
    return (xp, xs, jnp.stack(new_f, axis=1).astype(x_prompt.dtype), jnp.stack(new_b, axis=1).astype(x_prompt.dtype))
```

```python
import functools

import jax
import jax.numpy as jnp
from jax import lax
from jax.experimental import pallas as pl
from jax.experimental.pallas import tpu as pltpu

F32 = jnp.float32
BF16 = jnp.bfloat16

D_MODEL = 1024
GRID_W = 64
EPS = 1e-6
LOG2E = 1.4426950408889634
D_SSD = 2 * D_MODEL
HEADDIM = 64
HEADS = D_SSD // HEADDIM
STATE = 128
GROUPS = 4
HEADS_PER_GROUP = HEADS // GROUPS
GROUP_W = HEADS_PER_GROUP * HEADDIM
BC_W = GROUPS * STATE
CONV_CH = D_SSD + 2 * BC_W
CHUNK = 128
SGU_GROUPS = 8
D_FF = 2816
SPLIT_XBC = D_SSD + CONV_CH
N_DT = 2 * HEADS
LANES = 128
PROJ_COLS = 9216
FF_BLK = 256
N_FF_BLK = D_FF // FF_BLK

VMEM_LIMIT_BYTES = 56 * 1024 * 1024


def _cparams(sem):
    return pltpu.CompilerParams(dimension_semantics=sem, vmem_limit_bytes=VMEM_LIMIT_BYTES)


def _sigmoid(x):
    return 1.0 / (1.0 + jnp.exp(-x))


def _silu(x):
    return x * _sigmoid(x)


def _dot(a, b):
    return jnp.dot(a, b, preferred_element_type=F32)


def _dot_nt(a, b):
    return lax.dot_general(a, b, (((1,), (1,)), ((), ())), preferred_element_type=F32)


def _mod_kernel(c_ref, w_ref, b_ref, o_ref):
    c = c_ref[...]
    o_ref[...] = jnp.dot(_silu(c), w_ref[...], preferred_element_type=F32,
                         precision=lax.Precision.HIGHEST) + b_ref[...]


def _mod_vectors(c_rows, w_mod, b_mod):
    rows = c_rows.shape[0]
    tn = 1024
    return pl.pallas_call(
        _mod_kernel,
        grid=(6 * D_MODEL // tn,),
        in_specs=[pl.BlockSpec((rows, D_MODEL), lambda j: (0, 0)),
                  pl.BlockSpec((D_MODEL, tn), lambda j: (0, j)),
                  pl.BlockSpec((1, tn), lambda j: (0, j))],
        out_specs=pl.BlockSpec((rows, tn), lambda j: (0, j)),
        out_shape=jax.ShapeDtypeStruct((rows, 6 * D_MODEL), F32),
        compiler_params=_cparams(("arbitrary",)),
        name="mod",
    )(c_rows, w_mod, b_mod.reshape(1, -1))


def _modulated_norm(x, nw, shift, scale):
    ms = jnp.mean(x * x, axis=-1, keepdims=True)
    return (x * lax.rsqrt(ms + EPS) * nw) * (1.0 + scale) + shift


def _inproj_kernel(x_ref, mod_ref, nw_ref, w_ref, wdt_ref, o_ref, dt_ref, h_scr):
    @pl.when(pl.program_id(1) == 0)
    def _():
        h = _modulated_norm(x_ref[...], nw_ref[...], mod_ref[0, 0:1, :], mod_ref[0, 1:2, :])
        hb = h.astype(BF16)
        h_scr[...] = hb
        dt_ref[...] = _dot(hb, wdt_ref[...])

    o_ref[...] = _dot(h_scr[...], w_ref[...]).astype(BF16)


def _inproj(x, mod, nw, w_main, w_dt, tokens_per_mod):
    t = x.shape[0]
    tm, tn = 1024, PROJ_COLS // 4
    mod_row = lambda i, j: ((i * tm) // tokens_per_mod, 0, 0)
    return pl.pallas_call(
        _inproj_kernel,
        grid=(t // tm, PROJ_COLS // tn),
        in_specs=[pl.BlockSpec((tm, D_MODEL), lambda i, j: (i, 0)),
                  pl.BlockSpec((1, 6, D_MODEL), mod_row),
                  pl.BlockSpec((1, D_MODEL), lambda i, j: (0, 0)),
                  pl.BlockSpec((D_MODEL, tn), lambda i, j: (0, j)),
                  pl.BlockSpec((D_MODEL, LANES), lambda i, j: (0, 0))],
        out_specs=[pl.BlockSpec((tm, tn), lambda i, j: (i, j)),
                   pl.BlockSpec((tm, LANES), lambda i, j: (i, 0))],
        out_shape=[jax.ShapeDtypeStruct((t, PROJ_COLS), BF16),
                   jax.ShapeDtypeStruct((t, LANES), F32)],
        scratch_shapes=[pltpu.VMEM((tm, D_MODEL), BF16)],
        compiler_params=_cparams(("arbitrary", "arbitrary")),
        name="inproj",
    )(x, mod, nw, w_main, w_dt)


def _ffn_up_kernel(x_ref, mod_ref, nw_ref, w_ref, o_ref, h_scr):
    @pl.when(pl.program_id(1) == 0)
    def _():
        h = _modulated_norm(x_ref[...], nw_ref[...], mod_ref[0, 3:4, :], mod_ref[0, 4:5, :])
        h_scr[...] = h.astype(BF16)

    o_ref[...] = _dot(h_scr[...], w_ref[...]).astype(BF16)


def _ffn_up(x, mod, nw, w_up, tokens_per_mod):
    t = x.shape[0]
    tm, tn = 1024, 2 * D_FF // 4
    mod_row = lambda i, j: ((i * tm) // tokens_per_mod, 0, 0)
    return pl.pallas_call(
        _ffn_up_kernel,
        grid=(t // tm, 2 * D_FF // tn),
        in_specs=[pl.BlockSpec((tm, D_MODEL), lambda i, j: (i, 0)),
                  pl.BlockSpec((1, 6, D_MODEL), mod_row),
                  pl.BlockSpec((1, D_MODEL), lambda i, j: (0, 0)),
                  pl.BlockSpec((D_MODEL, tn), lambda i, j: (0, j))],
        out_specs=pl.BlockSpec((tm, tn), lambda i, j: (i, j)),
        out_shape=jax.ShapeDtypeStruct((t, 2 * D_FF), BF16),
        scratch_shapes=[pltpu.VMEM((tm, D_MODEL), BF16)],
        compiler_params=_cparams(("arbitrary", "arbitrary")),
        name="ffn_up",
    )(x, mod, nw, w_up)


def _softplus(x):
    return jnp.maximum(x, 0.0) + jnp.log1p(jnp.exp(-jnp.abs(x)))


def _cumsum_rows(a):
    n = a.shape[0]
    rid = lax.broadcasted_iota(jnp.int32, a.shape, 0)
    s = 1
    while s < n:
        a = a + jnp.where(rid >= s, pltpu.roll(a, s, axis=0), 0.0)
        s *= 2
    return a


def _rev_cumsum_rows(a):
    n = a.shape[0]
    rid = lax.broadcasted_iota(jnp.int32, a.shape, 0)
    s = 1
    while s < n:
        a = a + jnp.where(rid < n - s, pltpu.roll(a, n - s, axis=0), 0.0)
        s *= 2
    return a


def _expand_heads(w, e_ref):
    hi = w.astype(BF16)
    lo = (w - hi.astype(F32)).astype(BF16)
    e = e_ref[...]
    return _dot(hi, e) + _dot(lo, e)


def _conv3_silu(main_ref, prev_ref, next_ref, w_ref, b_ref, first, last):
    x = main_ref[...].astype(F32)
    n = x.shape[0]
    prow = prev_ref[...].astype(F32)[-1:, :]
    nrow = next_ref[...].astype(F32)[0:1, :]
    prow = jnp.where(first, 0.0, prow)
    nrow = jnp.where(last, 0.0, nrow)
    rid = lax.broadcasted_iota(jnp.int32, x.shape, 0)
    xm1 = jnp.where(rid == 0, prow, pltpu.roll(x, 1, axis=0))
    xp1 = jnp.where(rid == n - 1, nrow, pltpu.roll(x, n - 1, axis=0))
    w = w_ref[...]
    y = w[0:1, :] * xm1 + w[1:2, :] * x + w[2:3, :] * xp1 + b_ref[...]
    return _silu(y)


def _dt_and_decay_rates(dt_ref, dtb_ref, alog_ref):
    dt = _softplus(dt_ref[...] + dtb_ref[...])
    a = dt * (-jnp.exp(alog_ref[...]))
    return dt, a


def _pair_rhs(xs_bf, pair, left):
    xp = xs_bf[:, pair * LANES:(pair + 1) * LANES]
    zero = jnp.zeros_like(xp)
    return jnp.concatenate([jnp.where(left, xp, zero), jnp.where(left, zero, xp)], axis=0)


def _ssd_fwd_kernel(has_h0, *refs):
    (xm_ref, bcm_ref, xp_ref, bcp_ref, xn_ref, bcn_ref, dt_ref,
     cwx_ref, cwbc_ref, cbx_ref, cbbc_ref, dtb_ref, alog_ref, ef_ref) = refs[:14]
    pos = 14
    h0_ref = None
    if has_h0:
        h0_ref = refs[pos]
        pos += 1
    xs_ref, bc_ref, hprev_ref, hfin_ref, h_scr = refs[pos:pos + 5]

    c = pl.program_id(1)
    nchunks = pl.num_programs(1)
    first = c == 0
    last = c == nchunks - 1

    @pl.when(first)
    def _():
        if has_h0:
            h_scr[...] = h0_ref[0].T
        else:
            h_scr[...] = jnp.zeros_like(h_scr)

    xs = _conv3_silu(xm_ref, xp_ref, xn_ref, cwx_ref, cbx_ref, first, last)
    bc = _conv3_silu(bcm_ref, bcp_ref, bcn_ref, cwbc_ref, cbbc_ref, first, last)
    xs_bf = xs.astype(BF16)
    xs_ref[...] = xs_bf
    bc_ref[...] = bc.astype(BF16)

    dt, a = _dt_and_decay_rates(dt_ref, dtb_ref, alog_ref)
    acs = _cumsum_rows(a)
    acs_t = acs.T
    w2_t = dt.T * jnp.exp(acs_t[:, CHUNK - 1:CHUNK] - acs_t)
    cdec = _expand_heads(jnp.broadcast_to(jnp.exp(acs[CHUNK - 1:CHUNK, :]), (8, LANES)), ef_ref)[0:1, :]

    left = lax.broadcasted_iota(jnp.int32, (CHUNK, LANES), 1) < HEADDIM
    parts = []
    for g in range(GROUPS):
        bt = bc[:, g * STATE:(g + 1) * STATE].T
        for k in range(HEADS_PER_GROUP // 2):
            h = g * HEADS_PER_GROUP + 2 * k
            lhs = jnp.concatenate([(bt * w2_t[h:h + 1, :]).astype(BF16),
                                   (bt * w2_t[h + 1:h + 2, :]).astype(BF16)], axis=1)
            parts.append(_dot(lhs, _pair_rhs(xs_bf, h // 2, left)))
    s_loc = jnp.concatenate(parts, axis=1)

    h_prev = h_scr[...]
    hprev_ref[0] = h_prev.astype(BF16)
    h_new = cdec * h_prev + s_loc
    h_scr[...] = h_new

    @pl.when(last)
    def _():
        hfin_ref[0] = h_new.T


def _ssd_fwd(proj, dt_raw, h0, consts, nseq, nchunks):
    t = nseq * nchunks * CHUNK
    n16 = t // 16
    gidx = lambda b, c: b * nchunks + c
    in_specs = [
        pl.BlockSpec((CHUNK, D_SSD), lambda b, c: (gidx(b, c), 1)),
        pl.BlockSpec((CHUNK, 2 * BC_W), lambda b, c: (gidx(b, c), 4)),
        pl.BlockSpec((16, D_SSD), lambda b, c: (jnp.maximum(gidx(b, c) * 8 - 1, 0), 1)),
        pl.BlockSpec((16, 2 * BC_W), lambda b, c: (jnp.maximum(gidx(b, c) * 8 - 1, 0), 4)),
        pl.BlockSpec((16, D_SSD), lambda b, c: (jnp.minimum((gidx(b, c) + 1) * 8, n16 - 1), 1)),
        pl.BlockSpec((16, 2 * BC_W), lambda b, c: (jnp.minimum((gidx(b, c) + 1) * 8, n16 - 1), 4)),
        pl.BlockSpec((CHUNK, LANES), lambda b, c: (gidx(b, c), 0)),
        pl.BlockSpec((3, D_SSD), lambda b, c: (0, 0)),
        pl.BlockSpec((3, 2 * BC_W), lambda b, c: (0, 0)),
        pl.BlockSpec((1, D_SSD), lambda b, c: (0, 0)),
        pl.BlockSpec((1, 2 * BC_W), lambda b, c: (0, 0)),
        pl.BlockSpec((1, LANES), lambda b, c: (0, 0)),
        pl.BlockSpec((1, LANES), lambda b, c: (0, 0)),
        pl.BlockSpec((LANES, D_SSD), lambda b, c: (0, 0)),
    ]
    args = [proj, proj, proj, proj, proj, proj, dt_raw,
            consts["cw_x"], consts["cw_bc"], consts["cb_x"], consts["cb_bc"],
            consts["dt_bias"], consts["a_log"], consts["e_fwd"]]
    has_h0 = h0 is not None
    if has_h0:
        in_specs.append(pl.BlockSpec((1, D_SSD, STATE), lambda b, c: (b, 0, 0)))
        args.append(h0)
    return pl.pallas_call(
        functools.partial(_ssd_fwd_kernel, has_h0),
        grid=(nseq, nchunks),
        in_specs=in_specs,
        out_specs=[pl.BlockSpec((CHUNK, D_SSD), lambda b, c: (gidx(b, c), 0)),
                   pl.BlockSpec((CHUNK, 2 * BC_W), lambda b, c: (gidx(b, c), 0)),
                   pl.BlockSpec((1, STATE, D_SSD), lambda b, c: (gidx(b, c), 0, 0)),
                   pl.BlockSpec((1, D_SSD, STATE), lambda b, c: (b, 0, 0))],
        out_shape=[jax.ShapeDtypeStruct((t, D_SSD), BF16),
                   jax.ShapeDtypeStruct((t, 2 * BC_W), BF16),
                   jax.ShapeDtypeStruct((nseq * nchunks, STATE, D_SSD), BF16),
                   jax.ShapeDtypeStruct((nseq, D_SSD, STATE), F32)],
        scratch_shapes=[pltpu.VMEM((STATE, D_SSD), F32)],
        compiler_params=_cparams(("arbitrary", "arbitrary")),
        name="ssd_fwd",
    )(*args)


def _ssd_bwd_kernel(has_h0, *refs):
    (xs_ref, bc_ref, z_ref, dt_ref, hprev_ref, dtb_ref, alog_ref, dvec_ref, nw_ref,
     eb_ref) = refs[:10]
    pos = 10
    h0_ref = None
    if has_h0:
        h0_ref = refs[pos]
        pos += 1
    y_ref, hfin_ref, h_scr = refs[pos:pos + 3]

    c = pl.program_id(1)
    nchunks = pl.num_programs(1)

    @pl.when(c == 0)
    def _():
        if has_h0:
            h_scr[...] = h0_ref[0].T
        else:
            h_scr[...] = jnp.zeros_like(h_scr)

    xs_bf = xs_ref[...]
    xs = xs_bf.astype(F32)
    bc_bf = bc_ref[...]

    dt, a = _dt_and_decay_rates(dt_ref, dtb_ref, alog_ref)
    acs = _cumsum_rows(a) * LOG2E
    rcs = _rev_cumsum_rows(a) * LOG2E
    acs_t = acs.T
    rcs_t = rcs.T
    dt_t = dt.T
    lg_t = jnp.log2(dt_t)
    rf_t = acs_t - lg_t
    rb_t = rcs_t - lg_t
    w2b_t = dt_t * jnp.exp2(rcs_t[:, 0:1] - rcs_t)
    cdec = _expand_heads(jnp.broadcast_to(jnp.exp2(rcs[0:1, :]), (8, LANES)), eb_ref)[0:1, :]

    ri = lax.broadcasted_iota(jnp.int32, (CHUNK, CHUNK), 0)
    ci = lax.broadcasted_iota(jnp.int32, (CHUNK, CHUNK), 1)
    lower = ri >= ci
    diag = ri == ci
    left = lax.broadcasted_iota(jnp.int32, (CHUNK, LANES), 1) < HEADDIM

    h_f = hprev_ref[0]
    h_b = h_scr[...]
    h_b_bf = h_b.astype(BF16)

    y_parts = []
    s_parts = []
    for g in range(GROUPS):
        b_g = bc_bf[:, g * STATE:(g + 1) * STATE]
        c_g = bc_bf[:, BC_W + g * STATE:BC_W + (g + 1) * STATE]
        cb = _dot_nt(c_g, b_g)
        bt = b_g.astype(F32).T
        off_f = _dot(c_g, h_f[:, g * GROUP_W:(g + 1) * GROUP_W])
        off_b = _dot(c_g, h_b_bf[:, g * GROUP_W:(g + 1) * GROUP_W])
        for k in range(HEADS_PER_GROUP // 2):
            ms, bs, cfs, cbs = [], [], [], []
            for h in (g * HEADS_PER_GROUP + 2 * k, g * HEADS_PER_GROUP + 2 * k + 1):
                hb = HEADS + h
                cf = acs[:, h:h + 1]
                cbk = rcs[:, hb:hb + 1]
                e = jnp.exp2(jnp.where(lower, cf - rf_t[h:h + 1, :], cbk - rb_t[hb:hb + 1, :]))
                e = e + jnp.where(diag, dt_t[hb:hb + 1, :], 0.0)
                ms.append((cb * e).astype(BF16))
                bs.append((bt * w2b_t[hb:hb + 1, :]).astype(BF16))
                cfs.append(cf)
                cbs.append(cbk)
            lhs = jnp.concatenate([jnp.concatenate(ms, axis=1), jnp.concatenate(bs, axis=1)], axis=0)
            out = _dot(lhs, _pair_rhs(xs_bf, g * HEADS_PER_GROUP // 2 + k, left))
            ef = jnp.exp2(jnp.where(left, cfs[0], cfs[1]))
            eb = jnp.exp2(jnp.where(left, cbs[0], cbs[1]))
            cols = slice(k * LANES, (k + 1) * LANES)
            y_parts.append(out[:CHUNK, :] + ef * off_f[:, cols] + eb * off_b[:, cols])
            s_parts.append(out[CHUNK:, :])
    y = jnp.concatenate(y_parts, axis=1) + dvec_ref[...] * xs
    y = y * _silu(z_ref[...].astype(F32))
    ms_y = jnp.mean(y * y, axis=-1, keepdims=True)
    y_ref[...] = (y * lax.rsqrt(ms_y + EPS) * nw_ref[...]).astype(BF16)

    h_new = cdec * h_b + jnp.concatenate(s_parts, axis=1)
    h_scr[...] = h_new

    @pl.when(c == nchunks - 1)
    def _():
        hfin_ref[0] = h_new.T


def _ssd_bwd(xs, bc, proj, dt_raw, hprev, h0, consts, nseq, nchunks):
    t = nseq * nchunks * CHUNK
    gidx = lambda b, c: b * nchunks + (nchunks - 1 - c)
    in_specs = [
        pl.BlockSpec((CHUNK, D_SSD), lambda b, c: (gidx(b, c), 0)),
        pl.BlockSpec((CHUNK, 2 * BC_W), lambda b, c: (gidx(b, c), 0)),
        pl.BlockSpec((CHUNK, D_SSD), lambda b, c: (gidx(b, c), 0)),
        pl.BlockSpec((CHUNK, LANES), lambda b, c: (gidx(b, c), 0)),
        pl.BlockSpec((1, STATE, D_SSD), lambda b, c: (gidx(b, c), 0, 0)),
        pl.BlockSpec((1, LANES), lambda b, c: (0, 0)),
        pl.BlockSpec((1, LANES), lambda b, c: (0, 0)),
        pl.BlockSpec((1, D_SSD), lambda b, c: (0, 0)),
        pl.BlockSpec((1, D_SSD), lambda b, c: (0, 0)),
        pl.BlockSpec((LANES, D_SSD), lambda b, c: (0, 0)),
    ]
    args = [xs, bc, proj, dt_raw, hprev, consts["dt_bias"], consts["a_log"], consts["d_vec"],
            consts["ssd_norm"], consts["e_bwd"]]
    has_h0 = h0 is not None
    if has_h0:
        in_specs.append(pl.BlockSpec((1, D_SSD, STATE), lambda b, c: (b, 0, 0)))
        args.append(h0)
    return pl.pallas_call(
        functools.partial(_ssd_bwd_kernel, has_h0),
        grid=(nseq, nchunks),
        in_specs=in_specs,
        out_specs=[pl.BlockSpec((CHUNK, D_SSD), lambda b, c: (gidx(b, c), 0)),
                   pl.BlockSpec((1, D_SSD, STATE), lambda b, c: (b, 0, 0))],
        out_shape=[jax.ShapeDtypeStruct((t, D_SSD), BF16),
                   jax.ShapeDtypeStruct((nseq, D_SSD, STATE), F32)],
        scratch_shapes=[pltpu.VMEM((STATE, D_SSD), F32)],
        compiler_params=_cparams(("arbitrary", "arbitrary")),
        name="ssd_bwd",
    )(*args)


def _mix_kernel(y_ref, u_ref, v_ref, ga_ref, gb_ref, x_ref, mod_ref, lnw_ref, lnb_ref, ws_ref, bst_ref,
                wbs_ref, wbg_ref, wout_ref, npost_ref, o_ref, ysgu_scr):
    tm = x_ref.shape[0]
    v = v_ref[...].astype(F32)
    mu = jnp.mean(v, axis=-1, keepdims=True)
    vc = v - mu
    var = jnp.mean(vc * vc, axis=-1, keepdims=True)
    vn = (vc * lax.rsqrt(var + EPS) * lnw_ref[...] + lnb_ref[...]).astype(BF16)
    bst = bst_ref[...]
    for r in range(tm // CHUNK):
        rows = slice(r * CHUNK, (r + 1) * CHUNK)
        for g in range(SGU_GROUPS):
            cols = slice(g * LANES, (g + 1) * LANES)
            s = _dot(ws_ref[g], vn[rows, cols]) + bst[:, g:g + 1]
            ysgu_scr[rows, cols] = (u_ref[rows, cols].astype(F32) * s).astype(BF16)
    br_ssd = _dot(y_ref[...], wbs_ref[...])
    br_sgu = _dot(ysgu_scr[...], wbg_ref[...])
    merged = _sigmoid(ga_ref[...].astype(F32)) * br_ssd + _sigmoid(gb_ref[...].astype(F32)) * br_sgu
    mix = _dot(merged.astype(BF16), wout_ref[...])
    ms = jnp.mean(mix * mix, axis=-1, keepdims=True)
    o_ref[...] = x_ref[...] + mod_ref[0, 2:3, :] * (mix * lax.rsqrt(ms + EPS) * npost_ref[...])


def _mix(y_ssd, proj, x, mod, consts, tokens_per_mod):
    t = x.shape[0]
    tm = 512
    row = lambda i: (i, 0)
    const2 = lambda i: (0, 0)
    return pl.pallas_call(
        _mix_kernel,
        grid=(t // tm,),
        in_specs=[pl.BlockSpec((tm, D_SSD), row),
                  pl.BlockSpec((tm, D_MODEL), lambda i: (i, 5)),
                  pl.BlockSpec((tm, D_MODEL), lambda i: (i, 6)),
                  pl.BlockSpec((tm, D_MODEL), lambda i: (i, 7)),
                  pl.BlockSpec((tm, D_MODEL), lambda i: (i, 8)),
                  pl.BlockSpec((tm, D_MODEL), row),
                  pl.BlockSpec((1, 6, D_MODEL), lambda i: ((i * tm) // tokens_per_mod, 0, 0)),
                  pl.BlockSpec((1, D_MODEL), const2),
                  pl.BlockSpec((1, D_MODEL), const2),
                  pl.BlockSpec((SGU_GROUPS, CHUNK, CHUNK), lambda i: (0, 0, 0)),
                  pl.BlockSpec((CHUNK, SGU_GROUPS), const2),
                  pl.BlockSpec((D_SSD, D_MODEL), const2),
                  pl.BlockSpec((D_MODEL, D_MODEL), const2),
                  pl.BlockSpec((D_MODEL, D_MODEL), const2),
                  pl.BlockSpec((1, D_MODEL), const2)],
        out_specs=pl.BlockSpec((tm, D_MODEL), row),
        out_shape=jax.ShapeDtypeStruct((t, D_MODEL), F32),
        scratch_shapes=[pltpu.VMEM((tm, D_MODEL), BF16)],
        compiler_params=_cparams(("arbitrary",)),
        name="mix",
    )(y_ssd, proj, proj, proj, proj, x, mod, consts["sgu_norm_w"], consts["sgu_norm_b"], consts["sgu_w"],
      consts["sgu_bt"], consts["w_branch_ssd"], consts["w_branch_sgu"], consts["w_out"], consts["norm_mix_post"])


def _gelu_tanh(x):
    return 0.5 * x * (1.0 + jnp.tanh(0.7978845608028654 * (x + 0.044715 * x * x * x)))


def _grid_conv(x_bf, prev_bf, next_bf, w, b, first, last):
    x = x_bf.astype(F32)
    tm = x.shape[0]
    prev = jnp.where(first, 0.0, prev_bf.astype(F32))
    nxt = jnp.where(last, 0.0, next_bf.astype(F32))
    ext = jnp.concatenate([prev, x, nxt], axis=0)
    n = ext.shape[0]
    col = lax.broadcasted_iota(jnp.int32, ext.shape, 0) % GRID_W
    shifted = (jnp.where(col == 0, 0.0, pltpu.roll(ext, 1, axis=0)),
               ext,
               jnp.where(col == GRID_W - 1, 0.0, pltpu.roll(ext, n - 1, axis=0)))
    acc = None
    for dy in range(3):
        for dx in range(3):
            term = w[3 * dy + dx:3 * dy + dx + 1, :] * shifted[dx][dy * GRID_W:dy * GRID_W + tm, :]
            acc = term if acc is None else acc + term
    return acc + b


def _seq_conv(x_bf, w, b, seq_len):
    x = x_bf.astype(F32)
    n = x.shape[0]
    pos = lax.broadcasted_iota(jnp.int32, x.shape, 0) % seq_len
    xm1 = jnp.where(pos == 0, 0.0, pltpu.roll(x, 1, axis=0))
    xp1 = jnp.where(pos == seq_len - 1, 0.0, pltpu.roll(x, n - 1, axis=0))
    return w[3:4, :] * xm1 + w[4:5, :] * x + w[5:6, :] * xp1 + b


def _ffn_down_kernel(on_grid, seq_len, tiles_per_seq, *refs):
    if on_grid:
        (up_ref, upp_ref, upn_ref, w_ref, b_ref, wd_ref, x_ref, mod_ref, npost_ref, o_ref, g_scr) = refs
    else:
        (up_ref, w_ref, b_ref, wd_ref, x_ref, mod_ref, npost_ref, o_ref, g_scr) = refs
    i = pl.program_id(0)
    first = i % tiles_per_seq == 0
    last = i % tiles_per_seq == tiles_per_seq - 1

    def conv(c0):
        cols = pl.ds(c0, FF_BLK)
        if on_grid:
            return _grid_conv(up_ref[:, cols], upp_ref[:, cols], upn_ref[:, cols], w_ref[:, cols], b_ref[:, cols],
                              first, last)
        return _seq_conv(up_ref[:, cols], w_ref[:, cols], b_ref[:, cols], seq_len)

    def block(k, carry):
        c0 = pl.multiple_of(k * FF_BLK, FF_BLK)
        a = conv(c0)
        val = conv(pl.multiple_of(c0 + D_FF, FF_BLK))
        g_scr[:, pl.ds(c0, FF_BLK)] = (_gelu_tanh(a) * val).astype(BF16)
        return carry

    lax.fori_loop(0, N_FF_BLK, block, 0)
    f = _dot(g_scr[...], wd_ref[...])
    ms = jnp.mean(f * f, axis=-1, keepdims=True)
    o_ref[...] = x_ref[...] + mod_ref[0, 5:6, :] * (f * lax.rsqrt(ms + EPS) * npost_ref[...])


def _ffn_down(up, x, mod, consts, tokens_per_mod, on_grid, seq_len):
    t = x.shape[0]
    tm = 512
    tiles_per_seq = max(seq_len // tm, 1)
    rows_per_tile = tm // GRID_W
    n_rows = t // GRID_W
    in_specs = [pl.BlockSpec((tm, 2 * D_FF), lambda i: (i, 0))]
    args = [up]
    if on_grid:
        in_specs += [pl.BlockSpec((GRID_W, 2 * D_FF), lambda i: (jnp.maximum(i * rows_per_tile - 1, 0), 0)),
                     pl.BlockSpec((GRID_W, 2 * D_FF), lambda i: (jnp.minimum((i + 1) * rows_per_tile, n_rows - 1), 0))]
        args += [up, up]
    in_specs += [pl.BlockSpec((9, 2 * D_FF), lambda i: (0, 0)),
                 pl.BlockSpec((1, 2 * D_FF), lambda i: (0, 0)),
                 pl.BlockSpec((D_FF, D_MODEL), lambda i: (0, 0)),
                 pl.BlockSpec((tm, D_MODEL), lambda i: (i, 0)),
                 pl.BlockSpec((1, 6, D_MODEL), lambda i: ((i * tm) // tokens_per_mod, 0, 0)),
                 pl.BlockSpec((1, D_MODEL), lambda i: (0, 0))]
    args += [consts["ffn_conv_w"], consts["ffn_conv_b"], consts["ffn_w_down"], x, mod, consts["norm_ffn_post"]]
    return pl.pallas_call(
        functools.partial(_ffn_down_kernel, on_grid, seq_len, tiles_per_seq),
        grid=(t // tm,),
        in_specs=in_specs,
        out_specs=pl.BlockSpec((tm, D_MODEL), lambda i: (i, 0)),
        out_shape=jax.ShapeDtypeStruct((t, D_MODEL), F32),
        scratch_shapes=[pltpu.VMEM((tm, D_FF), BF16)],
        compiler_params=_cparams(("arbitrary",)),
        name="ffn_down",
    )(*args)


def _trunk_path(x, mod, h0_f, h0_b, on_grid, consts):
    nseq, seq_len, _ = x.shape
    t = nseq * seq_len
    nchunks = seq_len // CHUNK
    tokens_per_mod = t // mod.shape[0]
    x2d = x.reshape(t, D_MODEL)

    proj, dt_raw = _inproj(x2d, mod, consts["norm_mix_pre"], consts["w_in_main"], consts["w_in_dt"], tokens_per_mod)
    xs, bc, hprev, hf = _ssd_fwd(proj, dt_raw, h0_f, consts, nseq, nchunks)
    y_ssd, hb = _ssd_bwd(xs, bc, proj, dt_raw, hprev, h0_b, consts, nseq, nchunks)
    x1 = _mix(y_ssd, proj, x2d, mod, consts, tokens_per_mod)
    up = _ffn_up(x1, mod, consts["norm_ffn_pre"], consts["ffn_w_up"], tokens_per_mod)
    x2 = _ffn_down(up, x1, mod, consts, tokens_per_mod, on_grid, seq_len)
    return x2.reshape(nseq, seq_len, D_MODEL), hf, hb


def _head_expansion(offset):
    rows = jnp.arange(LANES)[:, None]
    cols = jnp.arange(D_SSD)[None, :] // HEADDIM
    return (rows == cols + offset).astype(BF16)


def _layer_consts(i, p):
    w_in = p["w_in"][i]
    row = lambda v: v.reshape(1, -1).astype(F32)
    pad_lanes = lambda v: jnp.pad(v.reshape(1, -1).astype(F32), ((0, 0), (0, LANES - N_DT)))
    conv_w = p["ssd_conv_w"][i]
    conv_b = p["ssd_conv_b"][i]
    return {
        "norm_mix_pre": row(p["norm_mix_pre"][i]),
        "norm_mix_post": row(p["norm_mix_post"][i]),
        "norm_ffn_pre": row(p["norm_ffn_pre"][i]),
        "norm_ffn_post": row(p["norm_ffn_post"][i]),
        "w_in_main": jnp.concatenate([w_in[:, :SPLIT_XBC], w_in[:, SPLIT_XBC + N_DT:]], axis=1).astype(BF16),
        "w_in_dt": jnp.pad(w_in[:, SPLIT_XBC:SPLIT_XBC + N_DT], ((0, 0), (0, LANES - N_DT))).astype(BF16),
        "cw_x": conv_w[:, :D_SSD], "cw_bc": conv_w[:, D_SSD:],
        "cb_x": row(conv_b[:D_SSD]), "cb_bc": row(conv_b[D_SSD:]),
        "dt_bias": pad_lanes(p["ssd_dt_bias"][i]),
        "a_log": pad_lanes(p["ssd_a_log"][i]),
        "d_vec": row(jnp.repeat(p["ssd_d"][i], HEADDIM)),
        "ssd_norm": row(p["ssd_norm"][i]),
        "e_fwd": _head_expansion(0), "e_bwd": _head_expansion(HEADS),
        "sgu_norm_w": row(p["sgu_norm_w"][i]), "sgu_norm_b": row(p["sgu_norm_b"][i]),
        "sgu_w": p["sgu_w"][i].astype(BF16),
        "sgu_bt": jnp.transpose(p["sgu_b"][i]).astype(F32),
        "w_branch_ssd": p["w_branch_ssd"][i].astype(BF16),
        "w_branch_sgu": p["w_branch_sgu"][i].astype(BF16),
        "w_out": p["w_out"][i].astype(BF16),
        "ffn_w_up": p["ffn_w_up"][i].astype(BF16),
        "ffn_conv_w": p["ffn_conv_w"][i].reshape(9, 2 * D_FF).astype(F32),
        "ffn_conv_b": row(p["ffn_conv_b"][i]),
        "ffn_w_down": p["ffn_w_down"][i].astype(BF16),
    }


def kernel(x_prompt, x_sample, state_ssd_fwd, state_ssd_bwd, c, c_ctx, w_mod, b_mod, norm_mix_pre, norm_mix_post, norm_ffn_pre, norm_ffn_post, w_in, ssd_conv_w, ssd_conv_b, ssd_a_log, ssd_dt_bias, ssd_d, ssd_norm, sgu_norm_w, sgu_norm_b, sgu_w, sgu_b, w_branch_ssd, w_branch_sgu, w_out, ffn_w_up, ffn_conv_w, ffn_conv_b, ffn_w_down):
    params = dict(norm_mix_pre=norm_mix_pre, norm_mix_post=norm_mix_post, norm_ffn_pre=norm_ffn_pre,
                  norm_ffn_post=norm_ffn_post, w_in=w_in, ssd_conv_w=ssd_conv_w, ssd_conv_b=ssd_conv_b,
                  ssd_a_log=ssd_a_log, ssd_dt_bias=ssd_dt_bias, ssd_d=ssd_d, ssd_norm=ssd_norm,
                  sgu_norm_w=sgu_norm_w, sgu_norm_b=sgu_norm_b, sgu_w=sgu_w, sgu_b=sgu_b,
                  w_branch_ssd=w_branch_ssd, w_branch_sgu=w_branch_sgu, w_out=w_out, ffn_w_up=ffn_w_up,
                  ffn_conv_w=ffn_conv_w, ffn_conv_b=ffn_conv_b, ffn_w_down=ffn_w_down)
    depth = w_mod.shape[0]
    n_lat = c.shape[0]
    c_rows = jnp.concatenate([c_ctx[None, :], c, jnp.zeros((8 - 1 - n_lat, D_MODEL), F32)], axis=0)
    xp, xs = x_prompt, x_sample
    new_f, new_b = [], []
    for i in range(depth):
        consts = _layer_consts(i, params)
        mod = _mod_vectors(c_rows, w_mod[i], b_mod[i]).reshape(8, 6, D_MODEL)
        xp, hf, hb = _trunk_path(xp, mod[0:1], None, None, False, consts)
        new_f.append(hf.reshape(-1, HEADS, HEADDIM, STATE))
        new_b.append(hb.reshape(-1, HEADS, HEADDIM, STATE))
        xs, _, _ = _trunk_path(xs, mod[1:1 + n_lat],
                               state_ssd_fwd[:, i].reshape(n_lat, D_SSD, STATE),
                               state_ssd_bwd[:, i].reshape(n_lat, D_SSD, STATE), True, consts)
    return (xp, xs, jnp.stack(new_f, axis=1).astype(x_prompt.dtype), jnp.stack(new_b, axis=1).astype(x_prompt.dtype))
```

```python
import functools

import jax
import jax.numpy as jnp
from jax import lax
from jax.experimental import pallas as pl
from jax.experimental.pallas import tpu as pltpu

F32 = jnp.float32
BF16 = jnp.bfloat16

D_MODEL = 1024
GRID_W = 64
EPS = 1e-6
LOG2E = 1.4426950408889634
D_SSD = 2 * D_MODEL
HEADDIM = 64
HEADS = D_SSD // HEADDIM
STATE = 128
GROUPS = 4
HEADS_PER_GROUP = HEADS // GROUPS
GROUP_W = HEADS_PER_GROUP * HEADDIM
BC_W = GROUPS * STATE
CONV_CH = D_SSD + 2 * BC_W
CHUNK = 128
SGU_GROUPS = 8
D_FF = 2816
SPLIT_XBC = D_SSD + CONV_CH
N_DT = 2 * HEADS
LANES = 128
PROJ_COLS = 9216
FF_BLK = 256
N_FF_BLK = D_FF // FF_BLK

VMEM_LIMIT_BYTES = 56 * 1024 * 1024


def _cparams(sem):
    return pltpu.CompilerParams(dimension_semantics=sem, vmem_limit_bytes=VMEM_LIMIT_BYTES)


def _sigmoid(x):
    return 1.0 / (1.0 + jnp.exp(-x))


def _silu(x):
    return x * _sigmoid(x)


def _dot(a, b):
    return jnp.dot(a, b, preferred_element_type=F32)


def _dot_nt(a, b):
    return lax.dot_general(a, b, (((1,), (1,)), ((), ())), preferred_element_type=F32)


def _mod_kernel(c_ref, w_ref, b_ref, o_ref):
    c = c_ref[...]
    o_ref[...] = jnp.dot(_silu(c), w_ref[...], preferred_element_type=F32,
                         precision=lax.Precision.HIGHEST) + b_ref[...]


def _mod_vectors(c_rows, w_mod, b_mod):
    rows = c_rows.shape[0]
    tn = 1024
    return pl.pallas_call(
        _mod_kernel,
        grid=(6 * D_MODEL // tn,),
        in_specs=[pl.BlockSpec((rows, D_MODEL), lambda j: (0, 0)),
                  pl.BlockSpec((D_MODEL, tn), lambda j: (0, j)),
                  pl.BlockSpec((1, tn), lambda j: (0, j))],
        out_specs=pl.BlockSpec((rows, tn), lambda j: (0, j)),
        out_shape=jax.ShapeDtypeStruct((rows, 6 * D_MODEL), F32),
        compiler_params=_cparams(("arbitrary",)),
        name="mod",
    )(c_rows, w_mod, b_mod.reshape(1, -1))


def _modulated_norm(x, nw, shift, scale):
    ms = jnp.mean(x * x, axis=-1, keepdims=True)
    return (x * lax.rsqrt(ms + EPS) * nw) * (1.0 + scale) + shift


def _inproj_kernel(n_col_steps, x0_ref, xn_ref, mod0_ref, modn_ref, nw_ref, w_ref, wdt_ref, o_ref, dt_ref, ha_scr, hb_scr):
    i = pl.program_id(0)
    j = pl.program_id(1)
    part = xn_ref.shape[0] // n_col_steps
    rows = pl.ds(pl.multiple_of(j * part, part), part)

    def norm(x, mod_ref):
        return _modulated_norm(x, nw_ref[...], mod_ref[0, 0:1, :], mod_ref[0, 1:2, :]).astype(BF16)

    @pl.when((i == 0) & (j == 0))
    def _():
        ha_scr[...] = norm(x0_ref[...], mod0_ref)

    def step(cur_scr, nxt_scr):
        nxt_scr[rows, :] = norm(xn_ref[rows, :], modn_ref)
        dt_ref[rows, :] = _dot(cur_scr[rows, :], wdt_ref[...])
        o_ref[...] = _dot(cur_scr[...], w_ref[...]).astype(BF16)

    @pl.when(i % 2 == 0)
    def _():
        step(ha_scr, hb_scr)

    @pl.when(i % 2 == 1)
    def _():
        step(hb_scr, ha_scr)


def _inproj(x, mod, nw, w_main, w_dt, tokens_per_mod):
    t = x.shape[0]
    tm, tn = 1024, PROJ_COLS // 4
    n_tiles = t // tm
    nxt = lambda i: jnp.minimum(i + 1, n_tiles - 1)
    return pl.pallas_call(
        functools.partial(_inproj_kernel, PROJ_COLS // tn),
        grid=(n_tiles, PROJ_COLS // tn),
        in_specs=[pl.BlockSpec((tm, D_MODEL), lambda i, j: (0, 0)),
                  pl.BlockSpec((tm, D_MODEL), lambda i, j: (nxt(i), 0)),
                  pl.BlockSpec((1, 6, D_MODEL), lambda i, j: (0, 0, 0)),
                  pl.BlockSpec((1, 6, D_MODEL), lambda i, j: ((nxt(i) * tm) // tokens_per_mod, 0, 0)),
                  pl.BlockSpec((1, D_MODEL), lambda i, j: (0, 0)),
                  pl.BlockSpec((D_MODEL, tn), lambda i, j: (0, j)),
                  pl.BlockSpec((D_MODEL, LANES), lambda i, j: (0, 0))],
        out_specs=[pl.BlockSpec((tm, tn), lambda i, j: (i, j)),
                   pl.BlockSpec((tm, LANES), lambda i, j: (i, 0))],
        out_shape=[jax.ShapeDtypeStruct((t, PROJ_COLS), BF16),
                   jax.ShapeDtypeStruct((t, LANES), F32)],
        scratch_shapes=[pltpu.VMEM((tm, D_MODEL), BF16), pltpu.VMEM((tm, D_MODEL), BF16)],
        compiler_params=_cparams(("arbitrary", "arbitrary")),
        name="inproj",
    )(x, x, mod, mod, nw, w_main, w_dt)


def _softplus(x):
    return jnp.maximum(x, 0.0) + jnp.log1p(jnp.exp(-jnp.abs(x)))


def _cumsum_rows(a):
    n = a.shape[0]
    rid = lax.broadcasted_iota(jnp.int32, a.shape, 0)
    s = 1
    while s < n:
        a = a + jnp.where(rid >= s, pltpu.roll(a, s, axis=0), 0.0)
        s *= 2
    return a


def _rev_cumsum_rows(a):
    n = a.shape[0]
    rid = lax.broadcasted_iota(jnp.int32, a.shape, 0)
    s = 1
    while s < n:
        a = a + jnp.where(rid < n - s, pltpu.roll(a, n - s, axis=0), 0.0)
        s *= 2
    return a


def _expand_heads(w, e_ref):
    hi = w.astype(BF16)
    lo = (w - hi.astype(F32)).astype(BF16)
    e = e_ref[...]
    return _dot(hi, e) + _dot(lo, e)


def _conv3_silu(main_ref, prev_ref, next_ref, w_ref, b_ref, first, last):
    x = main_ref[...].astype(F32)
    n = x.shape[0]
    prow = prev_ref[...].astype(F32)[-1:, :]
    nrow = next_ref[...].astype(F32)[0:1, :]
    prow = jnp.where(first, 0.0, prow)
    nrow = jnp.where(last, 0.0, nrow)
    rid = lax.broadcasted_iota(jnp.int32, x.shape, 0)
    xm1 = jnp.where(rid == 0, prow, pltpu.roll(x, 1, axis=0))
    xp1 = jnp.where(rid == n - 1, nrow, pltpu.roll(x, n - 1, axis=0))
    w = w_ref[...]
    y = w[0:1, :] * xm1 + w[1:2, :] * x + w[2:3, :] * xp1 + b_ref[...]
    return _silu(y)


def _dt_and_decay_rates(dt_ref, dtb_ref, alog_ref):
    dt = _softplus(dt_ref[...] + dtb_ref[...])
    a = dt * (-jnp.exp(alog_ref[...]))
    return dt, a


def _pair_rhs(xs_bf, pair, left):
    xp = xs_bf[:, pair * LANES:(pair + 1) * LANES]
    zero = jnp.zeros_like(xp)
    return jnp.concatenate([jnp.where(left, xp, zero), jnp.where(left, zero, xp)], axis=0)


def _ssd_fwd_kernel(has_h0, *refs):
    (xm_ref, bcm_ref, xp_ref, bcp_ref, xn_ref, bcn_ref, dt_ref,
     cwx_ref, cwbc_ref, cbx_ref, cbbc_ref, dtb_ref, alog_ref, ef_ref) = refs[:14]
    pos = 14
    h0_ref = None
    if has_h0:
        h0_ref = refs[pos]
        pos += 1
    xs_ref, bc_ref, hprev_ref, hfin_ref, h_scr = refs[pos:pos + 5]

    c = pl.program_id(1)
    nchunks = pl.num_programs(1)
    first = c == 0
    last = c == nchunks - 1

    @pl.when(first)
    def _():
        if has_h0:
            h_scr[...] = h0_ref[0].T
        else:
            h_scr[...] = jnp.zeros_like(h_scr)

    xs = _conv3_silu(xm_ref, xp_ref, xn_ref, cwx_ref, cbx_ref, first, last)
    bc = _conv3_silu(bcm_ref, bcp_ref, bcn_ref, cwbc_ref, cbbc_ref, first, last)
    xs_bf = xs.astype(BF16)
    xs_ref[...] = xs_bf
    bc_ref[...] = bc.astype(BF16)

    dt, a = _dt_and_decay_rates(dt_ref, dtb_ref, alog_ref)
    acs = _cumsum_rows(a)
    acs_t = acs.T
    w2_t = dt.T * jnp.exp(acs_t[:, CHUNK - 1:CHUNK] - acs_t)
    cdec = _expand_heads(jnp.broadcast_to(jnp.exp(acs[CHUNK - 1:CHUNK, :]), (8, LANES)), ef_ref)[0:1, :]

    left = lax.broadcasted_iota(jnp.int32, (CHUNK, LANES), 1) < HEADDIM
    parts = []
    for g in range(GROUPS):
        bt = bc[:, g * STATE:(g + 1) * STATE].T
        for k in range(HEADS_PER_GROUP // 2):
            h = g * HEADS_PER_GROUP + 2 * k
            lhs = jnp.concatenate([(bt * w2_t[h:h + 1, :]).astype(BF16),
                                   (bt * w2_t[h + 1:h + 2, :]).astype(BF16)], axis=1)
            parts.append(_dot(lhs, _pair_rhs(xs_bf, h // 2, left)))
    s_loc = jnp.concatenate(parts, axis=1)

    h_prev = h_scr[...]
    hprev_ref[0] = h_prev.astype(BF16)
    h_new = cdec * h_prev + s_loc
    h_scr[...] = h_new

    @pl.when(last)
    def _():
        hfin_ref[0] = h_new.T


def _ssd_fwd(proj, dt_raw, h0, consts, nseq, nchunks):
    t = nseq * nchunks * CHUNK
    n16 = t // 16
    gidx = lambda b, c: b * nchunks + c
    in_specs = [
        pl.BlockSpec((CHUNK, D_SSD), lambda b, c: (gidx(b, c), 1)),
        pl.BlockSpec((CHUNK, 2 * BC_W), lambda b, c: (gidx(b, c), 4)),
        pl.BlockSpec((16, D_SSD), lambda b, c: (jnp.maximum(gidx(b, c) * 8 - 1, 0), 1)),
        pl.BlockSpec((16, 2 * BC_W), lambda b, c: (jnp.maximum(gidx(b, c) * 8 - 1, 0), 4)),
        pl.BlockSpec((16, D_SSD), lambda b, c: (jnp.minimum((gidx(b, c) + 1) * 8, n16 - 1), 1)),
        pl.BlockSpec((16, 2 * BC_W), lambda b, c: (jnp.minimum((gidx(b, c) + 1) * 8, n16 - 1), 4)),
        pl.BlockSpec((CHUNK, LANES), lambda b, c: (gidx(b, c), 0)),
        pl.BlockSpec((3, D_SSD), lambda b, c: (0, 0)),
        pl.BlockSpec((3, 2 * BC_W), lambda b, c: (0, 0)),
        pl.BlockSpec((1, D_SSD), lambda b, c: (0, 0)),
        pl.BlockSpec((1, 2 * BC_W), lambda b, c: (0, 0)),
        pl.BlockSpec((1, LANES), lambda b, c: (0, 0)),
        pl.BlockSpec((1, LANES), lambda b, c: (0, 0)),
        pl.BlockSpec((LANES, D_SSD), lambda b, c: (0, 0)),
    ]
    args = [proj, proj, proj, proj, proj, proj, dt_raw,
            consts["cw_x"], consts["cw_bc"], consts["cb_x"], consts["cb_bc"],
            consts["dt_bias"], consts["a_log"], consts["e_fwd"]]
    has_h0 = h0 is not None
    if has_h0:
        in_specs.append(pl.BlockSpec((1, D_SSD, STATE), lambda b, c: (b, 0, 0)))
        args.append(h0)
    return pl.pallas_call(
        functools.partial(_ssd_fwd_kernel, has_h0),
        grid=(nseq, nchunks),
        in_specs=in_specs,
        out_specs=[pl.BlockSpec((CHUNK, D_SSD), lambda b, c: (gidx(b, c), 0)),
                   pl.BlockSpec((CHUNK, 2 * BC_W), lambda b, c: (gidx(b, c), 0)),
                   pl.BlockSpec((1, STATE, D_SSD), lambda b, c: (gidx(b, c), 0, 0)),
                   pl.BlockSpec((1, D_SSD, STATE), lambda b, c: (b, 0, 0))],
        out_shape=[jax.ShapeDtypeStruct((t, D_SSD), BF16),
                   jax.ShapeDtypeStruct((t, 2 * BC_W), BF16),
                   jax.ShapeDtypeStruct((nseq * nchunks, STATE, D_SSD), BF16),
                   jax.ShapeDtypeStruct((nseq, D_SSD, STATE), F32)],
        scratch_shapes=[pltpu.VMEM((STATE, D_SSD), F32)],
        compiler_params=_cparams(("arbitrary", "arbitrary")),
        name="ssd_fwd",
    )(*args)


def _ssd_bwd_kernel(has_h0, *refs):
    (xs_ref, bc_ref, z_ref, dt_ref, hprev_ref, dtb_ref, alog_ref, dvec_ref, nw_ref,
     eb_ref) = refs[:10]
    pos = 10
    h0_ref = None
    if has_h0:
        h0_ref = refs[pos]
        pos += 1
    y_ref, hfin_ref, h_scr = refs[pos:pos + 3]

    c = pl.program_id(1)
    nchunks = pl.num_programs(1)

    @pl.when(c == 0)
    def _():
        if has_h0:
            h_scr[...] = h0_ref[0].T
        else:
            h_scr[...] = jnp.zeros_like(h_scr)

    xs_bf = xs_ref[...]
    xs = xs_bf.astype(F32)
    bc_bf = bc_ref[...]

    dt, a = _dt_and_decay_rates(dt_ref, dtb_ref, alog_ref)
    acs = _cumsum_rows(a) * LOG2E
    rcs = _rev_cumsum_rows(a) * LOG2E
    acs_t = acs.T
    rcs_t = rcs.T
    dt_t = dt.T
    lg_t = jnp.log2(dt_t)
    rf_t = acs_t - lg_t
    rb_t = rcs_t - lg_t
    w2b_t = dt_t * jnp.exp2(rcs_t[:, 0:1] - rcs_t)
    cdec = _expand_heads(jnp.broadcast_to(jnp.exp2(rcs[0:1, :]), (8, LANES)), eb_ref)[0:1, :]

    ri = lax.broadcasted_iota(jnp.int32, (CHUNK, CHUNK), 0)
    ci = lax.broadcasted_iota(jnp.int32, (CHUNK, CHUNK), 1)
    lower = ri >= ci
    diag = ri == ci
    left = lax.broadcasted_iota(jnp.int32, (CHUNK, LANES), 1) < HEADDIM

    h_f = hprev_ref[0]
    h_b = h_scr[...]
    h_b_bf = h_b.astype(BF16)

    y_parts = []
    s_parts = []
    for g in range(GROUPS):
        b_g = bc_bf[:, g * STATE:(g + 1) * STATE]
        c_g = bc_bf[:, BC_W + g * STATE:BC_W + (g + 1) * STATE]
        cb = _dot_nt(c_g, b_g)
        bt = b_g.astype(F32).T
        off_f = _dot(c_g, h_f[:, g * GROUP_W:(g + 1) * GROUP_W])
        off_b = _dot(c_g, h_b_bf[:, g * GROUP_W:(g + 1) * GROUP_W])
        for k in range(HEADS_PER_GROUP // 2):
            ms, bs, cfs, cbs = [], [], [], []
            for h in (g * HEADS_PER_GROUP + 2 * k, g * HEADS_PER_GROUP + 2 * k + 1):
                hb = HEADS + h
                cf = acs[:, h:h + 1]
                cbk = rcs[:, hb:hb + 1]
                e = jnp.exp2(jnp.where(lower, cf - rf_t[h:h + 1, :], cbk - rb_t[hb:hb + 1, :]))
                e = e + jnp.where(diag, dt_t[hb:hb + 1, :], 0.0)
                ms.append((cb * e).astype(BF16))
                bs.append((bt * w2b_t[hb:hb + 1, :]).astype(BF16))
                cfs.append(cf)
                cbs.append(cbk)
            lhs = jnp.concatenate([jnp.concatenate(ms, axis=1), jnp.concatenate(bs, axis=1)], axis=0)
            out = _dot(lhs, _pair_rhs(xs_bf, g * HEADS_PER_GROUP // 2 + k, left))
            ef = jnp.exp2(jnp.where(left, cfs[0], cfs[1]))
            eb = jnp.exp2(jnp.where(left, cbs[0], cbs[1]))
            cols = slice(k * LANES, (k + 1) * LANES)
            y_parts.append(out[:CHUNK, :] + ef * off_f[:, cols] + eb * off_b[:, cols])
            s_parts.append(out[CHUNK:, :])
    y = jnp.concatenate(y_parts, axis=1) + dvec_ref[...] * xs
    y = y * _silu(z_ref[...].astype(F32))
    ms_y = jnp.mean(y * y, axis=-1, keepdims=True)
    y_ref[...] = (y * lax.rsqrt(ms_y + EPS) * nw_ref[...]).astype(BF16)

    h_new = cdec * h_b + jnp.concatenate(s_parts, axis=1)
    h_scr[...] = h_new

    @pl.when(c == nchunks - 1)
    def _():
        hfin_ref[0] = h_new.T


def _ssd_bwd(xs, bc, proj, dt_raw, hprev, h0, consts, nseq, nchunks):
    t = nseq * nchunks * CHUNK
    gidx = lambda b, c: b * nchunks + (nchunks - 1 - c)
    in_specs = [
        pl.BlockSpec((CHUNK, D_SSD), lambda b, c: (gidx(b, c), 0)),
        pl.BlockSpec((CHUNK, 2 * BC_W), lambda b, c: (gidx(b, c), 0)),
        pl.BlockSpec((CHUNK, D_SSD), lambda b, c: (gidx(b, c), 0)),
        pl.BlockSpec((CHUNK, LANES), lambda b, c: (gidx(b, c), 0)),
        pl.BlockSpec((1, STATE, D_SSD), lambda b, c: (gidx(b, c), 0, 0)),
        pl.BlockSpec((1, LANES), lambda b, c: (0, 0)),
        pl.BlockSpec((1, LANES), lambda b, c: (0, 0)),
        pl.BlockSpec((1, D_SSD), lambda b, c: (0, 0)),
        pl.BlockSpec((1, D_SSD), lambda b, c: (0, 0)),
        pl.BlockSpec((LANES, D_SSD), lambda b, c: (0, 0)),
    ]
    args = [xs, bc, proj, dt_raw, hprev, consts["dt_bias"], consts["a_log"], consts["d_vec"],
            consts["ssd_norm"], consts["e_bwd"]]
    has_h0 = h0 is not None
    if has_h0:
        in_specs.append(pl.BlockSpec((1, D_SSD, STATE), lambda b, c: (b, 0, 0)))
        args.append(h0)
    return pl.pallas_call(
        functools.partial(_ssd_bwd_kernel, has_h0),
        grid=(nseq, nchunks),
        in_specs=in_specs,
        out_specs=[pl.BlockSpec((CHUNK, D_SSD), lambda b, c: (gidx(b, c), 0)),
                   pl.BlockSpec((1, D_SSD, STATE), lambda b, c: (b, 0, 0))],
        out_shape=[jax.ShapeDtypeStruct((t, D_SSD), BF16),
                   jax.ShapeDtypeStruct((nseq, D_SSD, STATE), F32)],
        scratch_shapes=[pltpu.VMEM((STATE, D_SSD), F32)],
        compiler_params=_cparams(("arbitrary", "arbitrary")),
        name="ssd_bwd",
    )(*args)


def _mix_kernel(y_ref, u_ref, v_ref, ga_ref, gb_ref, x_ref, mod_ref, lnw_ref, lnb_ref, ws_ref, bst_ref,
                wbs_ref, wbg_ref, wout_ref, npost_ref, npre2_ref, o_ref, h2_ref, ysgu_scr):
    tm = x_ref.shape[0]
    v = v_ref[...].astype(F32)
    mu = jnp.mean(v, axis=-1, keepdims=True)
    vc = v - mu
    var = jnp.mean(vc * vc, axis=-1, keepdims=True)
    vn = (vc * lax.rsqrt(var + EPS) * lnw_ref[...] + lnb_ref[...]).astype(BF16)
    bst = bst_ref[...]
    for r in range(tm // CHUNK):
        rows = slice(r * CHUNK, (r + 1) * CHUNK)
        for g in range(SGU_GROUPS):
            cols = slice(g * LANES, (g + 1) * LANES)
            s = _dot(ws_ref[g], vn[rows, cols]) + bst[:, g:g + 1]
            ysgu_scr[rows, cols] = (u_ref[rows, cols].astype(F32) * s).astype(BF16)
    br_ssd = _dot(y_ref[...], wbs_ref[...])
    br_sgu = _dot(ysgu_scr[...], wbg_ref[...])
    merged = _sigmoid(ga_ref[...].astype(F32)) * br_ssd + _sigmoid(gb_ref[...].astype(F32)) * br_sgu
    mix = _dot(merged.astype(BF16), wout_ref[...])
    ms = jnp.mean(mix * mix, axis=-1, keepdims=True)
    x1 = x_ref[...] + mod_ref[0, 2:3, :] * (mix * lax.rsqrt(ms + EPS) * npost_ref[...])
    o_ref[...] = x1
    h2_ref[...] = _modulated_norm(x1, npre2_ref[...], mod_ref[0, 3:4, :], mod_ref[0, 4:5, :]).astype(BF16)


def _mix(y_ssd, proj, x, mod, consts, tokens_per_mod):
    t = x.shape[0]
    tm = 512
    row = lambda i: (i, 0)
    const2 = lambda i: (0, 0)
    return pl.pallas_call(
        _mix_kernel,
        grid=(t // tm,),
        in_specs=[pl.BlockSpec((tm, D_SSD), row),
                  pl.BlockSpec((tm, D_MODEL), lambda i: (i, 5)),
                  pl.BlockSpec((tm, D_MODEL), lambda i: (i, 6)),
                  pl.BlockSpec((tm, D_MODEL), lambda i: (i, 7)),
                  pl.BlockSpec((tm, D_MODEL), lambda i: (i, 8)),
                  pl.BlockSpec((tm, D_MODEL), row),
                  pl.BlockSpec((1, 6, D_MODEL), lambda i: ((i * tm) // tokens_per_mod, 0, 0)),
                  pl.BlockSpec((1, D_MODEL), const2),
                  pl.BlockSpec((1, D_MODEL), const2),
                  pl.BlockSpec((SGU_GROUPS, CHUNK, CHUNK), lambda i: (0, 0, 0)),
                  pl.BlockSpec((CHUNK, SGU_GROUPS), const2),
                  pl.BlockSpec((D_SSD, D_MODEL), const2),
                  pl.BlockSpec((D_MODEL, D_MODEL), const2),
                  pl.BlockSpec((D_MODEL, D_MODEL), const2),
                  pl.BlockSpec((1, D_MODEL), const2),
                  pl.BlockSpec((1, D_MODEL), const2)],
        out_specs=[pl.BlockSpec((tm, D_MODEL), row), pl.BlockSpec((tm, D_MODEL), row)],
        out_shape=[jax.ShapeDtypeStruct((t, D_MODEL), F32), jax.ShapeDtypeStruct((t, D_MODEL), BF16)],
        scratch_shapes=[pltpu.VMEM((tm, D_MODEL), BF16)],
        compiler_params=_cparams(("arbitrary",)),
        name="mix",
    )(y_ssd, proj, proj, proj, proj, x, mod, consts["sgu_norm_w"], consts["sgu_norm_b"], consts["sgu_w"],
      consts["sgu_bt"], consts["w_branch_ssd"], consts["w_branch_sgu"], consts["w_out"], consts["norm_mix_post"],
      consts["norm_ffn_pre"])


def _matmul_kernel(h_ref, w_ref, o_ref):
    o_ref[...] = _dot(h_ref[...], w_ref[...]).astype(BF16)


def _ffn_up(h2, w_up):
    t = h2.shape[0]
    tm, tn = 1024, D_FF
    return pl.pallas_call(
        _matmul_kernel,
        grid=(t // tm, 2 * D_FF // tn),
        in_specs=[pl.BlockSpec((tm, D_MODEL), lambda i, j: (i, 0)),
                  pl.BlockSpec((D_MODEL, tn), lambda i, j: (0, j))],
        out_specs=pl.BlockSpec((tm, tn), lambda i, j: (i, j)),
        out_shape=jax.ShapeDtypeStruct((t, 2 * D_FF), BF16),
        compiler_params=_cparams(("arbitrary", "arbitrary")),
        name="ffn_up",
    )(h2, w_up)


def _gelu_tanh(x):
    return 0.5 * x * (1.0 + jnp.tanh(0.7978845608028654 * (x + 0.044715 * x * x * x)))


def _grid_conv(x_bf, prev_bf, next_bf, w, b, first, last):
    x = x_bf.astype(F32)
    tm = x.shape[0]
    prev = jnp.where(first, 0.0, prev_bf.astype(F32))
    nxt = jnp.where(last, 0.0, next_bf.astype(F32))
    ext = jnp.concatenate([prev, x, nxt], axis=0)
    n = ext.shape[0]
    col = lax.broadcasted_iota(jnp.int32, ext.shape, 0) % GRID_W
    shifted = (jnp.where(col == 0, 0.0, pltpu.roll(ext, 1, axis=0)),
               ext,
               jnp.where(col == GRID_W - 1, 0.0, pltpu.roll(ext, n - 1, axis=0)))
    acc = None
    for dy in range(3):
        for dx in range(3):
            term = w[3 * dy + dx:3 * dy + dx + 1, :] * shifted[dx][dy * GRID_W:dy * GRID_W + tm, :]
            acc = term if acc is None else acc + term
    return acc + b


def _seq_conv(x_bf, w, b, seq_len):
    x = x_bf.astype(F32)
    n = x.shape[0]
    pos = lax.broadcasted_iota(jnp.int32, x.shape, 0) % seq_len
    xm1 = jnp.where(pos == 0, 0.0, pltpu.roll(x, 1, axis=0))
    xp1 = jnp.where(pos == seq_len - 1, 0.0, pltpu.roll(x, n - 1, axis=0))
    return w[3:4, :] * xm1 + w[4:5, :] * x + w[5:6, :] * xp1 + b


def _ffn_down_kernel(on_grid, seq_len, tiles_per_seq, *refs):
    if on_grid:
        (up_ref, upp_ref, upn_ref, w_ref, b_ref, wd_ref, x_ref, mod_ref, npost_ref, o_ref, g0_scr, g1_scr, acc_scr) = refs
    else:
        (up_ref, w_ref, b_ref, wd_ref, x_ref, mod_ref, npost_ref, o_ref, g0_scr, g1_scr, acc_scr) = refs
    i = pl.program_id(0)
    first = i % tiles_per_seq == 0
    last = i % tiles_per_seq == tiles_per_seq - 1

    def conv(c0):
        cols = pl.ds(c0, FF_BLK)
        if on_grid:
            return _grid_conv(up_ref[:, cols], upp_ref[:, cols], upn_ref[:, cols], w_ref[:, cols], b_ref[:, cols],
                              first, last)
        return _seq_conv(up_ref[:, cols], w_ref[:, cols], b_ref[:, cols], seq_len)

    def geglu_block(k, dst_ref):
        c0 = pl.multiple_of(k * FF_BLK, FF_BLK)
        a = conv(c0)
        val = conv(pl.multiple_of(c0 + D_FF, FF_BLK))
        dst_ref[...] = (_gelu_tanh(a) * val).astype(BF16)

    def down_block(k, src_ref):
        rows = pl.ds(pl.multiple_of(k * FF_BLK, FF_BLK), FF_BLK)
        acc_scr[...] += _dot(src_ref[...], wd_ref[rows, :])

    acc_scr[...] = jnp.zeros_like(acc_scr)
    geglu_block(0, g0_scr)

    def block_pair(m, carry):
        k = 2 * m + 1
        down_block(k - 1, g0_scr)
        geglu_block(k, g1_scr)
        down_block(k, g1_scr)
        geglu_block(k + 1, g0_scr)
        return carry

    lax.fori_loop(0, (N_FF_BLK - 1) // 2, block_pair, 0)
    down_block(N_FF_BLK - 1, g0_scr)
    f = acc_scr[...]
    ms = jnp.mean(f * f, axis=-1, keepdims=True)
    o_ref[...] = x_ref[...] + mod_ref[0, 5:6, :] * (f * lax.rsqrt(ms + EPS) * npost_ref[...])


def _ffn_down(up, x, mod, consts, tokens_per_mod, on_grid, seq_len):
    t = x.shape[0]
    tm = 512
    tiles_per_seq = max(seq_len // tm, 1)
    rows_per_tile = tm // GRID_W
    n_rows = t // GRID_W
    in_specs = [pl.BlockSpec((tm, 2 * D_FF), lambda i: (i, 0))]
    args = [up]
    if on_grid:
        in_specs += [pl.BlockSpec((GRID_W, 2 * D_FF), lambda i: (jnp.maximum(i * rows_per_tile - 1, 0), 0)),
                     pl.BlockSpec((GRID_W, 2 * D_FF), lambda i: (jnp.minimum((i + 1) * rows_per_tile, n_rows - 1), 0))]
        args += [up, up]
    in_specs += [pl.BlockSpec((9, 2 * D_FF), lambda i: (0, 0)),
                 pl.BlockSpec((1, 2 * D_FF), lambda i: (0, 0)),
                 pl.BlockSpec((D_FF, D_MODEL), lambda i: (0, 0)),
                 pl.BlockSpec((tm, D_MODEL), lambda i: (i, 0)),
                 pl.BlockSpec((1, 6, D_MODEL), lambda i: ((i * tm) // tokens_per_mod, 0, 0)),
                 pl.BlockSpec((1, D_MODEL), lambda i: (0, 0))]
    args += [consts["ffn_conv_w"], consts["ffn_conv_b"], consts["ffn_w_down"], x, mod, consts["norm_ffn_post"]]
    return pl.pallas_call(
        functools.partial(_ffn_down_kernel, on_grid, seq_len, tiles_per_seq),
        grid=(t // tm,),
        in_specs=in_specs,
        out_specs=pl.BlockSpec((tm, D_MODEL), lambda i: (i, 0)),
        out_shape=jax.ShapeDtypeStruct((t, D_MODEL), F32),
        scratch_shapes=[pltpu.VMEM((tm, FF_BLK), BF16), pltpu.VMEM((tm, FF_BLK), BF16),
                        pltpu.VMEM((tm, D_MODEL), F32)],
        compiler_params=_cparams(("arbitrary",)),
        name="ffn_down",
    )(*args)


def _trunk_path(x, mod, h0_f, h0_b, on_grid, consts):
    nseq, seq_len, _ = x.shape
    t = nseq * seq_len
    nchunks = seq_len // CHUNK
    tokens_per_mod = t // mod.shape[0]
    x2d = x.reshape(t, D_MODEL)

    proj, dt_raw = _inproj(x2d, mod, consts["norm_mix_pre"], consts["w_in_main"], consts["w_in_dt"], tokens_per_mod)
    xs, bc, hprev, hf = _ssd_fwd(proj, dt_raw, h0_f, consts, nseq, nchunks)
    y_ssd, hb = _ssd_bwd(xs, bc, proj, dt_raw, hprev, h0_b, consts, nseq, nchunks)
    x1, h2 = _mix(y_ssd, proj, x2d, mod, consts, tokens_per_mod)
    up = _ffn_up(h2, consts["ffn_w_up"])
    x2 = _ffn_down(up, x1, mod, consts, tokens_per_mod, on_grid, seq_len)
    return x2.reshape(nseq, seq_len, D_MODEL), hf, hb


def _head_expansion(offset):
    rows = jnp.arange(LANES)[:, None]
    cols = jnp.arange(D_SSD)[None, :] // HEADDIM
    return (rows == cols + offset).astype(BF16)


def _layer_consts(i, p):
    w_in = p["w_in"][i]
    row = lambda v: v.reshape(1, -1).astype(F32)
    pad_lanes = lambda v: jnp.pad(v.reshape(1, -1).astype(F32), ((0, 0), (0, LANES - N_DT)))
    conv_w = p["ssd_conv_w"][i]
    conv_b = p["ssd_conv_b"][i]
    return {
        "norm_mix_pre": row(p["norm_mix_pre"][i]),
        "norm_mix_post": row(p["norm_mix_post"][i]),
        "norm_ffn_pre": row(p["norm_ffn_pre"][i]),
        "norm_ffn_post": row(p["norm_ffn_post"][i]),
        "w_in_main": jnp.concatenate([w_in[:, :SPLIT_XBC].astype(BF16), w_in[:, SPLIT_XBC + N_DT:].astype(BF16)],
                                     axis=1),
        "w_in_dt": jnp.pad(w_in[:, SPLIT_XBC:SPLIT_XBC + N_DT], ((0, 0), (0, LANES - N_DT))).astype(BF16),
        "cw_x": conv_w[:, :D_SSD], "cw_bc": conv_w[:, D_SSD:],
        "cb_x": row(conv_b[:D_SSD]), "cb_bc": row(conv_b[D_SSD:]),
        "dt_bias": pad_lanes(p["ssd_dt_bias"][i]),
        "a_log": pad_lanes(p["ssd_a_log"][i]),
        "d_vec": row(jnp.repeat(p["ssd_d"][i], HEADDIM)),
        "ssd_norm": row(p["ssd_norm"][i]),
        "e_fwd": _head_expansion(0), "e_bwd": _head_expansion(HEADS),
        "sgu_norm_w": row(p["sgu_norm_w"][i]), "sgu_norm_b": row(p["sgu_norm_b"][i]),
        "sgu_w": p["sgu_w"][i].astype(BF16),
        "sgu_bt": jnp.transpose(p["sgu_b"][i]).astype(F32),
        "w_branch_ssd": p["w_branch_ssd"][i].astype(BF16),
        "w_branch_sgu": p["w_branch_sgu"][i].astype(BF16),
        "w_out": p["w_out"][i].astype(BF16),
        "ffn_w_up": p["ffn_w_up"][i].astype(BF16),
        "ffn_conv_w": p["ffn_conv_w"][i].reshape(9, 2 * D_FF).astype(F32),
        "ffn_conv_b": row(p["ffn_conv_b"][i]),
        "ffn_w_down": p["ffn_w_down"][i].astype(BF16),
    }


def kernel(x_prompt, x_sample, state_ssd_fwd, state_ssd_bwd, c, c_ctx, w_mod, b_mod, norm_mix_pre, norm_mix_post, norm_ffn_pre, norm_ffn_post, w_in, ssd_conv_w, ssd_conv_b, ssd_a_log, ssd_dt_bias, ssd_d, ssd_norm, sgu_norm_w, sgu_norm_b, sgu_w, sgu_b, w_branch_ssd, w_branch_sgu, w_out, ffn_w_up, ffn_conv_w, ffn_conv_b, ffn_w_down):
    params = dict(norm_mix_pre=norm_mix_pre, norm_mix_post=norm_mix_post, norm_ffn_pre=norm_ffn_pre,
                  norm_ffn_post=norm_ffn_post, w_in=w_in, ssd_conv_w=ssd_conv_w, ssd_conv_b=ssd_conv_b,
                  ssd_a_log=ssd_a_log, ssd_dt_bias=ssd_dt_bias, ssd_d=ssd_d, ssd_norm=ssd_norm,
                  sgu_norm_w=sgu_norm_w, sgu_norm_b=sgu_norm_b, sgu_w=sgu_w, sgu_b=sgu_b,
                  w_branch_ssd=w_branch_ssd, w_branch_sgu=w_branch_sgu, w_out=w_out, ffn_w_up=ffn_w_up,
                  ffn_conv_w=ffn_conv_w, ffn_conv_b=ffn_conv_b, ffn_w_down=ffn_w_down)
    depth = w_mod.shape[0]
    n_lat = c.shape[0]
    c_rows = jnp.concatenate([c_ctx[None, :], c, jnp.zeros((8 - 1 - n_lat, D_MODEL), F32)], axis=0)
    xp, xs = x_prompt, x_sample
    new_f, new_b = [], []
    for i in range(depth):
        consts = _layer_consts(i, params)
        mod = _mod_vectors(c_rows, w_mod[i], b_mod[i]).reshape(8, 6, D_MODEL)
        xp, hf, hb = _trunk_path(xp, mod[0:1], None, None, False, consts)
        new_f.append(hf.reshape(-1, HEADS, HEADDIM, STATE))
        new_b.append(hb.reshape(-1, HEADS, HEADDIM, STATE))
        xs, _, _ = _trunk_path(xs, mod[1:1 + n_lat],
                               state_ssd_fwd[:, i].reshape(n_lat, D_SSD, STATE),
                               state_ssd_bwd[:, i].reshape(n_lat, D_SSD, STATE), True, consts)
    return (xp, xs, jnp.stack(new_f, axis=1).astype(x_prompt.dtype), jnp.stack(new_b, axis=1).astype(x_prompt.dtype))
```

```python
import functools

import jax
import jax.numpy as jnp
from jax import lax
from jax.experimental import pallas as pl
from jax.experimental.pallas import tpu as pltpu

F32 = jnp.float32
BF16 = jnp.bfloat16

D_MODEL = 1024
GRID_W = 64
EPS = 1e-6
LOG2E = 1.4426950408889634
GELU_C = 0.7978845608028654
D_SSD = 2 * D_MODEL
HEADDIM = 64
HEADS = D_SSD // HEADDIM
STATE = 128
GROUPS = 4
HEADS_PER_GROUP = HEADS // GROUPS
GROUP_W = HEADS_PER_GROUP * HEADDIM
BC_W = GROUPS * STATE
CONV_CH = D_SSD + 2 * BC_W
CHUNK = 128
SSD_FWD_CHUNKS_PER_STEP = 2
SSD_BWD_CHUNKS_PER_STEP = 4
SGU_GROUPS = 8
D_FF = 2816
SPLIT_XBC = D_SSD + CONV_CH
N_DT = 2 * HEADS
LANES = 128
PROJ_COLS = 9216
FF_BLK = 256
N_FF_BLK = D_FF // FF_BLK

VMEM_LIMIT_BYTES = 56 * 1024 * 1024


def _cparams(sem):
    return pltpu.CompilerParams(dimension_semantics=sem, vmem_limit_bytes=VMEM_LIMIT_BYTES)


def _sigmoid(x):
    return 0.5 + 0.5 * jnp.tanh(0.5 * x)


def _silu(x):
    h = 0.5 * x
    return h + h * jnp.tanh(h)


def _dot(a, b):
    return jnp.dot(a, b, preferred_element_type=F32)


def _dot_nt(a, b):
    return lax.dot_general(a, b, (((1,), (1,)), ((), ())), preferred_element_type=F32)


def _mod_kernel(c_ref, w_ref, b_ref, o_ref):
    c = c_ref[...]
    o_ref[...] = jnp.dot(_silu(c), w_ref[...], preferred_element_type=F32,
                         precision=lax.Precision.HIGHEST) + b_ref[...]


def _mod_vectors(c_rows, w_mod, b_mod):
    rows = c_rows.shape[0]
    tn = 1024
    return pl.pallas_call(
        _mod_kernel,
        grid=(6 * D_MODEL // tn,),
        in_specs=[pl.BlockSpec((rows, D_MODEL), lambda j: (0, 0)),
                  pl.BlockSpec((D_MODEL, tn), lambda j: (0, j)),
                  pl.BlockSpec((1, tn), lambda j: (0, j))],
        out_specs=pl.BlockSpec((rows, tn), lambda j: (0, j)),
        out_shape=jax.ShapeDtypeStruct((rows, 6 * D_MODEL), F32),
        compiler_params=_cparams(("arbitrary",)),
        name="mod",
    )(c_rows, w_mod, b_mod.reshape(1, -1))


def _w_in_prep_kernel(w_ref, main_ref, dt_ref):
    w = w_ref[0]
    pad = jnp.zeros((w.shape[0], LANES - N_DT), F32)
    main_ref[...] = jnp.concatenate([w[:, :SPLIT_XBC], w[:, SPLIT_XBC + N_DT:]], axis=1).astype(BF16)
    dt_ref[...] = jnp.concatenate([w[:, SPLIT_XBC:SPLIT_XBC + N_DT], pad], axis=1).astype(BF16)


def _w_in_prep(w_in, layer):
    rows = 128
    return pl.pallas_call(
        _w_in_prep_kernel,
        grid=(D_MODEL // rows,),
        in_specs=[pl.BlockSpec((1, rows, PROJ_COLS + N_DT), lambda r: (layer, r, 0))],
        out_specs=[pl.BlockSpec((rows, PROJ_COLS), lambda r: (r, 0)),
                   pl.BlockSpec((rows, LANES), lambda r: (r, 0))],
        out_shape=[jax.ShapeDtypeStruct((D_MODEL, PROJ_COLS), BF16),
                   jax.ShapeDtypeStruct((D_MODEL, LANES), BF16)],
        compiler_params=_cparams(("arbitrary",)),
        name="w_in_prep",
    )(w_in)


def _modulated_norm(x, nw, shift, scale):
    ms = jnp.mean(x * x, axis=-1, keepdims=True)
    return (x * lax.rsqrt(ms + EPS) * nw) * (1.0 + scale) + shift


def _inproj_kernel(n_col_steps, x0_ref, xn_ref, mod0_ref, modn_ref, nw_ref, w_ref, wdt_ref, o_ref, dt_ref, ha_scr, hb_scr):
    i = pl.program_id(0)
    j = pl.program_id(1)
    part = xn_ref.shape[0] // n_col_steps
    rows = pl.ds(pl.multiple_of(j * part, part), part)

    def norm(x, mod_ref):
        return _modulated_norm(x, nw_ref[...], mod_ref[0, 0:1, :], mod_ref[0, 1:2, :]).astype(BF16)

    @pl.when((i == 0) & (j == 0))
    def _():
        ha_scr[...] = norm(x0_ref[...], mod0_ref)

    def step(cur_scr, nxt_scr):
        nxt_scr[rows, :] = norm(xn_ref[rows, :], modn_ref)
        dt_ref[rows, :] = _dot(cur_scr[rows, :], wdt_ref[...])
        o_ref[...] = _dot(cur_scr[...], w_ref[...]).astype(BF16)

    @pl.when(i % 2 == 0)
    def _():
        step(ha_scr, hb_scr)

    @pl.when(i % 2 == 1)
    def _():
        step(hb_scr, ha_scr)


def _inproj(x, mod, nw, w_main, w_dt, tokens_per_mod):
    t = x.shape[0]
    tm, tn = 1024, PROJ_COLS // 4
    n_tiles = t // tm
    nxt = lambda i: jnp.minimum(i + 1, n_tiles - 1)
    return pl.pallas_call(
        functools.partial(_inproj_kernel, PROJ_COLS // tn),
        grid=(n_tiles, PROJ_COLS // tn),
        in_specs=[pl.BlockSpec((tm, D_MODEL), lambda i, j: (0, 0)),
                  pl.BlockSpec((tm, D_MODEL), lambda i, j: (nxt(i), 0)),
                  pl.BlockSpec((1, 6, D_MODEL), lambda i, j: (0, 0, 0)),
                  pl.BlockSpec((1, 6, D_MODEL), lambda i, j: ((nxt(i) * tm) // tokens_per_mod, 0, 0)),
                  pl.BlockSpec((1, D_MODEL), lambda i, j: (0, 0)),
                  pl.BlockSpec((D_MODEL, tn), lambda i, j: (0, j)),
                  pl.BlockSpec((D_MODEL, LANES), lambda i, j: (0, 0))],
        out_specs=[pl.BlockSpec((tm, tn), lambda i, j: (i, j)),
                   pl.BlockSpec((tm, LANES), lambda i, j: (i, 0))],
        out_shape=[jax.ShapeDtypeStruct((t, PROJ_COLS), BF16),
                   jax.ShapeDtypeStruct((t, LANES), F32)],
        scratch_shapes=[pltpu.VMEM((tm, D_MODEL), BF16), pltpu.VMEM((tm, D_MODEL), BF16)],
        compiler_params=_cparams(("arbitrary", "arbitrary")),
        name="inproj",
    )(x, x, mod, mod, nw, w_main, w_dt)


def _softplus(x):
    return jnp.maximum(x, 0.0) + jnp.log1p(jnp.exp(-jnp.abs(x)))


def _cumsum_rows(a):
    n = a.shape[0]
    rid = lax.broadcasted_iota(jnp.int32, a.shape, 0)
    s = 1
    while s < n:
        a = a + jnp.where(rid >= s, pltpu.roll(a, s, axis=0), 0.0)
        s *= 2
    return a


def _rev_cumsum_rows(a):
    n = a.shape[0]
    rid = lax.broadcasted_iota(jnp.int32, a.shape, 0)
    s = 1
    while s < n:
        a = a + jnp.where(rid < n - s, pltpu.roll(a, n - s, axis=0), 0.0)
        s *= 2
    return a


def _expand_heads(w, e_ref):
    hi = w.astype(BF16)
    lo = (w - hi.astype(F32)).astype(BF16)
    e = e_ref[...]
    return _dot(hi, e) + _dot(lo, e)


def _row_shift_matrix(n):
    r = jnp.arange(n)[:, None]
    c = jnp.arange(n)[None, :]
    return jnp.concatenate([c == r - 1, c == r + 1], axis=0).astype(BF16)


def _conv3_silu(main_ref, prev_ref, next_ref, w_ref, b_ref, shift_ref, first, last):
    x_bf = main_ref[...]
    n = x_bf.shape[0]
    x = x_bf.astype(F32)
    sh = _dot(shift_ref[...], x_bf)
    xm1 = sh[:n, :]
    xp1 = sh[n:, :]
    prow = jnp.where(first, 0.0, prev_ref[...].astype(F32)[-1:, :])
    nrow = jnp.where(last, 0.0, next_ref[...].astype(F32)[0:1, :])
    rid = lax.broadcasted_iota(jnp.int32, (8, x.shape[1]), 0)
    xm1 = jnp.concatenate([jnp.where(rid == 0, prow, xm1[:8, :]), xm1[8:, :]], axis=0)
    xp1 = jnp.concatenate([xp1[:-8, :], jnp.where(rid == 7, nrow, xp1[-8:, :])], axis=0)
    w = w_ref[...]
    y = w[0:1, :] * xm1 + w[1:2, :] * x + w[2:3, :] * xp1 + b_ref[...]
    return _silu(y)


def _dt_and_decay_rates(dt_ref, dtb_ref, alog_ref):
    dt = _softplus(dt_ref[...] + dtb_ref[...])
    a = dt * (-jnp.exp(alog_ref[...]))
    return dt, a


def _pair_rhs(xs_bf, pair, left):
    xp = xs_bf[:, pair * LANES:(pair + 1) * LANES]
    zero = jnp.zeros_like(xp)
    return jnp.concatenate([jnp.where(left, xp, zero), jnp.where(left, zero, xp)], axis=0)


def _ssd_fwd_kernel(has_h0, cps, *refs):
    (xm_ref, bcm_ref, xp_ref, bcp_ref, xn_ref, bcn_ref, dt_ref,
     cwx_ref, cwbc_ref, cbx_ref, cbbc_ref, dtb_ref, alog_ref, ef_ref, s3_ref) = refs[:15]
    pos = 15
    h0_ref = None
    if has_h0:
        h0_ref = refs[pos]
        pos += 1
    xs_ref, bc_ref, hprev_ref, hfin_ref, h_scr = refs[pos:pos + 5]

    c = pl.program_id(1)
    nsteps = pl.num_programs(1)
    first = c == 0
    last = c == nsteps - 1

    @pl.when(first)
    def _():
        if has_h0:
            h_scr[...] = h0_ref[0].T
        else:
            h_scr[...] = jnp.zeros_like(h_scr)

    xs_all = _conv3_silu(xm_ref, xp_ref, xn_ref, cwx_ref, cbx_ref, s3_ref, first, last)
    bc_all = _conv3_silu(bcm_ref, bcp_ref, bcn_ref, cwbc_ref, cbbc_ref, s3_ref, first, last)
    xs_all_bf = xs_all.astype(BF16)
    xs_ref[...] = xs_all_bf
    bc_ref[...] = bc_all.astype(BF16)

    left = lax.broadcasted_iota(jnp.int32, (CHUNK, LANES), 1) < HEADDIM
    h_cur = h_scr[...]
    for u in range(cps):
        rows = slice(u * CHUNK, (u + 1) * CHUNK)
        xs_bf = xs_all_bf[rows, :]
        bc = bc_all[rows, :]
        dt = _softplus(dt_ref[rows, :] + dtb_ref[...])
        a = dt * (-jnp.exp(alog_ref[...]))
        acs = _cumsum_rows(a)
        acs_t = acs.T
        w2_t = dt.T * jnp.exp(acs_t[:, CHUNK - 1:CHUNK] - acs_t)
        cdec = _expand_heads(jnp.broadcast_to(jnp.exp(acs[CHUNK - 1:CHUNK, :]), (8, LANES)), ef_ref)[0:1, :]

        parts = []
        for g in range(GROUPS):
            bt = bc[:, g * STATE:(g + 1) * STATE].T
            for k in range(HEADS_PER_GROUP // 2):
                h = g * HEADS_PER_GROUP + 2 * k
                lhs = jnp.concatenate([(bt * w2_t[h:h + 1, :]).astype(BF16),
                                       (bt * w2_t[h + 1:h + 2, :]).astype(BF16)], axis=1)
                parts.append(_dot(lhs, _pair_rhs(xs_bf, h // 2, left)))
        s_loc = jnp.concatenate(parts, axis=1)

        hprev_ref[u] = h_cur.astype(BF16)
        h_cur = cdec * h_cur + s_loc
    h_scr[...] = h_cur

    @pl.when(last)
    def _():
        hfin_ref[0] = h_cur.T


def _ssd_fwd(proj, dt_raw, h0, consts, nseq, nchunks):
    t = nseq * nchunks * CHUNK
    cps = min(SSD_FWD_CHUNKS_PER_STEP, nchunks)
    nsteps = nchunks // cps
    rows = cps * CHUNK
    halo = 16
    per = rows // halo
    n_halo = t // halo
    gidx = lambda b, c: b * nsteps + c
    prev_blk = lambda b, c: jnp.maximum(gidx(b, c) * per - 1, 0)
    next_blk = lambda b, c: jnp.minimum((gidx(b, c) + 1) * per, n_halo - 1)
    in_specs = [
        pl.BlockSpec((rows, D_SSD), lambda b, c: (gidx(b, c), 1)),
        pl.BlockSpec((rows, 2 * BC_W), lambda b, c: (gidx(b, c), 4)),
        pl.BlockSpec((halo, D_SSD), lambda b, c: (prev_blk(b, c), 1)),
        pl.BlockSpec((halo, 2 * BC_W), lambda b, c: (prev_blk(b, c), 4)),
        pl.BlockSpec((halo, D_SSD), lambda b, c: (next_blk(b, c), 1)),
        pl.BlockSpec((halo, 2 * BC_W), lambda b, c: (next_blk(b, c), 4)),
        pl.BlockSpec((rows, LANES), lambda b, c: (gidx(b, c), 0)),
        pl.BlockSpec((3, D_SSD), lambda b, c: (0, 0)),
        pl.BlockSpec((3, 2 * BC_W), lambda b, c: (0, 0)),
        pl.BlockSpec((1, D_SSD), lambda b, c: (0, 0)),
        pl.BlockSpec((1, 2 * BC_W), lambda b, c: (0, 0)),
        pl.BlockSpec((1, LANES), lambda b, c: (0, 0)),
        pl.BlockSpec((1, LANES), lambda b, c: (0, 0)),
        pl.BlockSpec((LANES, D_SSD), lambda b, c: (0, 0)),
        pl.BlockSpec((2 * rows, rows), lambda b, c: (0, 0)),
    ]
    args = [proj, proj, proj, proj, proj, proj, dt_raw,
            consts["cw_x"], consts["cw_bc"], consts["cb_x"], consts["cb_bc"],
            consts["dt_bias"], consts["a_log"], consts["e_fwd"], _row_shift_matrix(rows)]
    has_h0 = h0 is not None
    if has_h0:
        in_specs.append(pl.BlockSpec((1, D_SSD, STATE), lambda b, c: (b, 0, 0)))
        args.append(h0)
    return pl.pallas_call(
        functools.partial(_ssd_fwd_kernel, has_h0, cps),
        grid=(nseq, nsteps),
        in_specs=in_specs,
        out_specs=[pl.BlockSpec((rows, D_SSD), lambda b, c: (gidx(b, c), 0)),
                   pl.BlockSpec((rows, 2 * BC_W), lambda b, c: (gidx(b, c), 0)),
                   pl.BlockSpec((cps, STATE, D_SSD), lambda b, c: (gidx(b, c), 0, 0)),
                   pl.BlockSpec((1, D_SSD, STATE), lambda b, c: (b, 0, 0))],
        out_shape=[jax.ShapeDtypeStruct((t, D_SSD), BF16),
                   jax.ShapeDtypeStruct((t, 2 * BC_W), BF16),
                   jax.ShapeDtypeStruct((nseq * nchunks, STATE, D_SSD), BF16),
                   jax.ShapeDtypeStruct((nseq, D_SSD, STATE), F32)],
        scratch_shapes=[pltpu.VMEM((STATE, D_SSD), F32)],
        compiler_params=_cparams(("arbitrary", "arbitrary")),
        name="ssd_fwd",
    )(*args)


def _ssd_bwd_kernel(has_h0, cps, *refs):
    (xs_ref, bc_ref, z_ref, dt_ref, hprev_ref, dtb_ref, alog_ref, dvec_ref, nw_ref,
     eb_ref) = refs[:10]
    pos = 10
    h0_ref = None
    if has_h0:
        h0_ref = refs[pos]
        pos += 1
    y_ref, hfin_ref, h_scr = refs[pos:pos + 3]

    c = pl.program_id(1)
    nsteps = pl.num_programs(1)

    @pl.when(c == 0)
    def _():
        if has_h0:
            h_scr[...] = h0_ref[0].T
        else:
            h_scr[...] = jnp.zeros_like(h_scr)

    ri = lax.broadcasted_iota(jnp.int32, (CHUNK, CHUNK), 0)
    ci = lax.broadcasted_iota(jnp.int32, (CHUNK, CHUNK), 1)
    lower = ri >= ci
    diag = ri == ci
    left = lax.broadcasted_iota(jnp.int32, (CHUNK, LANES), 1) < HEADDIM

    h_b = h_scr[...]
    for u in reversed(range(cps)):
        rows = slice(u * CHUNK, (u + 1) * CHUNK)
        xs_bf = xs_ref[rows, :]
        xs = xs_bf.astype(F32)
        bc_bf = bc_ref[rows, :]

        dt = _softplus(dt_ref[rows, :] + dtb_ref[...])
        a = dt * (-jnp.exp(alog_ref[...]))
        acs = _cumsum_rows(a) * LOG2E
        rcs = _rev_cumsum_rows(a) * LOG2E
        acs_t = acs.T
        rcs_t = rcs.T
        dt_t = dt.T
        lg_t = jnp.log2(dt_t)
        rf_t = acs_t - lg_t
        rb_t = rcs_t - lg_t
        w2b_t = dt_t * jnp.exp2(rcs_t[:, 0:1] - rcs_t)
        cdec = _expand_heads(jnp.broadcast_to(jnp.exp2(rcs[0:1, :]), (8, LANES)), eb_ref)[0:1, :]

        h_f = hprev_ref[u]
        h_b_bf = h_b.astype(BF16)

        y_parts = []
        s_parts = []
        for g in range(GROUPS):
            b_g = bc_bf[:, g * STATE:(g + 1) * STATE]
            c_g = bc_bf[:, BC_W + g * STATE:BC_W + (g + 1) * STATE]
            cb = _dot_nt(c_g, b_g)
            bt = b_g.astype(F32).T
            off_f = _dot(c_g, h_f[:, g * GROUP_W:(g + 1) * GROUP_W])
            off_b = _dot(c_g, h_b_bf[:, g * GROUP_W:(g + 1) * GROUP_W])
            for k in range(HEADS_PER_GROUP // 2):
                ms, bs, cfs, cbs = [], [], [], []
                for h in (g * HEADS_PER_GROUP + 2 * k, g * HEADS_PER_GROUP + 2 * k + 1):
                    hb = HEADS + h
                    cf = acs[:, h:h + 1]
                    cbk = rcs[:, hb:hb + 1]
                    e = jnp.exp2(jnp.where(lower, cf - rf_t[h:h + 1, :], cbk - rb_t[hb:hb + 1, :]))
                    e = e + jnp.where(diag, dt_t[hb:hb + 1, :], 0.0)
                    ms.append((cb * e).astype(BF16))
                    bs.append((bt * w2b_t[hb:hb + 1, :]).astype(BF16))
                    cfs.append(cf)
                    cbs.append(cbk)
                lhs = jnp.concatenate([jnp.concatenate(ms, axis=1), jnp.concatenate(bs, axis=1)], axis=0)
                out = _dot(lhs, _pair_rhs(xs_bf, g * HEADS_PER_GROUP // 2 + k, left))
                ef = jnp.exp2(jnp.where(left, cfs[0], cfs[1]))
                eb = jnp.exp2(jnp.where(left, cbs[0], cbs[1]))
                cols = slice(k * LANES, (k + 1) * LANES)
                y_parts.append(out[:CHUNK, :] + ef * off_f[:, cols] + eb * off_b[:, cols])
                s_parts.append(out[CHUNK:, :])
        y = jnp.concatenate(y_parts, axis=1) + dvec_ref[...] * xs
        y = y * _silu(z_ref[rows, :].astype(F32))
        ms_y = jnp.mean(y * y, axis=-1, keepdims=True)
        y_ref[rows, :] = (y * lax.rsqrt(ms_y + EPS) * nw_ref[...]).astype(BF16)

        h_b = cdec * h_b + jnp.concatenate(s_parts, axis=1)
    h_scr[...] = h_b

    @pl.when(c == nsteps - 1)
    def _():
        hfin_ref[0] = h_b.T


def _ssd_bwd(xs, bc, proj, dt_raw, hprev, h0, consts, nseq, nchunks):
    t = nseq * nchunks * CHUNK
    cps = min(SSD_BWD_CHUNKS_PER_STEP, nchunks)
    nsteps = nchunks // cps
    rows = cps * CHUNK
    gidx = lambda b, c: b * nsteps + (nsteps - 1 - c)
    in_specs = [
        pl.BlockSpec((rows, D_SSD), lambda b, c: (gidx(b, c), 0)),
        pl.BlockSpec((rows, 2 * BC_W), lambda b, c: (gidx(b, c), 0)),
        pl.BlockSpec((rows, D_SSD), lambda b, c: (gidx(b, c), 0)),
        pl.BlockSpec((rows, LANES), lambda b, c: (gidx(b, c), 0)),
        pl.BlockSpec((cps, STATE, D_SSD), lambda b, c: (gidx(b, c), 0, 0)),
        pl.BlockSpec((1, LANES), lambda b, c: (0, 0)),
        pl.BlockSpec((1, LANES), lambda b, c: (0, 0)),
        pl.BlockSpec((1, D_SSD), lambda b, c: (0, 0)),
        pl.BlockSpec((1, D_SSD), lambda b, c: (0, 0)),
        pl.BlockSpec((LANES, D_SSD), lambda b, c: (0, 0)),
    ]
    args = [xs, bc, proj, dt_raw, hprev, consts["dt_bias"], consts["a_log"], consts["d_vec"],
            consts["ssd_norm"], consts["e_bwd"]]
    has_h0 = h0 is not None
    if has_h0:
        in_specs.append(pl.BlockSpec((1, D_SSD, STATE), lambda b, c: (b, 0, 0)))
        args.append(h0)
    return pl.pallas_call(
        functools.partial(_ssd_bwd_kernel, has_h0, cps),
        grid=(nseq, nsteps),
        in_specs=in_specs,
        out_specs=[pl.BlockSpec((rows, D_SSD), lambda b, c: (gidx(b, c), 0)),
                   pl.BlockSpec((1, D_SSD, STATE), lambda b, c: (b, 0, 0))],
        out_shape=[jax.ShapeDtypeStruct((t, D_SSD), BF16),
                   jax.ShapeDtypeStruct((nseq, D_SSD, STATE), F32)],
        scratch_shapes=[pltpu.VMEM((STATE, D_SSD), F32)],
        compiler_params=_cparams(("arbitrary", "arbitrary")),
        name="ssd_bwd",
    )(*args)


def _mix_kernel(y_ref, u_ref, v_ref, ga_ref, gb_ref, x_ref, mod_ref, lnw_ref, lnb_ref, ws_ref, bst_ref,
                wbs_ref, wbg_ref, wout_ref, npost_ref, npre2_ref, o_ref, h2_ref, ysgu_scr):
    tm = x_ref.shape[0]
    v = v_ref[...].astype(F32)
    mu = jnp.mean(v, axis=-1, keepdims=True)
    vc = v - mu
    var = jnp.mean(vc * vc, axis=-1, keepdims=True)
    vn = (vc * lax.rsqrt(var + EPS) * lnw_ref[...] + lnb_ref[...]).astype(BF16)
    bst = bst_ref[...]
    for r in range(tm // CHUNK):
        rows = slice(r * CHUNK, (r + 1) * CHUNK)
        for g in range(SGU_GROUPS):
            cols = slice(g * LANES, (g + 1) * LANES)
            s = _dot(ws_ref[g], vn[rows, cols]) + bst[:, g:g + 1]
            ysgu_scr[rows, cols] = (u_ref[rows, cols].astype(F32) * s).astype(BF16)
    br_ssd = _dot(y_ref[...], wbs_ref[...])
    br_sgu = _dot(ysgu_scr[...], wbg_ref[...])
    merged = _sigmoid(ga_ref[...].astype(F32)) * br_ssd + _sigmoid(gb_ref[...].astype(F32)) * br_sgu
    merged = merged.astype(BF16)
    quarter = tm // 4
    for r in range(4):
        rows = slice(r * quarter, (r + 1) * quarter)
        mix = _dot(merged[rows, :], wout_ref[...])
        ms = jnp.mean(mix * mix, axis=-1, keepdims=True)
        x1 = x_ref[rows, :] + mod_ref[0, 2:3, :] * (mix * lax.rsqrt(ms + EPS) * npost_ref[...])
        o_ref[rows, :] = x1
        h2_ref[rows, :] = _modulated_norm(x1, npre2_ref[...], mod_ref[0, 3:4, :], mod_ref[0, 4:5, :]).astype(BF16)


def _mix(y_ssd, proj, x, mod, consts, tokens_per_mod):
    t = x.shape[0]
    tm = 512
    row = lambda i: (i, 0)
    const2 = lambda i: (0, 0)
    return pl.pallas_call(
        _mix_kernel,
        grid=(t // tm,),
        in_specs=[pl.BlockSpec((tm, D_SSD), row),
                  pl.BlockSpec((tm, D_MODEL), lambda i: (i, 5)),
                  pl.BlockSpec((tm, D_MODEL), lambda i: (i, 6)),
                  pl.BlockSpec((tm, D_MODEL), lambda i: (i, 7)),
                  pl.BlockSpec((tm, D_MODEL), lambda i: (i, 8)),
                  pl.BlockSpec((tm, D_MODEL), row),
                  pl.BlockSpec((1, 6, D_MODEL), lambda i: ((i * tm) // tokens_per_mod, 0, 0)),
                  pl.BlockSpec((1, D_MODEL), const2),
                  pl.BlockSpec((1, D_MODEL), const2),
                  pl.BlockSpec((SGU_GROUPS, CHUNK, CHUNK), lambda i: (0, 0, 0)),
                  pl.BlockSpec((CHUNK, SGU_GROUPS), const2),
                  pl.BlockSpec((D_SSD, D_MODEL), const2),
                  pl.BlockSpec((D_MODEL, D_MODEL), const2),
                  pl.BlockSpec((D_MODEL, D_MODEL), const2),
                  pl.BlockSpec((1, D_MODEL), const2),
                  pl.BlockSpec((1, D_MODEL), const2)],
        out_specs=[pl.BlockSpec((tm, D_MODEL), row), pl.BlockSpec((tm, D_MODEL), row)],
        out_shape=[jax.ShapeDtypeStruct((t, D_MODEL), F32), jax.ShapeDtypeStruct((t, D_MODEL), BF16)],
        scratch_shapes=[pltpu.VMEM((tm, D_MODEL), BF16)],
        compiler_params=_cparams(("arbitrary",)),
        name="mix",
    )(y_ssd, proj, proj, proj, proj, x, mod, consts["sgu_norm_w"], consts["sgu_norm_b"], consts["sgu_w"],
      consts["sgu_bt"], consts["w_branch_ssd"], consts["w_branch_sgu"], consts["w_out"], consts["norm_mix_post"],
      consts["norm_ffn_pre"])


def _matmul_kernel(h_ref, w_ref, o_ref):
    o_ref[...] = _dot(h_ref[...], w_ref[...]).astype(BF16)


def _ffn_up(h2, w_up):
    t = h2.shape[0]
    tm, tn = 1024, D_FF
    return pl.pallas_call(
        _matmul_kernel,
        grid=(t // tm, 2 * D_FF // tn),
        in_specs=[pl.BlockSpec((tm, D_MODEL), lambda i, j: (i, 0)),
                  pl.BlockSpec((D_MODEL, tn), lambda i, j: (0, j))],
        out_specs=pl.BlockSpec((tm, tn), lambda i, j: (i, j)),
        out_shape=jax.ShapeDtypeStruct((t, 2 * D_FF), BF16),
        compiler_params=_cparams(("arbitrary", "arbitrary")),
        name="ffn_up",
    )(h2, w_up)


def _gelu_tanh(x):
    h = 0.5 * x
    u = x * (GELU_C + (GELU_C * 0.044715) * (x * x))
    return h + h * jnp.tanh(u)


def _grid_conv(x_bf, prev_bf, next_bf, w, b, first, last):
    x = x_bf.astype(F32)
    tm = x.shape[0]
    prev = jnp.where(first, 0.0, prev_bf.astype(F32))
    nxt = jnp.where(last, 0.0, next_bf.astype(F32))
    ext = jnp.concatenate([prev, x, nxt], axis=0)
    n = ext.shape[0]
    col = lax.broadcasted_iota(jnp.int32, ext.shape, 0) % GRID_W
    shifted = (jnp.where(col == 0, 0.0, pltpu.roll(ext, 1, axis=0)),
               ext,
               jnp.where(col == GRID_W - 1, 0.0, pltpu.roll(ext, n - 1, axis=0)))
    acc = None
    for dy in range(3):
        for dx in range(3):
            term = w[3 * dy + dx:3 * dy + dx + 1, :] * shifted[dx][dy * GRID_W:dy * GRID_W + tm, :]
            acc = term if acc is None else acc + term
    return acc + b


def _seq_conv(x_bf, w, b, seq_len):
    x = x_bf.astype(F32)
    n = x.shape[0]
    pos = lax.broadcasted_iota(jnp.int32, x.shape, 0) % seq_len
    xm1 = jnp.where(pos == 0, 0.0, pltpu.roll(x, 1, axis=0))
    xp1 = jnp.where(pos == seq_len - 1, 0.0, pltpu.roll(x, n - 1, axis=0))
    return w[3:4, :] * xm1 + w[4:5, :] * x + w[5:6, :] * xp1 + b


def _ffn_down_kernel(on_grid, seq_len, tiles_per_seq, *refs):
    if on_grid:
        (up_ref, upp_ref, upn_ref, w_ref, b_ref, wd_ref, x_ref, mod_ref, npost_ref, o_ref, g0_scr, g1_scr, acc_scr) = refs
    else:
        (up_ref, w_ref, b_ref, wd_ref, x_ref, mod_ref, npost_ref, o_ref, g0_scr, g1_scr, acc_scr) = refs
    i = pl.program_id(0)
    first = i % tiles_per_seq == 0
    last = i % tiles_per_seq == tiles_per_seq - 1

    def conv(c0):
        cols = pl.ds(c0, FF_BLK)
        if on_grid:
            return _grid_conv(up_ref[:, cols], upp_ref[:, cols], upn_ref[:, cols], w_ref[:, cols], b_ref[:, cols],
                              first, last)
        return _seq_conv(up_ref[:, cols], w_ref[:, cols], b_ref[:, cols], seq_len)

    def geglu_block(k, dst_ref):
        c0 = pl.multiple_of(k * FF_BLK, FF_BLK)
        a = conv(c0)
        val = conv(pl.multiple_of(c0 + D_FF, FF_BLK))
        dst_ref[...] = (_gelu_tanh(a) * val).astype(BF16)

    def down_block(k, src_ref):
        rows = pl.ds(pl.multiple_of(k * FF_BLK, FF_BLK), FF_BLK)
        acc_scr[...] += _dot(src_ref[...], wd_ref[rows, :])

    acc_scr[...] = jnp.zeros_like(acc_scr)
    geglu_block(0, g0_scr)

    def block_pair(m, carry):
        k = 2 * m + 1
        down_block(k - 1, g0_scr)
        geglu_block(k, g1_scr)
        down_block(k, g1_scr)
        geglu_block(k + 1, g0_scr)
        return carry

    lax.fori_loop(0, (N_FF_BLK - 1) // 2, block_pair, 0)
    down_block(N_FF_BLK - 1, g0_scr)
    f = acc_scr[...]
    ms = jnp.mean(f * f, axis=-1, keepdims=True)
    o_ref[...] = x_ref[...] + mod_ref[0, 5:6, :] * (f * lax.rsqrt(ms + EPS) * npost_ref[...])


def _ffn_down(up, x, mod, consts, tokens_per_mod, on_grid, seq_len):
    t = x.shape[0]
    tm = 512
    tiles_per_seq = max(seq_len // tm, 1)
    rows_per_tile = tm // GRID_W
    n_rows = t // GRID_W
    in_specs = [pl.BlockSpec((tm, 2 * D_FF), lambda i: (i, 0))]
    args = [up]
    if on_grid:
        in_specs += [pl.BlockSpec((GRID_W, 2 * D_FF), lambda i: (jnp.maximum(i * rows_per_tile - 1, 0), 0)),
                     pl.BlockSpec((GRID_W, 2 * D_FF), lambda i: (jnp.minimum((i + 1) * rows_per_tile, n_rows - 1), 0))]
        args += [up, up]
    in_specs += [pl.BlockSpec((9, 2 * D_FF), lambda i: (0, 0)),
                 pl.BlockSpec((1, 2 * D_FF), lambda i: (0, 0)),
                 pl.BlockSpec((D_FF, D_MODEL), lambda i: (0, 0)),
                 pl.BlockSpec((tm, D_MODEL), lambda i: (i, 0)),
                 pl.BlockSpec((1, 6, D_MODEL), lambda i: ((i * tm) // tokens_per_mod, 0, 0)),
                 pl.BlockSpec((1, D_MODEL), lambda i: (0, 0))]
    args += [consts["ffn_conv_w"], consts["ffn_conv_b"], consts["ffn_w_down"], x, mod, consts["norm_ffn_post"]]
    return pl.pallas_call(
        functools.partial(_ffn_down_kernel, on_grid, seq_len, tiles_per_seq),
        grid=(t // tm,),
        in_specs=in_specs,
        out_specs=pl.BlockSpec((tm, D_MODEL), lambda i: (i, 0)),
        out_shape=jax.ShapeDtypeStruct((t, D_MODEL), F32),
        scratch_shapes=[pltpu.VMEM((tm, FF_BLK), BF16), pltpu.VMEM((tm, FF_BLK), BF16),
                        pltpu.VMEM((tm, D_MODEL), F32)],
        compiler_params=_cparams(("arbitrary",)),
        name="ffn_down",
    )(*args)


def _trunk_path(x, mod, h0_f, h0_b, on_grid, consts):
    nseq, seq_len, _ = x.shape
    t = nseq * seq_len
    nchunks = seq_len // CHUNK
    tokens_per_mod = t // mod.shape[0]
    x2d = x.reshape(t, D_MODEL)

    proj, dt_raw = _inproj(x2d, mod, consts["norm_mix_pre"], consts["w_in_main"], consts["w_in_dt"], tokens_per_mod)
    xs, bc, hprev, hf = _ssd_fwd(proj, dt_raw, h0_f, consts, nseq, nchunks)
    y_ssd, hb = _ssd_bwd(xs, bc, proj, dt_raw, hprev, h0_b, consts, nseq, nchunks)
    x1, h2 = _mix(y_ssd, proj, x2d, mod, consts, tokens_per_mod)
    up = _ffn_up(h2, consts["ffn_w_up"])
    x2 = _ffn_down(up, x1, mod, consts, tokens_per_mod, on_grid, seq_len)
    return x2.reshape(nseq, seq_len, D_MODEL), hf, hb


def _head_expansion(offset):
    rows = jnp.arange(LANES)[:, None]
    cols = jnp.arange(D_SSD)[None, :] // HEADDIM
    return (rows == cols + offset).astype(BF16)


def _layer_consts(i, p):
    w_in_main, w_in_dt = _w_in_prep(p["w_in"], i)
    row = lambda v: v.reshape(1, -1).astype(F32)
    pad_lanes = lambda v: jnp.pad(v.reshape(1, -1).astype(F32), ((0, 0), (0, LANES - N_DT)))
    conv_w = p["ssd_conv_w"][i]
    conv_b = p["ssd_conv_b"][i]
    return {
        "norm_mix_pre": row(p["norm_mix_pre"][i]),
        "norm_mix_post": row(p["norm_mix_post"][i]),
        "norm_ffn_pre": row(p["norm_ffn_pre"][i]),
        "norm_ffn_post": row(p["norm_ffn_post"][i]),
        "w_in_main": w_in_main, "w_in_dt": w_in_dt,
        "cw_x": conv_w[:, :D_SSD], "cw_bc": conv_w[:, D_SSD:],
        "cb_x": row(conv_b[:D_SSD]), "cb_bc": row(conv_b[D_SSD:]),
        "dt_bias": pad_lanes(p["ssd_dt_bias"][i]),
        "a_log": pad_lanes(p["ssd_a_log"][i]),
        "d_vec": row(jnp.repeat(p["ssd_d"][i], HEADDIM)),
        "ssd_norm": row(p["ssd_norm"][i]),
        "e_fwd": _head_expansion(0), "e_bwd": _head_expansion(HEADS),
        "sgu_norm_w": row(p["sgu_norm_w"][i]), "sgu_norm_b": row(p["sgu_norm_b"][i]),
        "sgu_w": p["sgu_w"][i].astype(BF16),
        "sgu_bt": jnp.transpose(p["sgu_b"][i]).astype(F32),
        "w_branch_ssd": p["w_branch_ssd"][i].astype(BF16),
        "w_branch_sgu": p["w_branch_sgu"][i].astype(BF16),
        "w_out": p["w_out"][i].astype(BF16),
        "ffn_w_up": p["ffn_w_up"][i].astype(BF16),
        "ffn_conv_w": p["ffn_conv_w"][i].reshape(9, 2 * D_FF).astype(F32),
        "ffn_conv_b": row(p["ffn_conv_b"][i]),
        "ffn_w_down": p["ffn_w_down"][i].astype(BF16),
    }


def kernel(x_prompt, x_sample, state_ssd_fwd, state_ssd_bwd, c, c_ctx, w_mod, b_mod, norm_mix_pre, norm_mix_post, norm_ffn_pre, norm_ffn_post, w_in, ssd_conv_w, ssd_conv_b, ssd_a_log, ssd_dt_bias, ssd_d, ssd_norm, sgu_norm_w, sgu_norm_b, sgu_w, sgu_b, w_branch_ssd, w_branch_sgu, w_out, ffn_w_up, ffn_conv_w, ffn_conv_b, ffn_w_down):
    params = dict(norm_mix_pre=norm_mix_pre, norm_mix_post=norm_mix_post, norm_ffn_pre=norm_ffn_pre,
                  norm_ffn_post=norm_ffn_post, w_in=w_in, ssd_conv_w=ssd_conv_w, ssd_conv_b=ssd_conv_b,
                  ssd_a_log=ssd_a_log, ssd_dt_bias=ssd_dt_bias, ssd_d=ssd_d, ssd_norm=ssd_norm,
                  sgu_norm_w=sgu_norm_w, sgu_norm_b=sgu_norm_b, sgu_w=sgu_w, sgu_b=sgu_b,
                  w_branch_ssd=w_branch_ssd, w_branch_sgu=w_branch_sgu, w_out=w_out, ffn_w_up=ffn_w_up,
                  ffn_conv_w=ffn_conv_w, ffn_conv_b=ffn_conv_b, ffn_w_down=ffn_w_down)
    depth = w_mod.shape[0]
    n_lat = c.shape[0]
    c_rows = jnp.concatenate([c_ctx[None, :], c, jnp.zeros((8 - 1 - n_lat, D_MODEL), F32)], axis=0)
    xp, xs = x_prompt, x_sample
    new_f, new_b = [], []
    for i in range(depth):
        consts = _layer_consts(i, params)
        mod = _mod_vectors(c_rows, w_mod[i], b_mod[i]).reshape(8, 6, D_MODEL)
        xp, hf, hb = _trunk_path(xp, mod[0:1], None, None, False, consts)
        new_f.append(hf.reshape(-1, HEADS, HEADDIM, STATE))
        new_b.append(hb.reshape(-1, HEADS, HEADDIM, STATE))
        xs, _, _ = _trunk_path(xs, mod[1:1 + n_lat],
                               state_ssd_fwd[:, i].reshape(n_lat, D_SSD, STATE),
                               state_ssd_bwd[:, i].reshape(n_lat, D_SSD, STATE), True, consts)
    return (xp, xs, jnp.stack(new_f, axis=1).astype(x_prompt.dtype), jnp.stack(new_b, axis=1).astype(x_prompt.dtype))
```

```python
import functools

import jax
import jax.numpy as jnp
from jax import lax
from jax.experimental import pallas as pl
from jax.experimental.pallas import tpu as pltpu

F32 = jnp.float32
BF16 = jnp.bfloat16

D_MODEL = 1024
GRID_W = 64
EPS = 1e-6
LOG2E = 1.4426950408889634
GELU_C = 0.7978845608028654
D_SSD = 2 * D_MODEL
HEADDIM = 64
HEADS = D_SSD // HEADDIM
STATE = 128
GROUPS = 4
HEADS_PER_GROUP = HEADS // GROUPS
GROUP_W = HEADS_PER_GROUP * HEADDIM
BC_W = GROUPS * STATE
CONV_CH = D_SSD + 2 * BC_W
CHUNK = 128
SSD_FWD_CHUNKS_PER_STEP = 2
SSD_BWD_CHUNKS_PER_STEP = 4
SGU_GROUPS = 8
D_FF = 2816
SPLIT_XBC = D_SSD + CONV_CH
N_DT = 2 * HEADS
LANES = 128
PROJ_COLS = 9216
FF_BLK = 256
N_FF_BLK = D_FF // FF_BLK

VMEM_LIMIT_BYTES = 56 * 1024 * 1024


def _cparams(sem):
    return pltpu.CompilerParams(dimension_semantics=sem, vmem_limit_bytes=VMEM_LIMIT_BYTES)


def _sigmoid(x):
    return 0.5 + 0.5 * jnp.tanh(0.5 * x)


def _silu(x):
    h = 0.5 * x
    return h + h * jnp.tanh(h)


def _dot(a, b):
    return jnp.dot(a, b, preferred_element_type=F32)


def _dot_nt(a, b):
    return lax.dot_general(a, b, (((1,), (1,)), ((), ())), preferred_element_type=F32)


def _mod_kernel(c_ref, w_ref, b_ref, o_ref):
    c = c_ref[...]
    o_ref[...] = jnp.dot(_silu(c), w_ref[...], preferred_element_type=F32,
                         precision=lax.Precision.HIGHEST) + b_ref[...]


def _mod_vectors(c_rows, w_mod, b_mod):
    rows = c_rows.shape[0]
    tn = 1024
    return pl.pallas_call(
        _mod_kernel,
        grid=(6 * D_MODEL // tn,),
        in_specs=[pl.BlockSpec((rows, D_MODEL), lambda j: (0, 0)),
                  pl.BlockSpec((D_MODEL, tn), lambda j: (0, j)),
                  pl.BlockSpec((1, tn), lambda j: (0, j))],
        out_specs=pl.BlockSpec((rows, tn), lambda j: (0, j)),
        out_shape=jax.ShapeDtypeStruct((rows, 6 * D_MODEL), F32),
        compiler_params=_cparams(("arbitrary",)),
        name="mod",
    )(c_rows, w_mod, b_mod.reshape(1, -1))


def _w_in_prep_kernel(w_ref, main_ref, dt_ref):
    w = w_ref[...]
    pad = jnp.zeros((w.shape[0], LANES - N_DT), F32)
    main_ref[...] = jnp.concatenate([w[:, :SPLIT_XBC], w[:, SPLIT_XBC + N_DT:]], axis=1).astype(BF16)
    dt_ref[...] = jnp.concatenate([w[:, SPLIT_XBC:SPLIT_XBC + N_DT], pad], axis=1).astype(BF16)


def _w_in_prep(w_in, layer):
    rows = 128
    steps = D_MODEL // rows
    return pl.pallas_call(
        _w_in_prep_kernel,
        grid=(steps,),
        in_specs=[pl.BlockSpec((rows, PROJ_COLS + N_DT), lambda r: (layer * steps + r, 0))],
        out_specs=[pl.BlockSpec((rows, PROJ_COLS), lambda r: (r, 0)),
                   pl.BlockSpec((rows, LANES), lambda r: (r, 0))],
        out_shape=[jax.ShapeDtypeStruct((D_MODEL, PROJ_COLS), BF16),
                   jax.ShapeDtypeStruct((D_MODEL, LANES), BF16)],
        compiler_params=_cparams(("arbitrary",)),
        name="w_in_prep",
    )(w_in.reshape(-1, PROJ_COLS + N_DT))


def _modulated_norm(x, nw, shift, scale):
    ms = jnp.mean(x * x, axis=-1, keepdims=True)
    return (x * lax.rsqrt(ms + EPS) * nw) * (1.0 + scale) + shift


def _inproj_kernel(n_col_steps, x0_ref, xn_ref, mod0_ref, modn_ref, nw_ref, w_ref, wdt_ref, o_ref, dt_ref, ha_scr, hb_scr):
    i = pl.program_id(0)
    j = pl.program_id(1)
    part = xn_ref.shape[0] // n_col_steps
    rows = pl.ds(pl.multiple_of(j * part, part), part)

    def norm(x, mod_ref):
        return _modulated_norm(x, nw_ref[...], mod_ref[0, 0:1, :], mod_ref[0, 1:2, :]).astype(BF16)

    @pl.when((i == 0) & (j == 0))
    def _():
        ha_scr[...] = norm(x0_ref[...], mod0_ref)

    def step(cur_scr, nxt_scr):
        nxt_scr[rows, :] = norm(xn_ref[rows, :], modn_ref)
        dt_ref[rows, :] = _dot(cur_scr[rows, :], wdt_ref[...])
        o_ref[...] = _dot(cur_scr[...], w_ref[...]).astype(BF16)

    @pl.when(i % 2 == 0)
    def _():
        step(ha_scr, hb_scr)

    @pl.when(i % 2 == 1)
    def _():
        step(hb_scr, ha_scr)


def _inproj(x, mod, nw, w_main, w_dt, tokens_per_mod):
    t = x.shape[0]
    tm, tn = 1024, PROJ_COLS // 4
    n_tiles = t // tm
    nxt = lambda i: jnp.minimum(i + 1, n_tiles - 1)
    return pl.pallas_call(
        functools.partial(_inproj_kernel, PROJ_COLS // tn),
        grid=(n_tiles, PROJ_COLS // tn),
        in_specs=[pl.BlockSpec((tm, D_MODEL), lambda i, j: (0, 0)),
                  pl.BlockSpec((tm, D_MODEL), lambda i, j: (nxt(i), 0)),
                  pl.BlockSpec((1, 6, D_MODEL), lambda i, j: (0, 0, 0)),
                  pl.BlockSpec((1, 6, D_MODEL), lambda i, j: ((nxt(i) * tm) // tokens_per_mod, 0, 0)),
                  pl.BlockSpec((1, D_MODEL), lambda i, j: (0, 0)),
                  pl.BlockSpec((D_MODEL, tn), lambda i, j: (0, j)),
                  pl.BlockSpec((D_MODEL, LANES), lambda i, j: (0, 0))],
        out_specs=[pl.BlockSpec((tm, tn), lambda i, j: (i, j)),
                   pl.BlockSpec((tm, LANES), lambda i, j: (i, 0))],
        out_shape=[jax.ShapeDtypeStruct((t, PROJ_COLS), BF16),
                   jax.ShapeDtypeStruct((t, LANES), F32)],
        scratch_shapes=[pltpu.VMEM((tm, D_MODEL), BF16), pltpu.VMEM((tm, D_MODEL), BF16)],
        compiler_params=_cparams(("arbitrary", "arbitrary")),
        name="inproj",
    )(x, x, mod, mod, nw, w_main, w_dt)


def _softplus(x):
    return jnp.maximum(x, 0.0) + jnp.log1p(jnp.exp(-jnp.abs(x)))


def _cumsum_rows(a):
    n = a.shape[0]
    rid = lax.broadcasted_iota(jnp.int32, a.shape, 0)
    s = 1
    while s < n:
        a = a + jnp.where(rid >= s, pltpu.roll(a, s, axis=0), 0.0)
        s *= 2
    return a


def _rev_cumsum_rows(a):
    n = a.shape[0]
    rid = lax.broadcasted_iota(jnp.int32, a.shape, 0)
    s = 1
    while s < n:
        a = a + jnp.where(rid < n - s, pltpu.roll(a, n - s, axis=0), 0.0)
        s *= 2
    return a


def _expand_heads(w, e_ref):
    hi = w.astype(BF16)
    lo = (w - hi.astype(F32)).astype(BF16)
    e = e_ref[...]
    return _dot(hi, e) + _dot(lo, e)


def _row_shift_matrix(n):
    r = jnp.arange(n)[:, None]
    c = jnp.arange(n)[None, :]
    return jnp.concatenate([c == r - 1, c == r + 1], axis=0).astype(BF16)


def _conv3_silu(main_ref, prev_ref, next_ref, w_ref, b_ref, shift_ref, first, last):
    x_bf = main_ref[...]
    n = x_bf.shape[0]
    x = x_bf.astype(F32)
    sh = _dot(shift_ref[...], x_bf)
    xm1 = sh[:n, :]
    xp1 = sh[n:, :]
    prow = jnp.where(first, 0.0, prev_ref[...].astype(F32)[-1:, :])
    nrow = jnp.where(last, 0.0, next_ref[...].astype(F32)[0:1, :])
    rid = lax.broadcasted_iota(jnp.int32, (8, x.shape[1]), 0)
    xm1 = jnp.concatenate([jnp.where(rid == 0, prow, xm1[:8, :]), xm1[8:, :]], axis=0)
    xp1 = jnp.concatenate([xp1[:-8, :], jnp.where(rid == 7, nrow, xp1[-8:, :])], axis=0)
    w = w_ref[...]
    y = w[0:1, :] * xm1 + w[1:2, :] * x + w[2:3, :] * xp1 + b_ref[...]
    return _silu(y)


def _dt_and_decay_rates(dt_ref, dtb_ref, alog_ref):
    dt = _softplus(dt_ref[...] + dtb_ref[...])
    a = dt * (-jnp.exp(alog_ref[...]))
    return dt, a


def _pair_rhs(xs_bf, pair, left):
    xp = xs_bf[:, pair * LANES:(pair + 1) * LANES]
    zero = jnp.zeros_like(xp)
    return jnp.concatenate([jnp.where(left, xp, zero), jnp.where(left, zero, xp)], axis=0)


def _ssd_fwd_kernel(has_h0, cps, *refs):
    (xm_ref, bcm_ref, xp_ref, bcp_ref, xn_ref, bcn_ref, dt_ref,
     cwx_ref, cwbc_ref, cbx_ref, cbbc_ref, dtb_ref, alog_ref, ef_ref, s3_ref) = refs[:15]
    pos = 15
    h0_ref = None
    if has_h0:
        h0_ref = refs[pos]
        pos += 1
    xs_ref, bc_ref, hprev_ref, hfin_ref, h_scr = refs[pos:pos + 5]

    c = pl.program_id(1)
    nsteps = pl.num_programs(1)
    first = c == 0
    last = c == nsteps - 1

    @pl.when(first)
    def _():
        if has_h0:
            h_scr[...] = h0_ref[0].T
        else:
            h_scr[...] = jnp.zeros_like(h_scr)

    xs_all = _conv3_silu(xm_ref, xp_ref, xn_ref, cwx_ref, cbx_ref, s3_ref, first, last)
    bc_all = _conv3_silu(bcm_ref, bcp_ref, bcn_ref, cwbc_ref, cbbc_ref, s3_ref, first, last)
    xs_all_bf = xs_all.astype(BF16)
    xs_ref[...] = xs_all_bf
    bc_ref[...] = bc_all.astype(BF16)

    left = lax.broadcasted_iota(jnp.int32, (CHUNK, LANES), 1) < HEADDIM
    h_cur = h_scr[...]
    for u in range(cps):
        rows = slice(u * CHUNK, (u + 1) * CHUNK)
        xs_bf = xs_all_bf[rows, :]
        bc = bc_all[rows, :]
        dt = _softplus(dt_ref[rows, :] + dtb_ref[...])
        a = dt * (-jnp.exp(alog_ref[...]))
        acs = _cumsum_rows(a)
        acs_t = acs.T
        w2_t = dt.T * jnp.exp(acs_t[:, CHUNK - 1:CHUNK] - acs_t)
        cdec = _expand_heads(jnp.broadcast_to(jnp.exp(acs[CHUNK - 1:CHUNK, :]), (8, LANES)), ef_ref)[0:1, :]

        parts = []
        for g in range(GROUPS):
            bt = bc[:, g * STATE:(g + 1) * STATE].T
            for k in range(HEADS_PER_GROUP // 2):
                h = g * HEADS_PER_GROUP + 2 * k
                lhs = jnp.concatenate([(bt * w2_t[h:h + 1, :]).astype(BF16),
                                       (bt * w2_t[h + 1:h + 2, :]).astype(BF16)], axis=1)
                parts.append(_dot(lhs, _pair_rhs(xs_bf, h // 2, left)))
        s_loc = jnp.concatenate(parts, axis=1)

        hprev_ref[u] = h_cur.astype(BF16)
        h_cur = cdec * h_cur + s_loc
    h_scr[...] = h_cur

    @pl.when(last)
    def _():
        hfin_ref[0] = h_cur.T


def _ssd_fwd(proj, dt_raw, h0, consts, nseq, nchunks):
    t = nseq * nchunks * CHUNK
    cps = min(SSD_FWD_CHUNKS_PER_STEP, nchunks)
    nsteps = nchunks // cps
    rows = cps * CHUNK
    halo = 16
    per = rows // halo
    n_halo = t // halo
    gidx = lambda b, c: b * nsteps + c
    prev_blk = lambda b, c: jnp.maximum(gidx(b, c) * per - 1, 0)
    next_blk = lambda b, c: jnp.minimum((gidx(b, c) + 1) * per, n_halo - 1)
    in_specs = [
        pl.BlockSpec((rows, D_SSD), lambda b, c: (gidx(b, c), 1)),
        pl.BlockSpec((rows, 2 * BC_W), lambda b, c: (gidx(b, c), 4)),
        pl.BlockSpec((halo, D_SSD), lambda b, c: (prev_blk(b, c), 1)),
        pl.BlockSpec((halo, 2 * BC_W), lambda b, c: (prev_blk(b, c), 4)),
        pl.BlockSpec((halo, D_SSD), lambda b, c: (next_blk(b, c), 1)),
        pl.BlockSpec((halo, 2 * BC_W), lambda b, c: (next_blk(b, c), 4)),
        pl.BlockSpec((rows, LANES), lambda b, c: (gidx(b, c), 0)),
        pl.BlockSpec((3, D_SSD), lambda b, c: (0, 0)),
        pl.BlockSpec((3, 2 * BC_W), lambda b, c: (0, 0)),
        pl.BlockSpec((1, D_SSD), lambda b, c: (0, 0)),
        pl.BlockSpec((1, 2 * BC_W), lambda b, c: (0, 0)),
        pl.BlockSpec((1, LANES), lambda b, c: (0, 0)),
        pl.BlockSpec((1, LANES), lambda b, c: (0, 0)),
        pl.BlockSpec((LANES, D_SSD), lambda b, c: (0, 0)),
        pl.BlockSpec((2 * rows, rows), lambda b, c: (0, 0)),
    ]
    args = [proj, proj, proj, proj, proj, proj, dt_raw,
            consts["cw_x"], consts["cw_bc"], consts["cb_x"], consts["cb_bc"],
            consts["dt_bias"], consts["a_log"], consts["e_fwd"], _row_shift_matrix(rows)]
    has_h0 = h0 is not None
    if has_h0:
        in_specs.append(pl.BlockSpec((1, D_SSD, STATE), lambda b, c: (b, 0, 0)))
        args.append(h0)
    return pl.pallas_call(
        functools.partial(_ssd_fwd_kernel, has_h0, cps),
        grid=(nseq, nsteps),
        in_specs=in_specs,
        out_specs=[pl.BlockSpec((rows, D_SSD), lambda b, c: (gidx(b, c), 0)),
                   pl.BlockSpec((rows, 2 * BC_W), lambda b, c: (gidx(b, c), 0)),
                   pl.BlockSpec((cps, STATE, D_SSD), lambda b, c: (gidx(b, c), 0, 0)),
                   pl.BlockSpec((1, D_SSD, STATE), lambda b, c: (b, 0, 0))],
        out_shape=[jax.ShapeDtypeStruct((t, D_SSD), BF16),
                   jax.ShapeDtypeStruct((t, 2 * BC_W), BF16),
                   jax.ShapeDtypeStruct((nseq * nchunks, STATE, D_SSD), BF16),
                   jax.ShapeDtypeStruct((nseq, D_SSD, STATE), F32)],
        scratch_shapes=[pltpu.VMEM((STATE, D_SSD), F32)],
        compiler_params=_cparams(("arbitrary", "arbitrary")),
        name="ssd_fwd",
    )(*args)


def _ssd_bwd_kernel(has_h0, cps, *refs):
    (xs_ref, bc_ref, z_ref, dt_ref, hprev_ref, dtb_ref, alog_ref, dvec_ref, nw_ref,
     eb_ref) = refs[:10]
    pos = 10
    h0_ref = None
    if has_h0:
        h0_ref = refs[pos]
        pos += 1
    y_ref, hfin_ref, h_scr = refs[pos:pos + 3]

    c = pl.program_id(1)
    nsteps = pl.num_programs(1)

    @pl.when(c == 0)
    def _():
        if has_h0:
            h_scr[...] = h0_ref[0].T
        else:
            h_scr[...] = jnp.zeros_like(h_scr)

    ri = lax.broadcasted_iota(jnp.int32, (CHUNK, CHUNK), 0)
    ci = lax.broadcasted_iota(jnp.int32, (CHUNK, CHUNK), 1)
    lower = ri >= ci
    diag = ri == ci
    left = lax.broadcasted_iota(jnp.int32, (CHUNK, LANES), 1) < HEADDIM

    h_b = h_scr[...]
    for u in reversed(range(cps)):
        rows = slice(u * CHUNK, (u + 1) * CHUNK)
        xs_bf = xs_ref[rows, :]
        xs = xs_bf.astype(F32)
        bc_bf = bc_ref[rows, :]

        dt = _softplus(dt_ref[rows, :] + dtb_ref[...])
        a = dt * (-jnp.exp(alog_ref[...]))
        acs = _cumsum_rows(a) * LOG2E
        rcs = _rev_cumsum_rows(a) * LOG2E
        acs_t = acs.T
        rcs_t = rcs.T
        dt_t = dt.T
        lg_t = jnp.log2(dt_t)
        rf_t = acs_t - lg_t
        rb_t = rcs_t - lg_t
        w2b_t = dt_t * jnp.exp2(rcs_t[:, 0:1] - rcs_t)
        cdec = _expand_heads(jnp.broadcast_to(jnp.exp2(rcs[0:1, :]), (8, LANES)), eb_ref)[0:1, :]

        h_f = hprev_ref[u]
        h_b_bf = h_b.astype(BF16)

        y_parts = []
        s_parts = []
        for g in range(GROUPS):
            b_g = bc_bf[:, g * STATE:(g + 1) * STATE]
            c_g = bc_bf[:, BC_W + g * STATE:BC_W + (g + 1) * STATE]
            cb = _dot_nt(c_g, b_g)
            bt = b_g.astype(F32).T
            off_f = _dot(c_g, h_f[:, g * GROUP_W:(g + 1) * GROUP_W])
            off_b = _dot(c_g, h_b_bf[:, g * GROUP_W:(g + 1) * GROUP_W])
            for k in range(HEADS_PER_GROUP // 2):
                ms, bs, cfs, cbs = [], [], [], []
                for h in (g * HEADS_PER_GROUP + 2 * k, g * HEADS_PER_GROUP + 2 * k + 1):
                    hb = HEADS + h
                    cf = acs[:, h:h + 1]
                    cbk = rcs[:, hb:hb + 1]
                    e = jnp.exp2(jnp.where(lower, cf - rf_t[h:h + 1, :], cbk - rb_t[hb:hb + 1, :]))
                    e = e + jnp.where(diag, dt_t[hb:hb + 1, :], 0.0)
                    ms.append((cb * e).astype(BF16))
                    bs.append((bt * w2b_t[hb:hb + 1, :]).astype(BF16))
                    cfs.append(cf)
                    cbs.append(cbk)
                lhs = jnp.concatenate([jnp.concatenate(ms, axis=1), jnp.concatenate(bs, axis=1)], axis=0)
                out = _dot(lhs, _pair_rhs(xs_bf, g * HEADS_PER_GROUP // 2 + k, left))
                ef = jnp.exp2(jnp.where(left, cfs[0], cfs[1]))
                eb = jnp.exp2(jnp.where(left, cbs[0], cbs[1]))
                cols = slice(k * LANES, (k + 1) * LANES)
                y_parts.append(out[:CHUNK, :] + ef * off_f[:, cols] + eb * off_b[:, cols])
                s_parts.append(out[CHUNK:, :])
        y = jnp.concatenate(y_parts, axis=1) + dvec_ref[...] * xs
        y = y * _silu(z_ref[rows, :].astype(F32))
        ms_y = jnp.mean(y * y, axis=-1, keepdims=True)
        y_ref[rows, :] = (y * lax.rsqrt(ms_y + EPS) * nw_ref[...]).astype(BF16)

        h_b = cdec * h_b + jnp.concatenate(s_parts, axis=1)
    h_scr[...] = h_b

    @pl.when(c == nsteps - 1)
    def _():
        hfin_ref[0] = h_b.T


def _ssd_bwd(xs, bc, proj, dt_raw, hprev, h0, consts, nseq, nchunks):
    t = nseq * nchunks * CHUNK
    cps = min(SSD_BWD_CHUNKS_PER_STEP, nchunks)
    nsteps = nchunks // cps
    rows = cps * CHUNK
    gidx = lambda b, c: b * nsteps + (nsteps - 1 - c)
    in_specs = [
        pl.BlockSpec((rows, D_SSD), lambda b, c: (gidx(b, c), 0)),
        pl.BlockSpec((rows, 2 * BC_W), lambda b, c: (gidx(b, c), 0)),
        pl.BlockSpec((rows, D_SSD), lambda b, c: (gidx(b, c), 0)),
        pl.BlockSpec((rows, LANES), lambda b, c: (gidx(b, c), 0)),
        pl.BlockSpec((cps, STATE, D_SSD), lambda b, c: (gidx(b, c), 0, 0)),
        pl.BlockSpec((1, LANES), lambda b, c: (0, 0)),
        pl.BlockSpec((1, LANES), lambda b, c: (0, 0)),
        pl.BlockSpec((1, D_SSD), lambda b, c: (0, 0)),
        pl.BlockSpec((1, D_SSD), lambda b, c: (0, 0)),
        pl.BlockSpec((LANES, D_SSD), lambda b, c: (0, 0)),
    ]
    args = [xs, bc, proj, dt_raw, hprev, consts["dt_bias"], consts["a_log"], consts["d_vec"],
            consts["ssd_norm"], consts["e_bwd"]]
    has_h0 = h0 is not None
    if has_h0:
        in_specs.append(pl.BlockSpec((1, D_SSD, STATE), lambda b, c: (b, 0, 0)))
        args.append(h0)
    return pl.pallas_call(
        functools.partial(_ssd_bwd_kernel, has_h0, cps),
        grid=(nseq, nsteps),
        in_specs=in_specs,
        out_specs=[pl.BlockSpec((rows, D_SSD), lambda b, c: (gidx(b, c), 0)),
                   pl.BlockSpec((1, D_SSD, STATE), lambda b, c: (b, 0, 0))],
        out_shape=[jax.ShapeDtypeStruct((t, D_SSD), BF16),
                   jax.ShapeDtypeStruct((nseq, D_SSD, STATE), F32)],
        scratch_shapes=[pltpu.VMEM((STATE, D_SSD), F32)],
        compiler_params=_cparams(("arbitrary", "arbitrary")),
        name="ssd_bwd",
    )(*args)


def _mix_kernel(y_ref, u_ref, v_ref, ga_ref, gb_ref, x_ref, mod_ref, lnw_ref, lnb_ref, ws_ref, bst_ref,
                wbs_ref, wbg_ref, wout_ref, npost_ref, npre2_ref, o_ref, h2_ref, ysgu_scr):
    tm = x_ref.shape[0]
    v = v_ref[...].astype(F32)
    mu = jnp.mean(v, axis=-1, keepdims=True)
    vc = v - mu
    var = jnp.mean(vc * vc, axis=-1, keepdims=True)
    vn = (vc * lax.rsqrt(var + EPS) * lnw_ref[...] + lnb_ref[...]).astype(BF16)
    bst = bst_ref[...]
    for r in range(tm // CHUNK):
        rows = slice(r * CHUNK, (r + 1) * CHUNK)
        for g in range(SGU_GROUPS):
            cols = slice(g * LANES, (g + 1) * LANES)
            s = _dot(ws_ref[g], vn[rows, cols]) + bst[:, g:g + 1]
            ysgu_scr[rows, cols] = (u_ref[rows, cols].astype(F32) * s).astype(BF16)
    br_ssd = _dot(y_ref[...], wbs_ref[...])
    br_sgu = _dot(ysgu_scr[...], wbg_ref[...])
    merged = _sigmoid(ga_ref[...].astype(F32)) * br_ssd + _sigmoid(gb_ref[...].astype(F32)) * br_sgu
    merged = merged.astype(BF16)
    quarter = tm // 4
    for r in range(4):
        rows = slice(r * quarter, (r + 1) * quarter)
        mix = _dot(merged[rows, :], wout_ref[...])
        ms = jnp.mean(mix * mix, axis=-1, keepdims=True)
        x1 = x_ref[rows, :] + mod_ref[0, 2:3, :] * (mix * lax.rsqrt(ms + EPS) * npost_ref[...])
        o_ref[rows, :] = x1
        h2_ref[rows, :] = _modulated_norm(x1, npre2_ref[...], mod_ref[0, 3:4, :], mod_ref[0, 4:5, :]).astype(BF16)


def _mix(y_ssd, proj, x, mod, consts, tokens_per_mod):
    t = x.shape[0]
    tm = 512
    row = lambda i: (i, 0)
    const2 = lambda i: (0, 0)
    return pl.pallas_call(
        _mix_kernel,
        grid=(t // tm,),
        in_specs=[pl.BlockSpec((tm, D_SSD), row),
                  pl.BlockSpec((tm, D_MODEL), lambda i: (i, 5)),
                  pl.BlockSpec((tm, D_MODEL), lambda i: (i, 6)),
                  pl.BlockSpec((tm, D_MODEL), lambda i: (i, 7)),
                  pl.BlockSpec((tm, D_MODEL), lambda i: (i, 8)),
                  pl.BlockSpec((tm, D_MODEL), row),
                  pl.BlockSpec((1, 6, D_MODEL), lambda i: ((i * tm) // tokens_per_mod, 0, 0)),
                  pl.BlockSpec((1, D_MODEL), const2),
                  pl.BlockSpec((1, D_MODEL), const2),
                  pl.BlockSpec((SGU_GROUPS, CHUNK, CHUNK), lambda i: (0, 0, 0)),
                  pl.BlockSpec((CHUNK, SGU_GROUPS), const2),
                  pl.BlockSpec((D_SSD, D_MODEL), const2),
                  pl.BlockSpec((D_MODEL, D_MODEL), const2),
                  pl.BlockSpec((D_MODEL, D_MODEL), const2),
                  pl.BlockSpec((1, D_MODEL), const2),
                  pl.BlockSpec((1, D_MODEL), const2)],
        out_specs=[pl.BlockSpec((tm, D_MODEL), row), pl.BlockSpec((tm, D_MODEL), row)],
        out_shape=[jax.ShapeDtypeStruct((t, D_MODEL), F32), jax.ShapeDtypeStruct((t, D_MODEL), BF16)],
        scratch_shapes=[pltpu.VMEM((tm, D_MODEL), BF16)],
        compiler_params=_cparams(("arbitrary",)),
        name="mix",
    )(y_ssd, proj, proj, proj, proj, x, mod, consts["sgu_norm_w"], consts["sgu_norm_b"], consts["sgu_w"],
      consts["sgu_bt"], consts["w_branch_ssd"], consts["w_branch_sgu"], consts["w_out"], consts["norm_mix_post"],
      consts["norm_ffn_pre"])


def _matmul_kernel(h_ref, w_ref, o_ref):
    o_ref[...] = _dot(h_ref[...], w_ref[...]).astype(BF16)


def _ffn_up(h2, w_up):
    t = h2.shape[0]
    tm, tn = 1024, D_FF
    return pl.pallas_call(
        _matmul_kernel,
        grid=(t // tm, 2 * D_FF // tn),
        in_specs=[pl.BlockSpec((tm, D_MODEL), lambda i, j: (i, 0)),
                  pl.BlockSpec((D_MODEL, tn), lambda i, j: (0, j))],
        out_specs=pl.BlockSpec((tm, tn), lambda i, j: (i, j)),
        out_shape=jax.ShapeDtypeStruct((t, 2 * D_FF), BF16),
        compiler_params=_cparams(("arbitrary", "arbitrary")),
        name="ffn_up",
    )(h2, w_up)


def _gelu_tanh(x):
    h = 0.5 * x
    u = x * (GELU_C + (GELU_C * 0.044715) * (x * x))
    return h + h * jnp.tanh(u)


def _grid_conv(x_bf, prev_bf, next_bf, w, b, first, last):
    x = x_bf.astype(F32)
    tm = x.shape[0]
    prev = jnp.where(first, 0.0, prev_bf.astype(F32))
    nxt = jnp.where(last, 0.0, next_bf.astype(F32))
    ext = jnp.concatenate([prev, x, nxt], axis=0)
    n = ext.shape[0]
    col = lax.broadcasted_iota(jnp.int32, ext.shape, 0) % GRID_W
    shifted = (jnp.where(col == 0, 0.0, pltpu.roll(ext, 1, axis=0)),
               ext,
               jnp.where(col == GRID_W - 1, 0.0, pltpu.roll(ext, n - 1, axis=0)))
    acc = None
    for dy in range(3):
        for dx in range(3):
            term = w[3 * dy + dx:3 * dy + dx + 1, :] * shifted[dx][dy * GRID_W:dy * GRID_W + tm, :]
            acc = term if acc is None else acc + term
    return acc + b


def _seq_conv(x_bf, w, b, seq_len):
    x = x_bf.astype(F32)
    n = x.shape[0]
    pos = lax.broadcasted_iota(jnp.int32, x.shape, 0) % seq_len
    xm1 = jnp.where(pos == 0, 0.0, pltpu.roll(x, 1, axis=0))
    xp1 = jnp.where(pos == seq_len - 1, 0.0, pltpu.roll(x, n - 1, axis=0))
    return w[3:4, :] * xm1 + w[4:5, :] * x + w[5:6, :] * xp1 + b


def _ffn_down_kernel(on_grid, seq_len, tiles_per_seq, *refs):
    if on_grid:
        (up_ref, upp_ref, upn_ref, w_ref, b_ref, wd_ref, x_ref, mod_ref, npost_ref, o_ref, g0_scr, g1_scr, acc_scr) = refs
    else:
        (up_ref, w_ref, b_ref, wd_ref, x_ref, mod_ref, npost_ref, o_ref, g0_scr, g1_scr, acc_scr) = refs
    i = pl.program_id(0)
    first = i % tiles_per_seq == 0
    last = i % tiles_per_seq == tiles_per_seq - 1

    def conv(c0):
        cols = pl.ds(c0, FF_BLK)
        if on_grid:
            return _grid_conv(up_ref[:, cols], upp_ref[:, cols], upn_ref[:, cols], w_ref[:, cols], b_ref[:, cols],
                              first, last)
        return _seq_conv(up_ref[:, cols], w_ref[:, cols], b_ref[:, cols], seq_len)

    def geglu_block(k, dst_ref):
        c0 = pl.multiple_of(k * FF_BLK, FF_BLK)
        a = conv(c0)
        val = conv(pl.multiple_of(c0 + D_FF, FF_BLK))
        dst_ref[...] = (_gelu_tanh(a) * val).astype(BF16)

    def down_block(k, src_ref):
        rows = pl.ds(pl.multiple_of(k * FF_BLK, FF_BLK), FF_BLK)
        acc_scr[...] += _dot(src_ref[...], wd_ref[rows, :])

    acc_scr[...] = jnp.zeros_like(acc_scr)
    geglu_block(0, g0_scr)

    def block_pair(m, carry):
        k = 2 * m + 1
        down_block(k - 1, g0_scr)
        geglu_block(k, g1_scr)
        down_block(k, g1_scr)
        geglu_block(k + 1, g0_scr)
        return carry

    lax.fori_loop(0, (N_FF_BLK - 1) // 2, block_pair, 0)
    down_block(N_FF_BLK - 1, g0_scr)
    f = acc_scr[...]
    ms = jnp.mean(f * f, axis=-1, keepdims=True)
    o_ref[...] = x_ref[...] + mod_ref[0, 5:6, :] * (f * lax.rsqrt(ms + EPS) * npost_ref[...])


def _ffn_down(up, x, mod, consts, tokens_per_mod, on_grid, seq_len):
    t = x.shape[0]
    tm = 512
    tiles_per_seq = max(seq_len // tm, 1)
    rows_per_tile = tm // GRID_W
    n_rows = t // GRID_W
    in_specs = [pl.BlockSpec((tm, 2 * D_FF), lambda i: (i, 0))]
    args = [up]
    if on_grid:
        in_specs += [pl.BlockSpec((GRID_W, 2 * D_FF), lambda i: (jnp.maximum(i * rows_per_tile - 1, 0), 0)),
                     pl.BlockSpec((GRID_W, 2 * D_FF), lambda i: (jnp.minimum((i + 1) * rows_per_tile, n_rows - 1), 0))]
        args += [up, up]
    in_specs += [pl.BlockSpec((9, 2 * D_FF), lambda i: (0, 0)),
                 pl.BlockSpec((1, 2 * D_FF), lambda i: (0, 0)),
                 pl.BlockSpec((D_FF, D_MODEL), lambda i: (0, 0)),
                 pl.BlockSpec((tm, D_MODEL), lambda i: (i, 0)),
                 pl.BlockSpec((1, 6, D_MODEL), lambda i: ((i * tm) // tokens_per_mod, 0, 0)),
                 pl.BlockSpec((1, D_MODEL), lambda i: (0, 0))]
    args += [consts["ffn_conv_w"], consts["ffn_conv_b"], consts["ffn_w_down"], x, mod, consts["norm_ffn_post"]]
    return pl.pallas_call(
        functools.partial(_ffn_down_kernel, on_grid, seq_len, tiles_per_seq),
        grid=(t // tm,),
        in_specs=in_specs,
        out_specs=pl.BlockSpec((tm, D_MODEL), lambda i: (i, 0)),
        out_shape=jax.ShapeDtypeStruct((t, D_MODEL), F32),
        scratch_shapes=[pltpu.VMEM((tm, FF_BLK), BF16), pltpu.VMEM((tm, FF_BLK), BF16),
                        pltpu.VMEM((tm, D_MODEL), F32)],
        compiler_params=_cparams(("arbitrary",)),
        name="ffn_down",
    )(*args)


def _trunk_path(x, mod, h0_f, h0_b, on_grid, consts):
    nseq, seq_len, _ = x.shape
    t = nseq * seq_len
    nchunks = seq_len // CHUNK
    tokens_per_mod = t // mod.shape[0]
    x2d = x.reshape(t, D_MODEL)

    proj, dt_raw = _inproj(x2d, mod, consts["norm_mix_pre"], consts["w_in_main"], consts["w_in_dt"], tokens_per_mod)
    xs, bc, hprev, hf = _ssd_fwd(proj, dt_raw, h0_f, consts, nseq, nchunks)
    y_ssd, hb = _ssd_bwd(xs, bc, proj, dt_raw, hprev, h0_b, consts, nseq, nchunks)
    x1, h2 = _mix(y_ssd, proj, x2d, mod, consts, tokens_per_mod)
    up = _ffn_up(h2, consts["ffn_w_up"])
    x2 = _ffn_down(up, x1, mod, consts, tokens_per_mod, on_grid, seq_len)
    return x2.reshape(nseq, seq_len, D_MODEL), hf, hb


def _head_expansion(offset):
    rows = jnp.arange(LANES)[:, None]
    cols = jnp.arange(D_SSD)[None, :] // HEADDIM
    return (rows == cols + offset).astype(BF16)


def _layer_consts(i, p):
    w_in_main, w_in_dt = _w_in_prep(p["w_in"], i)
    row = lambda v: v.reshape(1, -1).astype(F32)
    pad_lanes = lambda v: jnp.pad(v.reshape(1, -1).astype(F32), ((0, 0), (0, LANES - N_DT)))
    conv_w = p["ssd_conv_w"][i]
    conv_b = p["ssd_conv_b"][i]
    return {
        "norm_mix_pre": row(p["norm_mix_pre"][i]),
        "norm_mix_post": row(p["norm_mix_post"][i]),
        "norm_ffn_pre": row(p["norm_ffn_pre"][i]),
        "norm_ffn_post": row(p["norm_ffn_post"][i]),
        "w_in_main": w_in_main, "w_in_dt": w_in_dt,
        "cw_x": conv_w[:, :D_SSD], "cw_bc": conv_w[:, D_SSD:],
        "cb_x": row(conv_b[:D_SSD]), "cb_bc": row(conv_b[D_SSD:]),
        "dt_bias": pad_lanes(p["ssd_dt_bias"][i]),
        "a_log": pad_lanes(p["ssd_a_log"][i]),
        "d_vec": row(jnp.repeat(p["ssd_d"][i], HEADDIM)),
        "ssd_norm": row(p["ssd_norm"][i]),
        "e_fwd": _head_expansion(0), "e_bwd": _head_expansion(HEADS),
        "sgu_norm_w": row(p["sgu_norm_w"][i]), "sgu_norm_b": row(p["sgu_norm_b"][i]),
        "sgu_w": p["sgu_w"][i].astype(BF16),
        "sgu_bt": jnp.transpose(p["sgu_b"][i]).astype(F32),
        "w_branch_ssd": p["w_branch_ssd"][i].astype(BF16),
        "w_branch_sgu": p["w_branch_sgu"][i].astype(BF16),
        "w_out": p["w_out"][i].astype(BF16),
        "ffn_w_up": p["ffn_w_up"][i].astype(BF16),
        "ffn_conv_w": p["ffn_conv_w"][i].reshape(9, 2 * D_FF).astype(F32),
        "ffn_conv_b": row(p["ffn_conv_b"][i]),
        "ffn_w_down": p["ffn_w_down"][i].astype(BF16),
    }


def kernel(x_prompt, x_sample, state_ssd_fwd, state_ssd_bwd, c, c_ctx, w_mod, b_mod, norm_mix_pre, norm_mix_post, norm_ffn_pre, norm_ffn_post, w_in, ssd_conv_w, ssd_conv_b, ssd_a_log, ssd_dt_bias, ssd_d, ssd_norm, sgu_norm_w, sgu_norm_b, sgu_w, sgu_b, w_branch_ssd, w_branch_sgu, w_out, ffn_w_up, ffn_conv_w, ffn_conv_b, ffn_w_down):
    params = dict(norm_mix_pre=norm_mix_pre, norm_mix_post=norm_mix_post, norm_ffn_pre=norm_ffn_pre,
                  norm_ffn_post=norm_ffn_post, w_in=w_in, ssd_conv_w=ssd_conv_w, ssd_conv_b=ssd_conv_b,
                  ssd_a_log=ssd_a_log, ssd_dt_bias=ssd_dt_bias, ssd_d=ssd_d, ssd_norm=ssd_norm,
                  sgu_norm_w=sgu_norm_w, sgu_norm_b=sgu_norm_b, sgu_w=sgu_w, sgu_b=sgu_b,
                  w_branch_ssd=w_branch_ssd, w_branch_sgu=w_branch_sgu, w_out=w_out, ffn_w_up=ffn_w_up,
                  ffn_conv_w=ffn_conv_w, ffn_conv_b=ffn_conv_b, ffn_w_down=ffn_w_down)
    depth = w_mod.shape[0]
    n_lat = c.shape[0]
    c_rows = jnp.concatenate([c_ctx[None, :], c, jnp.zeros((8 - 1 - n_lat, D_MODEL), F32)], axis=0)
    xp, xs = x_prompt, x_sample
    new_f, new_b = [], []
    for i in range(depth):
        consts = _layer_consts(i, params)
        mod = _mod_vectors(c_rows, w_mod[i], b_mod[i]).reshape(8, 6, D_MODEL)
        xp, hf, hb = _trunk_path(xp, mod[0:1], None, None, False, consts)
        new_f.append(hf.reshape(-1, HEADS, HEADDIM, STATE))
        new_b.append(hb.reshape(-1, HEADS, HEADDIM, STATE))
        xs, _, _ = _trunk_path(xs, mod[1:1 + n_lat],
                               state_ssd_fwd[:, i].reshape(n_lat, D_SSD, STATE),
                               state_ssd_bwd[:, i].reshape(n_lat, D_SSD, STATE), True, consts)
    return (xp, xs, jnp.stack(new_f, axis=1).astype(x_prompt.dtype), jnp.stack(new_b, axis=1).astype(x_prompt.dtype))
```

```python
import functools

import jax
import jax.numpy as jnp
from jax import lax
from jax.experimental import pallas as pl
from jax.experimental.pallas import tpu as pltpu

F32 = jnp.float32
BF16 = jnp.bfloat16

D_MODEL = 1024
GRID_W = 64
EPS = 1e-6
LOG2E = 1.4426950408889634
GELU_C = 0.7978845608028654
D_SSD = 2 * D_MODEL
HEADDIM = 64
HEADS = D_SSD // HEADDIM
STATE = 128
GROUPS = 4
HEADS_PER_GROUP = HEADS // GROUPS
GROUP_W = HEADS_PER_GROUP * HEADDIM
BC_W = GROUPS * STATE
CONV_CH = D_SSD + 2 * BC_W
CHUNK = 128
SSD_FWD_CHUNKS_PER_STEP = 2
SSD_BWD_CHUNKS_PER_STEP = 4
SGU_GROUPS = 8
D_FF = 2816
SPLIT_XBC = D_SSD + CONV_CH
N_DT = 2 * HEADS
LANES = 128
PROJ_B_COLS = 4 * D_MODEL
FF_BLK = 256
N_FF_BLK = D_FF // FF_BLK

VMEM_LIMIT_BYTES = 56 * 1024 * 1024


def _cparams(sem):
    return pltpu.CompilerParams(dimension_semantics=sem, vmem_limit_bytes=VMEM_LIMIT_BYTES)


def _sigmoid(x):
    return 0.5 + 0.5 * jnp.tanh(0.5 * x)


def _silu(x):
    h = 0.5 * x
    return h + h * jnp.tanh(h)


def _dot(a, b):
    return jnp.dot(a, b, preferred_element_type=F32)


def _dot_nt(a, b):
    return lax.dot_general(a, b, (((1,), (1,)), ((), ())), preferred_element_type=F32)


def _mod_kernel(c_ref, w_ref, b_ref, o_ref):
    c = c_ref[...]
    o_ref[...] = jnp.dot(_silu(c), w_ref[...], preferred_element_type=F32,
                         precision=lax.Precision.HIGHEST) + b_ref[...]


def _mod_vectors(c_rows, w_mod, b_mod):
    rows = c_rows.shape[0]
    tn = 1024
    return pl.pallas_call(
        _mod_kernel,
        grid=(6 * D_MODEL // tn,),
        in_specs=[pl.BlockSpec((rows, D_MODEL), lambda j: (0, 0)),
                  pl.BlockSpec((D_MODEL, tn), lambda j: (0, j)),
                  pl.BlockSpec((1, tn), lambda j: (0, j))],
        out_specs=pl.BlockSpec((rows, tn), lambda j: (0, j)),
        out_shape=jax.ShapeDtypeStruct((rows, 6 * D_MODEL), F32),
        compiler_params=_cparams(("arbitrary",)),
        name="mod",
    )(c_rows, w_mod, b_mod.reshape(1, -1))


def _modulated_norm(x, nw, shift, scale):
    ms = jnp.mean(x * x, axis=-1, keepdims=True)
    return (x * lax.rsqrt(ms + EPS) * nw) * (1.0 + scale) + shift


def _inproj_kernel(steps_a, x0_ref, xn_ref, mod0_ref, modn_ref, nw_ref, wa_ref, wb_ref, wdt_ref,
                   oa_ref, ob_ref, dt_ref, ha_scr, hb_scr):
    i = pl.program_id(0)
    j = pl.program_id(1)
    part = xn_ref.shape[0] // steps_a
    rows = pl.ds(pl.multiple_of(jnp.minimum(j, steps_a - 1) * part, part), part)

    def norm(x, mod_ref):
        return _modulated_norm(x, nw_ref[...], mod_ref[0, 0:1, :], mod_ref[0, 1:2, :]).astype(BF16)

    @pl.when((i == 0) & (j == 0))
    def _():
        ha_scr[...] = norm(x0_ref[...], mod0_ref)

    def step(cur_scr, nxt_scr):
        @pl.when(j < steps_a)
        def _():
            nxt_scr[rows, :] = norm(xn_ref[rows, :], modn_ref)
            dt_ref[rows, :] = _dot_nt(cur_scr[rows, :], wdt_ref[...])
            oa_ref[...] = _dot_nt(cur_scr[...], wa_ref[...]).astype(BF16)

        @pl.when(j >= steps_a)
        def _():
            ob_ref[...] = _dot_nt(cur_scr[...], wb_ref[...]).astype(BF16)

    @pl.when(i % 2 == 0)
    def _():
        step(ha_scr, hb_scr)

    @pl.when(i % 2 == 1)
    def _():
        step(hb_scr, ha_scr)


def _inproj(x, mod, nw, w_a, w_b, w_dt, tokens_per_mod):
    t = x.shape[0]
    tm = 1024
    tn_a, tn_b = SPLIT_XBC // 4, PROJ_B_COLS // 2
    steps_a, steps_b = SPLIT_XBC // tn_a, PROJ_B_COLS // tn_b
    n_tiles = t // tm
    nxt = lambda i: jnp.minimum(i + 1, n_tiles - 1)
    col_a = lambda j: jnp.minimum(j, steps_a - 1)
    col_b = lambda j: jnp.maximum(j - steps_a, 0)
    return pl.pallas_call(
        functools.partial(_inproj_kernel, steps_a),
        grid=(n_tiles, steps_a + steps_b),
        in_specs=[pl.BlockSpec((tm, D_MODEL), lambda i, j: (0, 0), pipeline_mode=pl.Buffered(1)),
                  pl.BlockSpec((tm, D_MODEL), lambda i, j: (nxt(i), 0)),
                  pl.BlockSpec((1, 6, D_MODEL), lambda i, j: (0, 0, 0)),
                  pl.BlockSpec((1, 6, D_MODEL), lambda i, j: ((nxt(i) * tm) // tokens_per_mod, 0, 0)),
                  pl.BlockSpec((1, D_MODEL), lambda i, j: (0, 0)),
                  pl.BlockSpec((tn_a, D_MODEL), lambda i, j: (col_a(j), 0)),
                  pl.BlockSpec((tn_b, D_MODEL), lambda i, j: (col_b(j), 0)),
                  pl.BlockSpec((LANES, D_MODEL), lambda i, j: (0, 0))],
        out_specs=[pl.BlockSpec((tm, tn_a), lambda i, j: (i, col_a(j))),
                   pl.BlockSpec((tm, tn_b), lambda i, j: (i, col_b(j))),
                   pl.BlockSpec((tm, LANES), lambda i, j: (i, 0))],
        out_shape=[jax.ShapeDtypeStruct((t, SPLIT_XBC), BF16),
                   jax.ShapeDtypeStruct((t, PROJ_B_COLS), BF16),
                   jax.ShapeDtypeStruct((t, LANES), F32)],
        scratch_shapes=[pltpu.VMEM((tm, D_MODEL), BF16), pltpu.VMEM((tm, D_MODEL), BF16)],
        compiler_params=_cparams(("arbitrary", "arbitrary")),
        name="inproj",
    )(x, x, mod, mod, nw, w_a, w_b, w_dt)


def _softplus(x):
    return jnp.maximum(x, 0.0) + jnp.log1p(jnp.exp(-jnp.abs(x)))


def _cumsum_rows(a):
    n = a.shape[0]
    rid = lax.broadcasted_iota(jnp.int32, a.shape, 0)
    s = 1
    while s < n:
        a = a + jnp.where(rid >= s, pltpu.roll(a, s, axis=0), 0.0)
        s *= 2
    return a


def _rev_cumsum_rows(a):
    n = a.shape[0]
    rid = lax.broadcasted_iota(jnp.int32, a.shape, 0)
    s = 1
    while s < n:
        a = a + jnp.where(rid < n - s, pltpu.roll(a, n - s, axis=0), 0.0)
        s *= 2
    return a


def _expand_heads(w, e_ref):
    hi = w.astype(BF16)
    lo = (w - hi.astype(F32)).astype(BF16)
    e = e_ref[...]
    return _dot(hi, e) + _dot(lo, e)


def _row_shift_matrix(n):
    r = jnp.arange(n)[:, None]
    c = jnp.arange(n)[None, :]
    return jnp.concatenate([c == r - 1, c == r + 1], axis=0).astype(BF16)


def _conv3_silu(main_ref, prev_ref, next_ref, w_ref, b_ref, shift_ref, first, last):
    x_bf = main_ref[...]
    n = x_bf.shape[0]
    x = x_bf.astype(F32)
    sh = _dot(shift_ref[...], x_bf)
    xm1 = sh[:n, :]
    xp1 = sh[n:, :]
    prow = jnp.where(first, 0.0, prev_ref[...].astype(F32)[-1:, :])
    nrow = jnp.where(last, 0.0, next_ref[...].astype(F32)[0:1, :])
    rid = lax.broadcasted_iota(jnp.int32, (8, x.shape[1]), 0)
    xm1 = jnp.concatenate([jnp.where(rid == 0, prow, xm1[:8, :]), xm1[8:, :]], axis=0)
    xp1 = jnp.concatenate([xp1[:-8, :], jnp.where(rid == 7, nrow, xp1[-8:, :])], axis=0)
    w = w_ref[...]
    y = w[0:1, :] * xm1 + w[1:2, :] * x + w[2:3, :] * xp1 + b_ref[...]
    return _silu(y)


def _dt_and_decay_rates(dt_ref, dtb_ref, alog_ref):
    dt = _softplus(dt_ref[...] + dtb_ref[...])
    a = dt * (-jnp.exp(alog_ref[...]))
    return dt, a


def _pair_rhs(xs_bf, pair, left):
    xp = xs_bf[:, pair * LANES:(pair + 1) * LANES]
    zero = jnp.zeros_like(xp)
    return jnp.concatenate([jnp.where(left, xp, zero), jnp.where(left, zero, xp)], axis=0)


def _ssd_fwd_kernel(has_h0, cps, *refs):
    (xm_ref, bcm_ref, xp_ref, bcp_ref, xn_ref, bcn_ref, dt_ref,
     cwx_ref, cwbc_ref, cbx_ref, cbbc_ref, dtb_ref, alog_ref, ef_ref, s3_ref) = refs[:15]
    pos = 15
    h0_ref = None
    if has_h0:
        h0_ref = refs[pos]
        pos += 1
    xs_ref, bc_ref, hprev_ref, hfin_ref, h_scr = refs[pos:pos + 5]

    c = pl.program_id(1)
    nsteps = pl.num_programs(1)
    first = c == 0
    last = c == nsteps - 1

    @pl.when(first)
    def _():
        if has_h0:
            h_scr[...] = h0_ref[0].T
        else:
            h_scr[...] = jnp.zeros_like(h_scr)

    xs_all = _conv3_silu(xm_ref, xp_ref, xn_ref, cwx_ref, cbx_ref, s3_ref, first, last)
    bc_all = _conv3_silu(bcm_ref, bcp_ref, bcn_ref, cwbc_ref, cbbc_ref, s3_ref, first, last)
    xs_all_bf = xs_all.astype(BF16)
    xs_ref[...] = xs_all_bf
    bc_ref[...] = bc_all.astype(BF16)

    left = lax.broadcasted_iota(jnp.int32, (CHUNK, LANES), 1) < HEADDIM
    h_cur = h_scr[...]
    for u in range(cps):
        rows = slice(u * CHUNK, (u + 1) * CHUNK)
        xs_bf = xs_all_bf[rows, :]
        bc = bc_all[rows, :]
        dt = _softplus(dt_ref[rows, :] + dtb_ref[...])
        a = dt * (-jnp.exp(alog_ref[...]))
        acs = _cumsum_rows(a)
        acs_t = acs.T
        w2_t = dt.T * jnp.exp(acs_t[:, CHUNK - 1:CHUNK] - acs_t)
        cdec = _expand_heads(jnp.broadcast_to(jnp.exp(acs[CHUNK - 1:CHUNK, :]), (8, LANES)), ef_ref)[0:1, :]

        parts = []
        for g in range(GROUPS):
            bt = bc[:, g * STATE:(g + 1) * STATE].T
            for k in range(HEADS_PER_GROUP // 2):
                h = g * HEADS_PER_GROUP + 2 * k
                lhs = jnp.concatenate([(bt * w2_t[h:h + 1, :]).astype(BF16),
                                       (bt * w2_t[h + 1:h + 2, :]).astype(BF16)], axis=1)
                parts.append(_dot(lhs, _pair_rhs(xs_bf, h // 2, left)))
        s_loc = jnp.concatenate(parts, axis=1)

        hprev_ref[u] = h_cur.astype(BF16)
        h_cur = cdec * h_cur + s_loc
    h_scr[...] = h_cur

    @pl.when(last)
    def _():
        hfin_ref[0] = h_cur.T


def _ssd_fwd(proj, dt_raw, h0, consts, nseq, nchunks):
    t = nseq * nchunks * CHUNK
    cps = min(SSD_FWD_CHUNKS_PER_STEP, nchunks)
    nsteps = nchunks // cps
    rows = cps * CHUNK
    halo = 16
    per = rows // halo
    n_halo = t // halo
    gidx = lambda b, c: b * nsteps + c
    prev_blk = lambda b, c: jnp.maximum(gidx(b, c) * per - 1, 0)
    next_blk = lambda b, c: jnp.minimum((gidx(b, c) + 1) * per, n_halo - 1)
    in_specs = [
        pl.BlockSpec((rows, D_SSD), lambda b, c: (gidx(b, c), 1)),
        pl.BlockSpec((rows, 2 * BC_W), lambda b, c: (gidx(b, c), 4)),
        pl.BlockSpec((halo, D_SSD), lambda b, c: (prev_blk(b, c), 1)),
        pl.BlockSpec((halo, 2 * BC_W), lambda b, c: (prev_blk(b, c), 4)),
        pl.BlockSpec((halo, D_SSD), lambda b, c: (next_blk(b, c), 1)),
        pl.BlockSpec((halo, 2 * BC_W), lambda b, c: (next_blk(b, c), 4)),
        pl.BlockSpec((rows, LANES), lambda b, c: (gidx(b, c), 0)),
        pl.BlockSpec((3, D_SSD), lambda b, c: (0, 0)),
        pl.BlockSpec((3, 2 * BC_W), lambda b, c: (0, 0)),
        pl.BlockSpec((1, D_SSD), lambda b, c: (0, 0)),
        pl.BlockSpec((1, 2 * BC_W), lambda b, c: (0, 0)),
        pl.BlockSpec((1, LANES), lambda b, c: (0, 0)),
        pl.BlockSpec((1, LANES), lambda b, c: (0, 0)),
        pl.BlockSpec((LANES, D_SSD), lambda b, c: (0, 0)),
        pl.BlockSpec((2 * rows, rows), lambda b, c: (0, 0)),
    ]
    args = [proj, proj, proj, proj, proj, proj, dt_raw,
            consts["cw_x"], consts["cw_bc"], consts["cb_x"], consts["cb_bc"],
            consts["dt_bias"], consts["a_log"], consts["e_fwd"], _row_shift_matrix(rows)]
    has_h0 = h0 is not None
    if has_h0:
        in_specs.append(pl.BlockSpec((1, D_SSD, STATE), lambda b, c: (b, 0, 0)))
        args.append(h0)
    return pl.pallas_call(
        functools.partial(_ssd_fwd_kernel, has_h0, cps),
        grid=(nseq, nsteps),
        in_specs=in_specs,
        out_specs=[pl.BlockSpec((rows, D_SSD), lambda b, c: (gidx(b, c), 0)),
                   pl.BlockSpec((rows, 2 * BC_W), lambda b, c: (gidx(b, c), 0)),
                   pl.BlockSpec((cps, STATE, D_SSD), lambda b, c: (gidx(b, c), 0, 0)),
                   pl.BlockSpec((1, D_SSD, STATE), lambda b, c: (b, 0, 0))],
        out_shape=[jax.ShapeDtypeStruct((t, D_SSD), BF16),
                   jax.ShapeDtypeStruct((t, 2 * BC_W), BF16),
                   jax.ShapeDtypeStruct((nseq * nchunks, STATE, D_SSD), BF16),
                   jax.ShapeDtypeStruct((nseq, D_SSD, STATE), F32)],
        scratch_shapes=[pltpu.VMEM((STATE, D_SSD), F32)],
        compiler_params=_cparams(("arbitrary", "arbitrary")),
        name="ssd_fwd",
    )(*args)


def _ssd_bwd_kernel(has_h0, cps, *refs):
    (xs_ref, bc_ref, z_ref, dt_ref, hprev_ref, dtb_ref, alog_ref, dvec_ref, nw_ref,
     eb_ref) = refs[:10]
    pos = 10
    h0_ref = None
    if has_h0:
        h0_ref = refs[pos]
        pos += 1
    y_ref, hfin_ref, h_scr = refs[pos:pos + 3]

    c = pl.program_id(1)
    nsteps = pl.num_programs(1)

    @pl.when(c == 0)
    def _():
        if has_h0:
            h_scr[...] = h0_ref[0].T
        else:
            h_scr[...] = jnp.zeros_like(h_scr)

    ri = lax.broadcasted_iota(jnp.int32, (CHUNK, CHUNK), 0)
    ci = lax.broadcasted_iota(jnp.int32, (CHUNK, CHUNK), 1)
    lower = ri >= ci
    diag = ri == ci
    left = lax.broadcasted_iota(jnp.int32, (CHUNK, LANES), 1) < HEADDIM

    h_b = h_scr[...]
    for u in reversed(range(cps)):
        rows = slice(u * CHUNK, (u + 1) * CHUNK)
        xs_bf = xs_ref[rows, :]
        xs = xs_bf.astype(F32)
        bc_bf = bc_ref[rows, :]

        dt = _softplus(dt_ref[rows, :] + dtb_ref[...])
        a = dt * (-jnp.exp(alog_ref[...]))
        acs = _cumsum_rows(a) * LOG2E
        rcs = _rev_cumsum_rows(a) * LOG2E
        acs_t = acs.T
        rcs_t = rcs.T
        dt_t = dt.T
        lg_t = jnp.log2(dt_t)
        rf_t = acs_t - lg_t
        rb_t = rcs_t - lg_t
        w2b_t = dt_t * jnp.exp2(rcs_t[:, 0:1] - rcs_t)
        cdec = _expand_heads(jnp.broadcast_to(jnp.exp2(rcs[0:1, :]), (8, LANES)), eb_ref)[0:1, :]

        h_f = hprev_ref[u]
        h_b_bf = h_b.astype(BF16)

        y_parts = []
        s_parts = []
        for g in range(GROUPS):
            b_g = bc_bf[:, g * STATE:(g + 1) * STATE]
            c_g = bc_bf[:, BC_W + g * STATE:BC_W + (g + 1) * STATE]
            cb = _dot_nt(c_g, b_g)
            bt = b_g.astype(F32).T
            off_f = _dot(c_g, h_f[:, g * GROUP_W:(g + 1) * GROUP_W])
            off_b = _dot(c_g, h_b_bf[:, g * GROUP_W:(g + 1) * GROUP_W])
            for k in range(HEADS_PER_GROUP // 2):
                ms, bs, cfs, cbs = [], [], [], []
                for h in (g * HEADS_PER_GROUP + 2 * k, g * HEADS_PER_GROUP + 2 * k + 1):
                    hb = HEADS + h
                    cf = acs[:, h:h + 1]
                    cbk = rcs[:, hb:hb + 1]
                    e = jnp.exp2(jnp.where(lower, cf - rf_t[h:h + 1, :], cbk - rb_t[hb:hb + 1, :]))
                    e = e + jnp.where(diag, dt_t[hb:hb + 1, :], 0.0)
                    ms.append((cb * e).astype(BF16))
                    bs.append((bt * w2b_t[hb:hb + 1, :]).astype(BF16))
                    cfs.append(cf)
                    cbs.append(cbk)
                lhs = jnp.concatenate([jnp.concatenate(ms, axis=1), jnp.concatenate(bs, axis=1)], axis=0)
                out = _dot(lhs, _pair_rhs(xs_bf, g * HEADS_PER_GROUP // 2 + k, left))
                ef = jnp.exp2(jnp.where(left, cfs[0], cfs[1]))
                eb = jnp.exp2(jnp.where(left, cbs[0], cbs[1]))
                cols = slice(k * LANES, (k + 1) * LANES)
                y_parts.append(out[:CHUNK, :] + ef * off_f[:, cols] + eb * off_b[:, cols])
                s_parts.append(out[CHUNK:, :])
        y = jnp.concatenate(y_parts, axis=1) + dvec_ref[...] * xs
        y = y * _silu(z_ref[rows, :].astype(F32))
        ms_y = jnp.mean(y * y, axis=-1, keepdims=True)
        y_ref[rows, :] = (y * lax.rsqrt(ms_y + EPS) * nw_ref[...]).astype(BF16)

        h_b = cdec * h_b + jnp.concatenate(s_parts, axis=1)
    h_scr[...] = h_b

    @pl.when(c == nsteps - 1)
    def _():
        hfin_ref[0] = h_b.T


def _ssd_bwd(xs, bc, proj, dt_raw, hprev, h0, consts, nseq, nchunks):
    t = nseq * nchunks * CHUNK
    cps = min(SSD_BWD_CHUNKS_PER_STEP, nchunks)
    nsteps = nchunks // cps
    rows = cps * CHUNK
    gidx = lambda b, c: b * nsteps + (nsteps - 1 - c)
    in_specs = [
        pl.BlockSpec((rows, D_SSD), lambda b, c: (gidx(b, c), 0)),
        pl.BlockSpec((rows, 2 * BC_W), lambda b, c: (gidx(b, c), 0)),
        pl.BlockSpec((rows, D_SSD), lambda b, c: (gidx(b, c), 0)),
        pl.BlockSpec((rows, LANES), lambda b, c: (gidx(b, c), 0)),
        pl.BlockSpec((cps, STATE, D_SSD), lambda b, c: (gidx(b, c), 0, 0)),
        pl.BlockSpec((1, LANES), lambda b, c: (0, 0)),
        pl.BlockSpec((1, LANES), lambda b, c: (0, 0)),
        pl.BlockSpec((1, D_SSD), lambda b, c: (0, 0)),
        pl.BlockSpec((1, D_SSD), lambda b, c: (0, 0)),
        pl.BlockSpec((LANES, D_SSD), lambda b, c: (0, 0)),
    ]
    args = [xs, bc, proj, dt_raw, hprev, consts["dt_bias"], consts["a_log"], consts["d_vec"],
            consts["ssd_norm"], consts["e_bwd"]]
    has_h0 = h0 is not None
    if has_h0:
        in_specs.append(pl.BlockSpec((1, D_SSD, STATE), lambda b, c: (b, 0, 0)))
        args.append(h0)
    return pl.pallas_call(
        functools.partial(_ssd_bwd_kernel, has_h0, cps),
        grid=(nseq, nsteps),
        in_specs=in_specs,
        out_specs=[pl.BlockSpec((rows, D_SSD), lambda b, c: (gidx(b, c), 0)),
                   pl.BlockSpec((1, D_SSD, STATE), lambda b, c: (b, 0, 0))],
        out_shape=[jax.ShapeDtypeStruct((t, D_SSD), BF16),
                   jax.ShapeDtypeStruct((nseq, D_SSD, STATE), F32)],
        scratch_shapes=[pltpu.VMEM((STATE, D_SSD), F32)],
        compiler_params=_cparams(("arbitrary", "arbitrary")),
        name="ssd_bwd",
    )(*args)


def _mix_kernel(y_ref, u_ref, v_ref, ga_ref, gb_ref, x_ref, mod_ref, lnw_ref, lnb_ref, ws_ref, bst_ref,
                wbs_ref, wbg_ref, wout_ref, npost_ref, npre2_ref, o_ref, h2_ref, ysgu_scr):
    tm = x_ref.shape[0]
    v = v_ref[...].astype(F32)
    mu = jnp.mean(v, axis=-1, keepdims=True)
    vc = v - mu
    var = jnp.mean(vc * vc, axis=-1, keepdims=True)
    vn = (vc * lax.rsqrt(var + EPS) * lnw_ref[...] + lnb_ref[...]).astype(BF16)
    bst = bst_ref[...]
    for r in range(tm // CHUNK):
        rows = slice(r * CHUNK, (r + 1) * CHUNK)
        for g in range(SGU_GROUPS):
            cols = slice(g * LANES, (g + 1) * LANES)
            s = _dot(ws_ref[g], vn[rows, cols]) + bst[:, g:g + 1]
            ysgu_scr[rows, cols] = (u_ref[rows, cols].astype(F32) * s).astype(BF16)
    br_ssd = _dot(y_ref[...], wbs_ref[...])
    br_sgu = _dot(ysgu_scr[...], wbg_ref[...])
    merged = _sigmoid(ga_ref[...].astype(F32)) * br_ssd + _sigmoid(gb_ref[...].astype(F32)) * br_sgu
    merged = merged.astype(BF16)
    quarter = tm // 4
    for r in range(4):
        rows = slice(r * quarter, (r + 1) * quarter)
        mix = _dot(merged[rows, :], wout_ref[...])
        ms = jnp.mean(mix * mix, axis=-1, keepdims=True)
        x1 = x_ref[rows, :] + mod_ref[0, 2:3, :] * (mix * lax.rsqrt(ms + EPS) * npost_ref[...])
        o_ref[rows, :] = x1
        h2_ref[rows, :] = _modulated_norm(x1, npre2_ref[...], mod_ref[0, 3:4, :], mod_ref[0, 4:5, :]).astype(BF16)


def _mix(y_ssd, proj, x, mod, consts, tokens_per_mod):
    t = x.shape[0]
    tm = 512
    row = lambda i: (i, 0)
    const2 = lambda i: (0, 0)
    return pl.pallas_call(
        _mix_kernel,
        grid=(t // tm,),
        in_specs=[pl.BlockSpec((tm, D_SSD), row),
                  pl.BlockSpec((tm, D_MODEL), lambda i: (i, 0)),
                  pl.BlockSpec((tm, D_MODEL), lambda i: (i, 1)),
                  pl.BlockSpec((tm, D_MODEL), lambda i: (i, 2)),
                  pl.BlockSpec((tm, D_MODEL), lambda i: (i, 3)),
                  pl.BlockSpec((tm, D_MODEL), row),
                  pl.BlockSpec((1, 6, D_MODEL), lambda i: ((i * tm) // tokens_per_mod, 0, 0)),
                  pl.BlockSpec((1, D_MODEL), const2),
                  pl.BlockSpec((1, D_MODEL), const2),
                  pl.BlockSpec((SGU_GROUPS, CHUNK, CHUNK), lambda i: (0, 0, 0)),
                  pl.BlockSpec((CHUNK, SGU_GROUPS), const2),
                  pl.BlockSpec((D_SSD, D_MODEL), const2),
                  pl.BlockSpec((D_MODEL, D_MODEL), const2),
                  pl.BlockSpec((D_MODEL, D_MODEL), const2),
                  pl.BlockSpec((1, D_MODEL), const2),
                  pl.BlockSpec((1, D_MODEL), const2)],
        out_specs=[pl.BlockSpec((tm, D_MODEL), row), pl.BlockSpec((tm, D_MODEL), row)],
        out_shape=[jax.ShapeDtypeStruct((t, D_MODEL), F32), jax.ShapeDtypeStruct((t, D_MODEL), BF16)],
        scratch_shapes=[pltpu.VMEM((tm, D_MODEL), BF16)],
        compiler_params=_cparams(("arbitrary",)),
        name="mix",
    )(y_ssd, proj, proj, proj, proj, x, mod, consts["sgu_norm_w"], consts["sgu_norm_b"], consts["sgu_w"],
      consts["sgu_bt"], consts["w_branch_ssd"], consts["w_branch_sgu"], consts["w_out"], consts["norm_mix_post"],
      consts["norm_ffn_pre"])


def _matmul_kernel(h_ref, w_ref, o_ref):
    o_ref[...] = _dot(h_ref[...], w_ref[...]).astype(BF16)


def _ffn_up(h2, w_up):
    t = h2.shape[0]
    tm, tn = 1024, D_FF
    return pl.pallas_call(
        _matmul_kernel,
        grid=(t // tm, 2 * D_FF // tn),
        in_specs=[pl.BlockSpec((tm, D_MODEL), lambda i, j: (i, 0)),
                  pl.BlockSpec((D_MODEL, tn), lambda i, j: (0, j))],
        out_specs=pl.BlockSpec((tm, tn), lambda i, j: (i, j)),
        out_shape=jax.ShapeDtypeStruct((t, 2 * D_FF), BF16),
        compiler_params=_cparams(("arbitrary", "arbitrary")),
        name="ffn_up",
    )(h2, w_up)


def _gelu_tanh(x):
    h = 0.5 * x
    u = x * (GELU_C + (GELU_C * 0.044715) * (x * x))
    return h + h * jnp.tanh(u)


def _grid_conv(x_bf, prev_bf, next_bf, w, b, first, last):
    x = x_bf.astype(F32)
    tm = x.shape[0]
    prev = jnp.where(first, 0.0, prev_bf.astype(F32))
    nxt = jnp.where(last, 0.0, next_bf.astype(F32))
    ext = jnp.concatenate([prev, x, nxt], axis=0)
    n = ext.shape[0]
    col = lax.broadcasted_iota(jnp.int32, ext.shape, 0) % GRID_W
    shifted = (jnp.where(col == 0, 0.0, pltpu.roll(ext, 1, axis=0)),
               ext,
               jnp.where(col == GRID_W - 1, 0.0, pltpu.roll(ext, n - 1, axis=0)))
    acc = None
    for dy in range(3):
        for dx in range(3):
            term = w[3 * dy + dx:3 * dy + dx + 1, :] * shifted[dx][dy * GRID_W:dy * GRID_W + tm, :]
            acc = term if acc is None else acc + term
    return acc + b


def _seq_conv(x_bf, w, b, seq_len):
    x = x_bf.astype(F32)
    n = x.shape[0]
    pos = lax.broadcasted_iota(jnp.int32, x.shape, 0) % seq_len
    xm1 = jnp.where(pos == 0, 0.0, pltpu.roll(x, 1, axis=0))
    xp1 = jnp.where(pos == seq_len - 1, 0.0, pltpu.roll(x, n - 1, axis=0))
    return w[3:4, :] * xm1 + w[4:5, :] * x + w[5:6, :] * xp1 + b


def _ffn_down_kernel(on_grid, seq_len, tiles_per_seq, *refs):
    if on_grid:
        (up_ref, upp_ref, upn_ref, w_ref, b_ref, wd_ref, x_ref, mod_ref, npost_ref, o_ref, g0_scr, g1_scr, acc_scr) = refs
    else:
        (up_ref, w_ref, b_ref, wd_ref, x_ref, mod_ref, npost_ref, o_ref, g0_scr, g1_scr, acc_scr) = refs
    i = pl.program_id(0)
    first = i % tiles_per_seq == 0
    last = i % tiles_per_seq == tiles_per_seq - 1

    def conv(c0):
        cols = pl.ds(c0, FF_BLK)
        if on_grid:
            return _grid_conv(up_ref[:, cols], upp_ref[:, cols], upn_ref[:, cols], w_ref[:, cols], b_ref[:, cols],
                              first, last)
        return _seq_conv(up_ref[:, cols], w_ref[:, cols], b_ref[:, cols], seq_len)

    def geglu_block(k, dst_ref):
        c0 = pl.multiple_of(k * FF_BLK, FF_BLK)
        a = conv(c0)
        val = conv(pl.multiple_of(c0 + D_FF, FF_BLK))
        dst_ref[...] = (_gelu_tanh(a) * val).astype(BF16)

    def down_block(k, src_ref):
        rows = pl.ds(pl.multiple_of(k * FF_BLK, FF_BLK), FF_BLK)
        acc_scr[...] += _dot(src_ref[...], wd_ref[rows, :])

    acc_scr[...] = jnp.zeros_like(acc_scr)
    geglu_block(0, g0_scr)

    def block_pair(m, carry):
        k = 2 * m + 1
        down_block(k - 1, g0_scr)
        geglu_block(k, g1_scr)
        down_block(k, g1_scr)
        geglu_block(k + 1, g0_scr)
        return carry

    lax.fori_loop(0, (N_FF_BLK - 1) // 2, block_pair, 0)
    down_block(N_FF_BLK - 1, g0_scr)
    f = acc_scr[...]
    ms = jnp.mean(f * f, axis=-1, keepdims=True)
    o_ref[...] = x_ref[...] + mod_ref[0, 5:6, :] * (f * lax.rsqrt(ms + EPS) * npost_ref[...])


def _ffn_down(up, x, mod, consts, tokens_per_mod, on_grid, seq_len):
    t = x.shape[0]
    tm = 512
    tiles_per_seq = max(seq_len // tm, 1)
    rows_per_tile = tm // GRID_W
    n_rows = t // GRID_W
    in_specs = [pl.BlockSpec((tm, 2 * D_FF), lambda i: (i, 0))]
    args = [up]
    if on_grid:
        in_specs += [pl.BlockSpec((GRID_W, 2 * D_FF), lambda i: (jnp.maximum(i * rows_per_tile - 1, 0), 0)),
                     pl.BlockSpec((GRID_W, 2 * D_FF), lambda i: (jnp.minimum((i + 1) * rows_per_tile, n_rows - 1), 0))]
        args += [up, up]
    in_specs += [pl.BlockSpec((9, 2 * D_FF), lambda i: (0, 0)),
                 pl.BlockSpec((1, 2 * D_FF), lambda i: (0, 0)),
                 pl.BlockSpec((D_FF, D_MODEL), lambda i: (0, 0)),
                 pl.BlockSpec((tm, D_MODEL), lambda i: (i, 0)),
                 pl.BlockSpec((1, 6, D_MODEL), lambda i: ((i * tm) // tokens_per_mod, 0, 0)),
                 pl.BlockSpec((1, D_MODEL), lambda i: (0, 0))]
    args += [consts["ffn_conv_w"], consts["ffn_conv_b"], consts["ffn_w_down"], x, mod, consts["norm_ffn_post"]]
    return pl.pallas_call(
        functools.partial(_ffn_down_kernel, on_grid, seq_len, tiles_per_seq),
        grid=(t // tm,),
        in_specs=in_specs,
        out_specs=pl.BlockSpec((tm, D_MODEL), lambda i: (i, 0)),
        out_shape=jax.ShapeDtypeStruct((t, D_MODEL), F32),
        scratch_shapes=[pltpu.VMEM((tm, FF_BLK), BF16), pltpu.VMEM((tm, FF_BLK), BF16),
                        pltpu.VMEM((tm, D_MODEL), F32)],
        compiler_params=_cparams(("arbitrary",)),
        name="ffn_down",
    )(*args)


def _trunk_path(x, mod, h0_f, h0_b, on_grid, consts):
    nseq, seq_len, _ = x.shape
    t = nseq * seq_len
    nchunks = seq_len // CHUNK
    tokens_per_mod = t // mod.shape[0]
    x2d = x.reshape(t, D_MODEL)

    proj_a, proj_b, dt_raw = _inproj(x2d, mod, consts["norm_mix_pre"], consts["w_in_a"], consts["w_in_b"],
                                     consts["w_in_dt"], tokens_per_mod)
    xs, bc, hprev, hf = _ssd_fwd(proj_a, dt_raw, h0_f, consts, nseq, nchunks)
    y_ssd, hb = _ssd_bwd(xs, bc, proj_a, dt_raw, hprev, h0_b, consts, nseq, nchunks)
    x1, h2 = _mix(y_ssd, proj_b, x2d, mod, consts, tokens_per_mod)
    up = _ffn_up(h2, consts["ffn_w_up"])
    x2 = _ffn_down(up, x1, mod, consts, tokens_per_mod, on_grid, seq_len)
    return x2.reshape(nseq, seq_len, D_MODEL), hf, hb


def _head_expansion(offset):
    rows = jnp.arange(LANES)[:, None]
    cols = jnp.arange(D_SSD)[None, :] // HEADDIM
    return (rows == cols + offset).astype(BF16)


def _layer_consts(i, p):
    w_t = jnp.swapaxes(p["w_in"][i], 0, 1)
    row = lambda v: v.reshape(1, -1).astype(F32)
    pad_lanes = lambda v: jnp.pad(v.reshape(1, -1).astype(F32), ((0, 0), (0, LANES - N_DT)))
    conv_w = p["ssd_conv_w"][i]
    conv_b = p["ssd_conv_b"][i]
    return {
        "norm_mix_pre": row(p["norm_mix_pre"][i]),
        "norm_mix_post": row(p["norm_mix_post"][i]),
        "norm_ffn_pre": row(p["norm_ffn_pre"][i]),
        "norm_ffn_post": row(p["norm_ffn_post"][i]),
        "w_in_a": w_t[:SPLIT_XBC].astype(BF16),
        "w_in_b": w_t[SPLIT_XBC + N_DT:].astype(BF16),
        "w_in_dt": jnp.pad(w_t[SPLIT_XBC:SPLIT_XBC + N_DT], ((0, LANES - N_DT), (0, 0))).astype(BF16),
        "cw_x": conv_w[:, :D_SSD], "cw_bc": conv_w[:, D_SSD:],
        "cb_x": row(conv_b[:D_SSD]), "cb_bc": row(conv_b[D_SSD:]),
        "dt_bias": pad_lanes(p["ssd_dt_bias"][i]),
        "a_log": pad_lanes(p["ssd_a_log"][i]),
        "d_vec": row(jnp.repeat(p["ssd_d"][i], HEADDIM)),
        "ssd_norm": row(p["ssd_norm"][i]),
        "e_fwd": _head_expansion(0), "e_bwd": _head_expansion(HEADS),
        "sgu_norm_w": row(p["sgu_norm_w"][i]), "sgu_norm_b": row(p["sgu_norm_b"][i]),
        "sgu_w": p["sgu_w"][i].astype(BF16),
        "sgu_bt": jnp.transpose(p["sgu_b"][i]).astype(F32),
        "w_branch_ssd": p["w_branch_ssd"][i].astype(BF16),
        "w_branch_sgu": p["w_branch_sgu"][i].astype(BF16),
        "w_out": p["w_out"][i].astype(BF16),
        "ffn_w_up": p["ffn_w_up"][i].astype(BF16),
        "ffn_conv_w": p["ffn_conv_w"][i].reshape(9, 2 * D_FF).astype(F32),
        "ffn_conv_b": row(p["ffn_conv_b"][i]),
        "ffn_w_down": p["ffn_w_down"][i].astype(BF16),
    }


def kernel(x_prompt, x_sample, state_ssd_fwd, state_ssd_bwd, c, c_ctx, w_mod, b_mod, norm_mix_pre, norm_mix_post, norm_ffn_pre, norm_ffn_post, w_in, ssd_conv_w, ssd_conv_b, ssd_a_log, ssd_dt_bias, ssd_d, ssd_norm, sgu_norm_w, sgu_norm_b, sgu_w, sgu_b, w_branch_ssd, w_branch_sgu, w_out, ffn_w_up, ffn_conv_w, ffn_conv_b, ffn_w_down):
    params = dict(norm_mix_pre=norm_mix_pre, norm_mix_post=norm_mix_post, norm_ffn_pre=norm_ffn_pre,
                  norm_ffn_post=norm_ffn_post, w_in=w_in, ssd_conv_w=ssd_conv_w, ssd_conv_b=ssd_conv_b,
                  ssd_a_log=ssd_a_log, ssd_dt_bias=ssd_dt_bias, ssd_d=ssd_d, ssd_norm=ssd_norm,
                  sgu_norm_w=sgu_norm_w, sgu_norm_b=sgu_norm_b, sgu_w=sgu_w, sgu_b=sgu_b,
                  w_branch_ssd=w_branch_ssd, w_branch_sgu=w_branch_sgu, w_out=w_out, ffn_w_up=ffn_w_up,
                  ffn_conv_w=ffn_conv_w, ffn_conv_b=ffn_conv_b, ffn_w_down=ffn_w_down)
    depth = w_mod.shape[0]
    n_lat = c.shape[0]
    c_rows = jnp.concatenate([c_ctx[None, :], c, jnp.zeros((8 - 1 - n_lat, D_MODEL), F32)], axis=0)
    xp, xs = x_prompt, x_sample
    new_f, new_b = [], []
    for i in range(depth):
        consts = _layer_consts(i, params)
        mod = _mod_vectors(c_rows, w_mod[i], b_mod[i]).reshape(8, 6, D_MODEL)
        xp, hf, hb = _trunk_path(xp, mod[0:1], None, None, False, consts)
        new_f.append(hf.reshape(-1, HEADS, HEADDIM, STATE))
        new_b.append(hb.reshape(-1, HEADS, HEADDIM, STATE))
        xs, _, _ = _trunk_path(xs, mod[1:1 + n_lat],
                               state_ssd_fwd[:, i].reshape(n_lat, D_SSD, STATE),
                               state_ssd_bwd[:, i].reshape(n_lat, D_SSD, STATE), True, consts)
    return (xp, xs, jnp.stack(new_f, axis=1).astype(x_prompt.dtype), jnp.stack(new_b, axis=1).astype(x_prompt.dtype))
```

```python
import functools

import jax
import jax.numpy as jnp
from jax import lax
from jax.experimental import pallas as pl
from jax.experimental.pallas import tpu as pltpu

F32 = jnp.float32
BF16 = jnp.bfloat16

D_MODEL = 1024
GRID_W = 64
EPS = 1e-6
LOG2E = 1.4426950408889634
GELU_C = 0.7978845608028654
D_SSD = 2 * D_MODEL
HEADDIM = 64
HEADS = D_SSD // HEADDIM
STATE = 128
GROUPS = 4
HEADS_PER_GROUP = HEADS // GROUPS
GROUP_W = HEADS_PER_GROUP * HEADDIM
BC_W = GROUPS * STATE
CONV_CH = D_SSD + 2 * BC_W
CHUNK = 128
SSD_FWD_CHUNKS_PER_STEP = 2
SSD_BWD_CHUNKS_PER_STEP = 4
SGU_GROUPS = 8
D_FF = 2816
SPLIT_XBC = D_SSD + CONV_CH
N_DT = 2 * HEADS
LANES = 128
PROJ_COLS = 9216
FF_BLK = 256
N_FF_BLK = D_FF // FF_BLK

VMEM_LIMIT_BYTES = 56 * 1024 * 1024


def _cparams(sem):
    return pltpu.CompilerParams(dimension_semantics=sem, vmem_limit_bytes=VMEM_LIMIT_BYTES)


def _sigmoid(x):
    return 0.5 + 0.5 * jnp.tanh(0.5 * x)


def _silu(x):
    h = 0.5 * x
    return h + h * jnp.tanh(h)


def _dot(a, b):
    return jnp.dot(a, b, preferred_element_type=F32)


def _dot_nt(a, b):
    return lax.dot_general(a, b, (((1,), (1,)), ((), ())), preferred_element_type=F32)


def _mod_kernel(c_ref, w_ref, b_ref, o_ref):
    c = c_ref[...]
    o_ref[...] = jnp.dot(_silu(c), w_ref[...], preferred_element_type=F32,
                         precision=lax.Precision.HIGHEST) + b_ref[...]


def _mod_vectors(c_rows, w_mod, b_mod):
    rows = c_rows.shape[0]
    tn = 1024
    return pl.pallas_call(
        _mod_kernel,
        grid=(6 * D_MODEL // tn,),
        in_specs=[pl.BlockSpec((rows, D_MODEL), lambda j: (0, 0)),
                  pl.BlockSpec((D_MODEL, tn), lambda j: (0, j)),
                  pl.BlockSpec((1, tn), lambda j: (0, j))],
        out_specs=pl.BlockSpec((rows, tn), lambda j: (0, j)),
        out_shape=jax.ShapeDtypeStruct((rows, 6 * D_MODEL), F32),
        compiler_params=_cparams(("arbitrary",)),
        name="mod",
    )(c_rows, w_mod, b_mod.reshape(1, -1))


def _w_in_prep_kernel(w_ref, main_ref, dt_ref):
    w = w_ref[0]
    pad = jnp.zeros((w.shape[0], LANES - N_DT), F32)
    main_ref[...] = jnp.concatenate([w[:, :SPLIT_XBC], w[:, SPLIT_XBC + N_DT:]], axis=1).astype(BF16)
    dt_ref[...] = jnp.concatenate([w[:, SPLIT_XBC:SPLIT_XBC + N_DT], pad], axis=1).astype(BF16)


def _w_in_prep(w_in, layer):
    rows = 128
    return pl.pallas_call(
        _w_in_prep_kernel,
        grid=(D_MODEL // rows,),
        in_specs=[pl.BlockSpec((1, rows, PROJ_COLS + N_DT), lambda r: (layer, r, 0))],
        out_specs=[pl.BlockSpec((rows, PROJ_COLS), lambda r: (r, 0)),
                   pl.BlockSpec((rows, LANES), lambda r: (r, 0))],
        out_shape=[jax.ShapeDtypeStruct((D_MODEL, PROJ_COLS), BF16),
                   jax.ShapeDtypeStruct((D_MODEL, LANES), BF16)],
        compiler_params=_cparams(("arbitrary",)),
        name="w_in_prep",
    )(w_in)


def _modulated_norm(x, nw, shift, scale):
    ms = jnp.mean(x * x, axis=-1, keepdims=True)
    return (x * lax.rsqrt(ms + EPS) * nw) * (1.0 + scale) + shift


def _inproj_kernel(n_col_steps, x0_ref, xn_ref, mod0_ref, modn_ref, nw_ref, w_ref, wdt_ref, o_ref, dt_ref, ha_scr, hb_scr):
    i = pl.program_id(0)
    j = pl.program_id(1)
    part = xn_ref.shape[0] // n_col_steps
    rows = pl.ds(pl.multiple_of(j * part, part), part)

    def norm(x, mod_ref):
        return _modulated_norm(x, nw_ref[...], mod_ref[0, 0:1, :], mod_ref[0, 1:2, :]).astype(BF16)

    @pl.when((i == 0) & (j == 0))
    def _():
        ha_scr[...] = norm(x0_ref[...], mod0_ref)

    def step(cur_scr, nxt_scr):
        nxt_scr[rows, :] = norm(xn_ref[rows, :], modn_ref)
        dt_ref[rows, :] = _dot(cur_scr[rows, :], wdt_ref[...])
        o_ref[...] = _dot(cur_scr[...], w_ref[...]).astype(BF16)

    @pl.when(i % 2 == 0)
    def _():
        step(ha_scr, hb_scr)

    @pl.when(i % 2 == 1)
    def _():
        step(hb_scr, ha_scr)


def _inproj(x, mod, nw, w_main, w_dt, tokens_per_mod):
    t = x.shape[0]
    tm, tn = 1024, PROJ_COLS // 4
    n_tiles = t // tm
    nxt = lambda i: jnp.minimum(i + 1, n_tiles - 1)
    return pl.pallas_call(
        functools.partial(_inproj_kernel, PROJ_COLS // tn),
        grid=(n_tiles, PROJ_COLS // tn),
        in_specs=[pl.BlockSpec((tm, D_MODEL), lambda i, j: (0, 0)),
                  pl.BlockSpec((tm, D_MODEL), lambda i, j: (nxt(i), 0)),
                  pl.BlockSpec((1, 6, D_MODEL), lambda i, j: (0, 0, 0)),
                  pl.BlockSpec((1, 6, D_MODEL), lambda i, j: ((nxt(i) * tm) // tokens_per_mod, 0, 0)),
                  pl.BlockSpec((1, D_MODEL), lambda i, j: (0, 0)),
                  pl.BlockSpec((D_MODEL, tn), lambda i, j: (0, j)),
                  pl.BlockSpec((D_MODEL, LANES), lambda i, j: (0, 0))],
        out_specs=[pl.BlockSpec((tm, tn), lambda i, j: (i, j)),
                   pl.BlockSpec((tm, LANES), lambda i, j: (i, 0))],
        out_shape=[jax.ShapeDtypeStruct((t, PROJ_COLS), BF16),
                   jax.ShapeDtypeStruct((t, LANES), F32)],
        scratch_shapes=[pltpu.VMEM((tm, D_MODEL), BF16), pltpu.VMEM((tm, D_MODEL), BF16)],
        compiler_params=_cparams(("arbitrary", "arbitrary")),
        name="inproj",
    )(x, x, mod, mod, nw, w_main, w_dt)


def _softplus(x):
    return jnp.maximum(x, 0.0) + jnp.log1p(jnp.exp(-jnp.abs(x)))


def _cumsum_rows(a):
    n = a.shape[0]
    rid = lax.broadcasted_iota(jnp.int32, a.shape, 0)
    s = 1
    while s < n:
        a = a + jnp.where(rid >= s, pltpu.roll(a, s, axis=0), 0.0)
        s *= 2
    return a


def _rev_cumsum_rows(a):
    n = a.shape[0]
    rid = lax.broadcasted_iota(jnp.int32, a.shape, 0)
    s = 1
    while s < n:
        a = a + jnp.where(rid < n - s, pltpu.roll(a, n - s, axis=0), 0.0)
        s *= 2
    return a


def _expand_heads(w, e_ref):
    hi = w.astype(BF16)
    lo = (w - hi.astype(F32)).astype(BF16)
    e = e_ref[...]
    return _dot(hi, e) + _dot(lo, e)


def _row_shift_matrix(n):
    r = jnp.arange(n)[:, None]
    c = jnp.arange(n)[None, :]
    return jnp.concatenate([c == r - 1, c == r + 1], axis=0).astype(BF16)


def _conv3_silu(main_ref, prev_ref, next_ref, w_ref, b_ref, shift_ref, first, last):
    x_bf = main_ref[...]
    n = x_bf.shape[0]
    x = x_bf.astype(F32)
    sh = _dot(shift_ref[...], x_bf)
    xm1 = sh[:n, :]
    xp1 = sh[n:, :]
    prow = jnp.where(first, 0.0, prev_ref[...].astype(F32)[-1:, :])
    nrow = jnp.where(last, 0.0, next_ref[...].astype(F32)[0:1, :])
    rid = lax.broadcasted_iota(jnp.int32, (8, x.shape[1]), 0)
    xm1 = jnp.concatenate([jnp.where(rid == 0, prow, xm1[:8, :]), xm1[8:, :]], axis=0)
    xp1 = jnp.concatenate([xp1[:-8, :], jnp.where(rid == 7, nrow, xp1[-8:, :])], axis=0)
    w = w_ref[...]
    y = w[0:1, :] * xm1 + w[1:2, :] * x + w[2:3, :] * xp1 + b_ref[...]
    return _silu(y)


def _dt_and_decay_rates(dt_ref, dtb_ref, alog_ref):
    dt = _softplus(dt_ref[...] + dtb_ref[...])
    a = dt * (-jnp.exp(alog_ref[...]))
    return dt, a


def _pair_rhs(xs_bf, pair, left):
    xp = xs_bf[:, pair * LANES:(pair + 1) * LANES]
    zero = jnp.zeros_like(xp)
    return jnp.concatenate([jnp.where(left, xp, zero), jnp.where(left, zero, xp)], axis=0)


def _ssd_fwd_kernel(has_h0, cps, *refs):
    (xm_ref, bcm_ref, xp_ref, bcp_ref, xn_ref, bcn_ref, dt_ref,
     cwx_ref, cwbc_ref, cbx_ref, cbbc_ref, dtb_ref, alog_ref, ef_ref, s3_ref) = refs[:15]
    pos = 15
    h0_ref = None
    if has_h0:
        h0_ref = refs[pos]
        pos += 1
    xs_ref, bc_ref, hprev_ref, hfin_ref, h_scr = refs[pos:pos + 5]

    c = pl.program_id(1)
    nsteps = pl.num_programs(1)
    first = c == 0
    last = c == nsteps - 1

    @pl.when(first)
    def _():
        if has_h0:
            h_scr[...] = h0_ref[0].T
        else:
            h_scr[...] = jnp.zeros_like(h_scr)

    xs_all = _conv3_silu(xm_ref, xp_ref, xn_ref, cwx_ref, cbx_ref, s3_ref, first, last)
    bc_all = _conv3_silu(bcm_ref, bcp_ref, bcn_ref, cwbc_ref, cbbc_ref, s3_ref, first, last)
    xs_all_bf = xs_all.astype(BF16)
    xs_ref[...] = xs_all_bf
    bc_ref[...] = bc_all.astype(BF16)

    left = lax.broadcasted_iota(jnp.int32, (CHUNK, LANES), 1) < HEADDIM
    h_cur = h_scr[...]
    for u in range(cps):
        rows = slice(u * CHUNK, (u + 1) * CHUNK)
        xs_bf = xs_all_bf[rows, :]
        bc = bc_all[rows, :]
        dt = _softplus(dt_ref[rows, :] + dtb_ref[...])
        a = dt * (-jnp.exp(alog_ref[...]))
        acs = _cumsum_rows(a)
        acs_t = acs.T
        w2_t = dt.T * jnp.exp(acs_t[:, CHUNK - 1:CHUNK] - acs_t)
        cdec = _expand_heads(jnp.broadcast_to(jnp.exp(acs[CHUNK - 1:CHUNK, :]), (8, LANES)), ef_ref)[0:1, :]

        parts = []
        for g in range(GROUPS):
            bt = bc[:, g * STATE:(g + 1) * STATE].T
            for k in range(HEADS_PER_GROUP // 2):
                h = g * HEADS_PER_GROUP + 2 * k
                lhs = jnp.concatenate([(bt * w2_t[h:h + 1, :]).astype(BF16),
                                       (bt * w2_t[h + 1:h + 2, :]).astype(BF16)], axis=1)
                parts.append(_dot(lhs, _pair_rhs(xs_bf, h // 2, left)))
        s_loc = jnp.concatenate(parts, axis=1)

        hprev_ref[u] = h_cur.astype(BF16)
        h_cur = cdec * h_cur + s_loc
    h_scr[...] = h_cur

    @pl.when(last)
    def _():
        hfin_ref[0] = h_cur.T


def _ssd_fwd(proj, dt_raw, h0, consts, nseq, nchunks):
    t = nseq * nchunks * CHUNK
    cps = min(SSD_FWD_CHUNKS_PER_STEP, nchunks)
    nsteps = nchunks // cps
    rows = cps * CHUNK
    halo = 16
    per = rows // halo
    n_halo = t // halo
    gidx = lambda b, c: b * nsteps + c
    prev_blk = lambda b, c: jnp.maximum(gidx(b, c) * per - 1, 0)
    next_blk = lambda b, c: jnp.minimum((gidx(b, c) + 1) * per, n_halo - 1)
    in_specs = [
        pl.BlockSpec((rows, D_SSD), lambda b, c: (gidx(b, c), 1)),
        pl.BlockSpec((rows, 2 * BC_W), lambda b, c: (gidx(b, c), 4)),
        pl.BlockSpec((halo, D_SSD), lambda b, c: (prev_blk(b, c), 1)),
        pl.BlockSpec((halo, 2 * BC_W), lambda b, c: (prev_blk(b, c), 4)),
        pl.BlockSpec((halo, D_SSD), lambda b, c: (next_blk(b, c), 1)),
        pl.BlockSpec((halo, 2 * BC_W), lambda b, c: (next_blk(b, c), 4)),
        pl.BlockSpec((rows, LANES), lambda b, c: (gidx(b, c), 0)),
        pl.BlockSpec((3, D_SSD), lambda b, c: (0, 0)),
        pl.BlockSpec((3, 2 * BC_W), lambda b, c: (0, 0)),
        pl.BlockSpec((1, D_SSD), lambda b, c: (0, 0)),
        pl.BlockSpec((1, 2 * BC_W), lambda b, c: (0, 0)),
        pl.BlockSpec((1, LANES), lambda b, c: (0, 0)),
        pl.BlockSpec((1, LANES), lambda b, c: (0, 0)),
        pl.BlockSpec((LANES, D_SSD), lambda b, c: (0, 0)),
        pl.BlockSpec((2 * rows, rows), lambda b, c: (0, 0)),
    ]
    args = [proj, proj, proj, proj, proj, proj, dt_raw,
            consts["cw_x"], consts["cw_bc"], consts["cb_x"], consts["cb_bc"],
            consts["dt_bias"], consts["a_log"], consts["e_fwd"], _row_shift_matrix(rows)]
    has_h0 = h0 is not None
    if has_h0:
        in_specs.append(pl.BlockSpec((1, D_SSD, STATE), lambda b, c: (b, 0, 0)))
        args.append(h0)
    return pl.pallas_call(
        functools.partial(_ssd_fwd_kernel, has_h0, cps),
        grid=(nseq, nsteps),
        in_specs=in_specs,
        out_specs=[pl.BlockSpec((rows, D_SSD), lambda b, c: (gidx(b, c), 0)),
                   pl.BlockSpec((rows, 2 * BC_W), lambda b, c: (gidx(b, c), 0)),
                   pl.BlockSpec((cps, STATE, D_SSD), lambda b, c: (gidx(b, c), 0, 0)),
                   pl.BlockSpec((1, D_SSD, STATE), lambda b, c: (b, 0, 0))],
        out_shape=[jax.ShapeDtypeStruct((t, D_SSD), BF16),
                   jax.ShapeDtypeStruct((t, 2 * BC_W), BF16),
                   jax.ShapeDtypeStruct((nseq * nchunks, STATE, D_SSD), BF16),
                   jax.ShapeDtypeStruct((nseq, D_SSD, STATE), F32)],
        scratch_shapes=[pltpu.VMEM((STATE, D_SSD), F32)],
        compiler_params=_cparams(("arbitrary", "arbitrary")),
        name="ssd_fwd",
    )(*args)


def _ssd_bwd_kernel(has_h0, cps, *refs):
    (xs_ref, bc_ref, z_ref, dt_ref, hprev_ref, dtb_ref, alog_ref, dvec_ref, nw_ref,
     eb_ref) = refs[:10]
    pos = 10
    h0_ref = None
    if has_h0:
        h0_ref = refs[pos]
        pos += 1
    y_ref, hfin_ref, h_scr = refs[pos:pos + 3]

    c = pl.program_id(1)
    nsteps = pl.num_programs(1)

    @pl.when(c == 0)
    def _():
        if has_h0:
            h_scr[...] = h0_ref[0].T
        else:
            h_scr[...] = jnp.zeros_like(h_scr)

    ri = lax.broadcasted_iota(jnp.int32, (CHUNK, CHUNK), 0)
    ci = lax.broadcasted_iota(jnp.int32, (CHUNK, CHUNK), 1)
    lower = ri >= ci
    diag = ri == ci
    left = lax.broadcasted_iota(jnp.int32, (CHUNK, LANES), 1) < HEADDIM

    h_b = h_scr[...]
    for u in reversed(range(cps)):
        rows = slice(u * CHUNK, (u + 1) * CHUNK)
        xs_bf = xs_ref[rows, :]
        xs = xs_bf.astype(F32)
        bc_bf = bc_ref[rows, :]

        dt = _softplus(dt_ref[rows, :] + dtb_ref[...])
        a = dt * (-jnp.exp(alog_ref[...]))
        acs = _cumsum_rows(a) * LOG2E
        rcs = _rev_cumsum_rows(a) * LOG2E
        acs_t = acs.T
        rcs_t = rcs.T
        dt_t = dt.T
        lg_t = jnp.log2(dt_t)
        rf_t = acs_t - lg_t
        rb_t = rcs_t - lg_t
        w2b_t = dt_t * jnp.exp2(rcs_t[:, 0:1] - rcs_t)
        cdec = _expand_heads(jnp.broadcast_to(jnp.exp2(rcs[0:1, :]), (8, LANES)), eb_ref)[0:1, :]

        h_f = hprev_ref[u]
        h_b_bf = h_b.astype(BF16)

        y_parts = []
        s_parts = []
        for g in range(GROUPS):
            b_g = bc_bf[:, g * STATE:(g + 1) * STATE]
            c_g = bc_bf[:, BC_W + g * STATE:BC_W + (g + 1) * STATE]
            cb = _dot_nt(c_g, b_g)
            bt = b_g.astype(F32).T
            off_f = _dot(c_g, h_f[:, g * GROUP_W:(g + 1) * GROUP_W])
            off_b = _dot(c_g, h_b_bf[:, g * GROUP_W:(g + 1) * GROUP_W])
            for k in range(HEADS_PER_GROUP // 2):
                ms, bs, cfs, cbs = [], [], [], []
                for h in (g * HEADS_PER_GROUP + 2 * k, g * HEADS_PER_GROUP + 2 * k + 1):
                    hb = HEADS + h
                    cf = acs[:, h:h + 1]
                    cbk = rcs[:, hb:hb + 1]
                    e = jnp.exp2(jnp.where(lower, cf - rf_t[h:h + 1, :], cbk - rb_t[hb:hb + 1, :]))
                    e = e + jnp.where(diag, dt_t[hb:hb + 1, :], 0.0)
                    ms.append((cb * e).astype(BF16))
                    bs.append((bt * w2b_t[hb:hb + 1, :]).astype(BF16))
                    cfs.append(cf)
                    cbs.append(cbk)
                lhs = jnp.concatenate([jnp.concatenate(ms, axis=1), jnp.concatenate(bs, axis=1)], axis=0)
                out = _dot(lhs, _pair_rhs(xs_bf, g * HEADS_PER_GROUP // 2 + k, left))
                ef = jnp.exp2(jnp.where(left, cfs[0], cfs[1]))
                eb = jnp.exp2(jnp.where(left, cbs[0], cbs[1]))
                cols = slice(k * LANES, (k + 1) * LANES)
                y_parts.append(out[:CHUNK, :] + ef * off_f[:, cols] + eb * off_b[:, cols])
                s_parts.append(out[CHUNK:, :])
        y = jnp.concatenate(y_parts, axis=1) + dvec_ref[...] * xs
        y = y * _silu(z_ref[rows, :].astype(F32))
        ms_y = jnp.mean(y * y, axis=-1, keepdims=True)
        y_ref[rows, :] = (y * lax.rsqrt(ms_y + EPS) * nw_ref[...]).astype(BF16)

        h_b = cdec * h_b + jnp.concatenate(s_parts, axis=1)
    h_scr[...] = h_b

    @pl.when(c == nsteps - 1)
    def _():
        hfin_ref[0] = h_b.T


def _ssd_bwd(xs, bc, proj, dt_raw, hprev, h0, consts, nseq, nchunks):
    t = nseq * nchunks * CHUNK
    cps = min(SSD_BWD_CHUNKS_PER_STEP, nchunks)
    nsteps = nchunks // cps
    rows = cps * CHUNK
    gidx = lambda b, c: b * nsteps + (nsteps - 1 - c)
    in_specs = [
        pl.BlockSpec((rows, D_SSD), lambda b, c: (gidx(b, c), 0)),
        pl.BlockSpec((rows, 2 * BC_W), lambda b, c: (gidx(b, c), 0)),
        pl.BlockSpec((rows, D_SSD), lambda b, c: (gidx(b, c), 0)),
        pl.BlockSpec((rows, LANES), lambda b, c: (gidx(b, c), 0)),
        pl.BlockSpec((cps, STATE, D_SSD), lambda b, c: (gidx(b, c), 0, 0)),
        pl.BlockSpec((1, LANES), lambda b, c: (0, 0)),
        pl.BlockSpec((1, LANES), lambda b, c: (0, 0)),
        pl.BlockSpec((1, D_SSD), lambda b, c: (0, 0)),
        pl.BlockSpec((1, D_SSD), lambda b, c: (0, 0)),
        pl.BlockSpec((LANES, D_SSD), lambda b, c: (0, 0)),
    ]
    args = [xs, bc, proj, dt_raw, hprev, consts["dt_bias"], consts["a_log"], consts["d_vec"],
            consts["ssd_norm"], consts["e_bwd"]]
    has_h0 = h0 is not None
    if has_h0:
        in_specs.append(pl.BlockSpec((1, D_SSD, STATE), lambda b, c: (b, 0, 0)))
        args.append(h0)
    return pl.pallas_call(
        functools.partial(_ssd_bwd_kernel, has_h0, cps),
        grid=(nseq, nsteps),
        in_specs=in_specs,
        out_specs=[pl.BlockSpec((rows, D_SSD), lambda b, c: (gidx(b, c), 0)),
                   pl.BlockSpec((1, D_SSD, STATE), lambda b, c: (b, 0, 0))],
        out_shape=[jax.ShapeDtypeStruct((t, D_SSD), BF16),
                   jax.ShapeDtypeStruct((nseq, D_SSD, STATE), F32)],
        scratch_shapes=[pltpu.VMEM((STATE, D_SSD), F32)],
        compiler_params=_cparams(("arbitrary", "arbitrary")),
        name="ssd_bwd",
    )(*args)


def _mix_kernel(y_ref, u_ref, v_ref, ga_ref, gb_ref, x_ref, mod_ref, lnw_ref, lnb_ref, ws_ref, bst_ref,
                wbs_ref, wbg_ref, wout_ref, npost_ref, npre2_ref, o_ref, h2_ref, ysgu_scr):
    tm = x_ref.shape[0]
    v = v_ref[...].astype(F32)
    mu = jnp.mean(v, axis=-1, keepdims=True)
    vc = v - mu
    var = jnp.mean(vc * vc, axis=-1, keepdims=True)
    vn = (vc * lax.rsqrt(var + EPS) * lnw_ref[...] + lnb_ref[...]).astype(BF16)
    bst = bst_ref[...]
    for r in range(tm // CHUNK):
        rows = slice(r * CHUNK, (r + 1) * CHUNK)
        for g in range(SGU_GROUPS):
            cols = slice(g * LANES, (g + 1) * LANES)
            s = _dot(ws_ref[g], vn[rows, cols]) + bst[:, g:g + 1]
            ysgu_scr[rows, cols] = (u_ref[rows, cols].astype(F32) * s).astype(BF16)
    br_ssd = _dot(y_ref[...], wbs_ref[...])
    br_sgu = _dot(ysgu_scr[...], wbg_ref[...])
    merged = _sigmoid(ga_ref[...].astype(F32)) * br_ssd + _sigmoid(gb_ref[...].astype(F32)) * br_sgu
    merged = merged.astype(BF16)
    quarter = tm // 4
    for r in range(4):
        rows = slice(r * quarter, (r + 1) * quarter)
        mix = _dot(merged[rows, :], wout_ref[...])
        ms = jnp.mean(mix * mix, axis=-1, keepdims=True)
        x1 = x_ref[rows, :] + mod_ref[0, 2:3, :] * (mix * lax.rsqrt(ms + EPS) * npost_ref[...])
        o_ref[rows, :] = x1
        h2_ref[rows, :] = _modulated_norm(x1, npre2_ref[...], mod_ref[0, 3:4, :], mod_ref[0, 4:5, :]).astype(BF16)


def _mix(y_ssd, proj, x, mod, consts, tokens_per_mod):
    t = x.shape[0]
    tm = 512
    row = lambda i: (i, 0)
    const2 = lambda i: (0, 0)
    return pl.pallas_call(
        _mix_kernel,
        grid=(t // tm,),
        in_specs=[pl.BlockSpec((tm, D_SSD), row),
                  pl.BlockSpec((tm, D_MODEL), lambda i: (i, 5)),
                  pl.BlockSpec((tm, D_MODEL), lambda i: (i, 6)),
                  pl.BlockSpec((tm, D_MODEL), lambda i: (i, 7)),
                  pl.BlockSpec((tm, D_MODEL), lambda i: (i, 8)),
                  pl.BlockSpec((tm, D_MODEL), row),
                  pl.BlockSpec((1, 6, D_MODEL), lambda i: ((i * tm) // tokens_per_mod, 0, 0)),
                  pl.BlockSpec((1, D_MODEL), const2),
                  pl.BlockSpec((1, D_MODEL), const2),
                  pl.BlockSpec((SGU_GROUPS, CHUNK, CHUNK), lambda i: (0, 0, 0)),
                  pl.BlockSpec((CHUNK, SGU_GROUPS), const2),
                  pl.BlockSpec((D_SSD, D_MODEL), const2),
                  pl.BlockSpec((D_MODEL, D_MODEL), const2),
                  pl.BlockSpec((D_MODEL, D_MODEL), const2),
                  pl.BlockSpec((1, D_MODEL), const2),
                  pl.BlockSpec((1, D_MODEL), const2)],
        out_specs=[pl.BlockSpec((tm, D_MODEL), row), pl.BlockSpec((tm, D_MODEL), row)],
        out_shape=[jax.ShapeDtypeStruct((t, D_MODEL), F32), jax.ShapeDtypeStruct((t, D_MODEL), BF16)],
        scratch_shapes=[pltpu.VMEM((tm, D_MODEL), BF16)],
        compiler_params=_cparams(("arbitrary",)),
        name="mix",
    )(y_ssd, proj, proj, proj, proj, x, mod, consts["sgu_norm_w"], consts["sgu_norm_b"], consts["sgu_w"],
      consts["sgu_bt"], consts["w_branch_ssd"], consts["w_branch_sgu"], consts["w_out"], consts["norm_mix_post"],
      consts["norm_ffn_pre"])


def _matmul_kernel(h_ref, w_ref, o_ref):
    o_ref[...] = _dot(h_ref[...], w_ref[...]).astype(BF16)


def _ffn_up(h2, w_up):
    t = h2.shape[0]
    tm, tn = 1024, D_FF
    return pl.pallas_call(
        _matmul_kernel,
        grid=(t // tm, 2 * D_FF // tn),
        in_specs=[pl.BlockSpec((tm, D_MODEL), lambda i, j: (i, 0)),
                  pl.BlockSpec((D_MODEL, tn), lambda i, j: (0, j))],
        out_specs=pl.BlockSpec((tm, tn), lambda i, j: (i, j)),
        out_shape=jax.ShapeDtypeStruct((t, 2 * D_FF), BF16),
        compiler_params=_cparams(("arbitrary", "arbitrary")),
        name="ffn_up",
    )(h2, w_up)


def _gelu_tanh(x):
    h = 0.5 * x
    u = x * (GELU_C + (GELU_C * 0.044715) * (x * x))
    return h + h * jnp.tanh(u)


def _grid_conv(x_bf, prev_bf, next_bf, w, b, first, last):
    tm = x_bf.shape[0]
    zero = jnp.zeros_like(prev_bf)
    ext_bf = jnp.concatenate([jnp.where(first, zero, prev_bf), x_bf, jnp.where(last, zero, next_bf)], axis=0)
    ext = ext_bf.astype(F32)
    n = ext.shape[0]
    col = lax.broadcasted_iota(jnp.int32, ext.shape, 0) % GRID_W
    shifted = (jnp.where(col == 0, 0.0, pltpu.roll(ext, 1, axis=0)).astype(BF16),
               ext_bf,
               jnp.where(col == GRID_W - 1, 0.0, pltpu.roll(ext, n - 1, axis=0)).astype(BF16))
    wb = w.astype(BF16)
    acc = None
    for dy in range(3):
        for dx in range(3):
            term = wb[3 * dy + dx:3 * dy + dx + 1, :] * shifted[dx][dy * GRID_W:dy * GRID_W + tm, :]
            acc = term if acc is None else acc + term
    return acc + b.astype(BF16)


def _seq_conv(x_bf, w, b, seq_len):
    x = x_bf.astype(F32)
    n = x.shape[0]
    pos = lax.broadcasted_iota(jnp.int32, x.shape, 0) % seq_len
    xm1 = jnp.where(pos == 0, 0.0, pltpu.roll(x, 1, axis=0)).astype(BF16)
    xp1 = jnp.where(pos == seq_len - 1, 0.0, pltpu.roll(x, n - 1, axis=0)).astype(BF16)
    wb = w.astype(BF16)
    return wb[3:4, :] * xm1 + wb[4:5, :] * x_bf + wb[5:6, :] * xp1 + b.astype(BF16)


def _ffn_down_kernel(on_grid, seq_len, tiles_per_seq, *refs):
    if on_grid:
        (up_ref, upp_ref, upn_ref, w_ref, b_ref, wd_ref, x_ref, mod_ref, npost_ref, o_ref, g0_scr, g1_scr, acc_scr) = refs
    else:
        (up_ref, w_ref, b_ref, wd_ref, x_ref, mod_ref, npost_ref, o_ref, g0_scr, g1_scr, acc_scr) = refs
    i = pl.program_id(0)
    first = i % tiles_per_seq == 0
    last = i % tiles_per_seq == tiles_per_seq - 1

    def conv(c0):
        cols = pl.ds(c0, FF_BLK)
        if on_grid:
            return _grid_conv(up_ref[:, cols], upp_ref[:, cols], upn_ref[:, cols], w_ref[:, cols], b_ref[:, cols],
                              first, last)
        return _seq_conv(up_ref[:, cols], w_ref[:, cols], b_ref[:, cols], seq_len)

    def geglu_block(k, dst_ref):
        c0 = pl.multiple_of(k * FF_BLK, FF_BLK)
        a = conv(c0)
        val = conv(pl.multiple_of(c0 + D_FF, FF_BLK))
        dst_ref[...] = (_gelu_tanh(a) * val).astype(BF16)

    def down_block(k, src_ref):
        rows = pl.ds(pl.multiple_of(k * FF_BLK, FF_BLK), FF_BLK)
        acc_scr[...] += _dot(src_ref[...], wd_ref[rows, :])

    acc_scr[...] = jnp.zeros_like(acc_scr)
    geglu_block(0, g0_scr)

    def block_pair(m, carry):
        k = 2 * m + 1
        down_block(k - 1, g0_scr)
        geglu_block(k, g1_scr)
        down_block(k, g1_scr)
        geglu_block(k + 1, g0_scr)
        return carry

    lax.fori_loop(0, (N_FF_BLK - 1) // 2, block_pair, 0)
    down_block(N_FF_BLK - 1, g0_scr)
    f = acc_scr[...]
    ms = jnp.mean(f * f, axis=-1, keepdims=True)
    o_ref[...] = x_ref[...] + mod_ref[0, 5:6, :] * (f * lax.rsqrt(ms + EPS) * npost_ref[...])


def _ffn_down(up, x, mod, consts, tokens_per_mod, on_grid, seq_len):
    t = x.shape[0]
    tm = 512
    tiles_per_seq = max(seq_len // tm, 1)
    rows_per_tile = tm // GRID_W
    n_rows = t // GRID_W
    in_specs = [pl.BlockSpec((tm, 2 * D_FF), lambda i: (i, 0))]
    args = [up]
    if on_grid:
        in_specs += [pl.BlockSpec((GRID_W, 2 * D_FF), lambda i: (jnp.maximum(i * rows_per_tile - 1, 0), 0)),
                     pl.BlockSpec((GRID_W, 2 * D_FF), lambda i: (jnp.minimum((i + 1) * rows_per_tile, n_rows - 1), 0))]
        args += [up, up]
    in_specs += [pl.BlockSpec((9, 2 * D_FF), lambda i: (0, 0)),
                 pl.BlockSpec((1, 2 * D_FF), lambda i: (0, 0)),
                 pl.BlockSpec((D_FF, D_MODEL), lambda i: (0, 0)),
                 pl.BlockSpec((tm, D_MODEL), lambda i: (i, 0)),
                 pl.BlockSpec((1, 6, D_MODEL), lambda i: ((i * tm) // tokens_per_mod, 0, 0)),
                 pl.BlockSpec((1, D_MODEL), lambda i: (0, 0))]
    args += [consts["ffn_conv_w"], consts["ffn_conv_b"], consts["ffn_w_down"], x, mod, consts["norm_ffn_post"]]
    return pl.pallas_call(
        functools.partial(_ffn_down_kernel, on_grid, seq_len, tiles_per_seq),
        grid=(t // tm,),
        in_specs=in_specs,
        out_specs=pl.BlockSpec((tm, D_MODEL), lambda i: (i, 0)),
        out_shape=jax.ShapeDtypeStruct((t, D_MODEL), F32),
        scratch_shapes=[pltpu.VMEM((tm, FF_BLK), BF16), pltpu.VMEM((tm, FF_BLK), BF16),
                        pltpu.VMEM((tm, D_MODEL), F32)],
        compiler_params=_cparams(("arbitrary",)),
        name="ffn_down",
    )(*args)


def _trunk_path(x, mod, h0_f, h0_b, on_grid, consts):
    nseq, seq_len, _ = x.shape
    t = nseq * seq_len
    nchunks = seq_len // CHUNK
    tokens_per_mod = t // mod.shape[0]
    x2d = x.reshape(t, D_MODEL)

    proj, dt_raw = _inproj(x2d, mod, consts["norm_mix_pre"], consts["w_in_main"], consts["w_in_dt"], tokens_per_mod)
    xs, bc, hprev, hf = _ssd_fwd(proj, dt_raw, h0_f, consts, nseq, nchunks)
    y_ssd, hb = _ssd_bwd(xs, bc, proj, dt_raw, hprev, h0_b, consts, nseq, nchunks)
    x1, h2 = _mix(y_ssd, proj, x2d, mod, consts, tokens_per_mod)
    up = _ffn_up(h2, consts["ffn_w_up"])
    x2 = _ffn_down(up, x1, mod, consts, tokens_per_mod, on_grid, seq_len)
    return x2.reshape(nseq, seq_len, D_MODEL), hf, hb


def _head_expansion(offset):
    rows = jnp.arange(LANES)[:, None]
    cols = jnp.arange(D_SSD)[None, :] // HEADDIM
    return (rows == cols + offset).astype(BF16)


def _layer_consts(i, p):
    w_in_main, w_in_dt = _w_in_prep(p["w_in"], i)
    row = lambda v: v.reshape(1, -1).astype(F32)
    pad_lanes = lambda v: jnp.pad(v.reshape(1, -1).astype(F32), ((0, 0), (0, LANES - N_DT)))
    conv_w = p["ssd_conv_w"][i]
    conv_b = p["ssd_conv_b"][i]
    return {
        "norm_mix_pre": row(p["norm_mix_pre"][i]),
        "norm_mix_post": row(p["norm_mix_post"][i]),
        "norm_ffn_pre": row(p["norm_ffn_pre"][i]),
        "norm_ffn_post": row(p["norm_ffn_post"][i]),
        "w_in_main": w_in_main, "w_in_dt": w_in_dt,
        "cw_x": conv_w[:, :D_SSD], "cw_bc": conv_w[:, D_SSD:],
        "cb_x": row(conv_b[:D_SSD]), "cb_bc": row(conv_b[D_SSD:]),
        "dt_bias": pad_lanes(p["ssd_dt_bias"][i]),
        "a_log": pad_lanes(p["ssd_a_log"][i]),
        "d_vec": row(jnp.repeat(p["ssd_d"][i], HEADDIM)),
        "ssd_norm": row(p["ssd_norm"][i]),
        "e_fwd": _head_expansion(0), "e_bwd": _head_expansion(HEADS),
        "sgu_norm_w": row(p["sgu_norm_w"][i]), "sgu_norm_b": row(p["sgu_norm_b"][i]),
        "sgu_w": p["sgu_w"][i].astype(BF16),
        "sgu_bt": jnp.transpose(p["sgu_b"][i]).astype(F32),
        "w_branch_ssd": p["w_branch_ssd"][i].astype(BF16),
        "w_branch_sgu": p["w_branch_sgu"][i].astype(BF16),
        "w_out": p["w_out"][i].astype(BF16),
        "ffn_w_up": p["ffn_w_up"][i].astype(BF16),
        "ffn_conv_w": p["ffn_conv_w"][i].reshape(9, 2 * D_FF).astype(F32),
        "ffn_conv_b": row(p["ffn_conv_b"][i]),
        "ffn_w_down": p["ffn_w_down"][i].astype(BF16),
    }


def kernel(x_prompt, x_sample, state_ssd_fwd, state_ssd_bwd, c, c_ctx, w_mod, b_mod, norm_mix_pre, norm_mix_post, norm_ffn_pre, norm_ffn_post, w_in, ssd_conv_w, ssd_conv_b, ssd_a_log, ssd_dt_bias, ssd_d, ssd_norm, sgu_norm_w, sgu_norm_b, sgu_w, sgu_b, w_branch_ssd, w_branch_sgu, w_out, ffn_w_up, ffn_conv_w, ffn_conv_b, ffn_w_down):
    params = dict(norm_mix_pre=norm_mix_pre, norm_mix_post=norm_mix_post, norm_ffn_pre=norm_ffn_pre,
                  norm_ffn_post=norm_ffn_post, w_in=w_in, ssd_conv_w=ssd_conv_w, ssd_conv_b=ssd_conv_b,
                  ssd_a_log=ssd_a_log, ssd_dt_bias=ssd_dt_bias, ssd_d=ssd_d, ssd_norm=ssd_norm,
                  sgu_norm_w=sgu_norm_w, sgu_norm_b=sgu_norm_b, sgu_w=sgu_w, sgu_b=sgu_b,
                  w_branch_ssd=w_branch_ssd, w_branch_sgu=w_branch_sgu, w_out=w_out, ffn_w_up=ffn_w_up,
                  ffn_conv_w=ffn_conv_w, ffn_conv_b=ffn_conv_b, ffn_w_down=ffn_w_down)
    depth = w_mod.shape[0]
    n_lat = c.shape[0]
    c_rows = jnp.concatenate([c_ctx[None, :], c, jnp.zeros((8 - 1 - n_lat, D_MODEL), F32)], axis=0)
    xp, xs = x_prompt, x_sample
    new_f, new_b = [], []
    for i in range(depth):
        consts = _layer_consts(i, params)
        mod = _mod_vectors(c_rows, w_mod[i], b_mod[i]).reshape(8, 6, D_MODEL)
        xp, hf, hb = _trunk_path(xp, mod[0:1], None, None, False, consts)
        new_f.append(hf.reshape(-1, HEADS, HEADDIM, STATE))
        new_b.append(hb.reshape(-1, HEADS, HEADDIM, STATE))
        xs, _, _ = _trunk_path(xs, mod[1:1 + n_lat],
                               state_ssd_fwd[:, i].reshape(n_lat, D_SSD, STATE),
                               state_ssd_bwd[:, i].reshape(n_lat, D_SSD, STATE), True, consts)
    return (xp, xs, jnp.stack(new_f, axis=1).astype(x_prompt.dtype), jnp.stack(new_b, axis=1).astype(x_prompt.dtype))
```

```python
import functools

import jax
import jax.numpy as jnp
from jax import lax
from jax.experimental import pallas as pl
from jax.experimental.pallas import tpu as pltpu

F32 = jnp.float32
BF16 = jnp.bfloat16

D_MODEL = 1024
GRID_W = 64
EPS = 1e-6
LOG2E = 1.4426950408889634
GELU_C = 0.7978845608028654
D_SSD = 2 * D_MODEL
HEADDIM = 64
HEADS = D_SSD // HEADDIM
STATE = 128
GROUPS = 4
HEADS_PER_GROUP = HEADS // GROUPS
GROUP_W = HEADS_PER_GROUP * HEADDIM
BC_W = GROUPS * STATE
CONV_CH = D_SSD + 2 * BC_W
CHUNK = 128
SSD_FWD_CHUNKS_PER_STEP = 2
SSD_BWD_CHUNKS_PER_STEP = 4
SGU_GROUPS = 8
D_FF = 2816
SPLIT_XBC = D_SSD + CONV_CH
N_DT = 2 * HEADS
LANES = 128
PROJ_COLS = 9216
FF_BLK = 256
N_FF_BLK = D_FF // FF_BLK

VMEM_LIMIT_BYTES = 56 * 1024 * 1024


def _cparams(sem):
    return pltpu.CompilerParams(dimension_semantics=sem, vmem_limit_bytes=VMEM_LIMIT_BYTES)


def _sigmoid(x):
    return 0.5 + 0.5 * jnp.tanh(0.5 * x)


def _silu(x):
    h = 0.5 * x
    return h + h * jnp.tanh(h)


def _dot(a, b):
    return jnp.dot(a, b, preferred_element_type=F32)


def _dot_nt(a, b):
    return lax.dot_general(a, b, (((1,), (1,)), ((), ())), preferred_element_type=F32)


def _mod_kernel(c_ref, w_ref, b_ref, o_ref):
    c = c_ref[...]
    o_ref[...] = jnp.dot(_silu(c), w_ref[...], preferred_element_type=F32,
                         precision=lax.Precision.HIGHEST) + b_ref[...]


def _mod_vectors(c_rows, w_mod, b_mod):
    rows = c_rows.shape[0]
    tn = 1024
    return pl.pallas_call(
        _mod_kernel,
        grid=(6 * D_MODEL // tn,),
        in_specs=[pl.BlockSpec((rows, D_MODEL), lambda j: (0, 0)),
                  pl.BlockSpec((D_MODEL, tn), lambda j: (0, j)),
                  pl.BlockSpec((1, tn), lambda j: (0, j))],
        out_specs=pl.BlockSpec((rows, tn), lambda j: (0, j)),
        out_shape=jax.ShapeDtypeStruct((rows, 6 * D_MODEL), F32),
        compiler_params=_cparams(("arbitrary",)),
        name="mod",
    )(c_rows, w_mod, b_mod.reshape(1, -1))


def _w_in_prep_kernel(wt_ref, main_ref, dt_ref):
    main_ref[:, 0:SPLIT_XBC] = wt_ref[0:SPLIT_XBC, :].T.astype(BF16)
    main_ref[:, SPLIT_XBC:] = wt_ref[SPLIT_XBC + N_DT:, :].T.astype(BF16)
    dt = wt_ref[SPLIT_XBC:SPLIT_XBC + LANES, :].T
    lane = lax.broadcasted_iota(jnp.int32, dt.shape, 1)
    dt_ref[...] = jnp.where(lane < N_DT, dt, 0.0).astype(BF16)


def _w_in_prep(w_in, layer):
    rows = 128
    return pl.pallas_call(
        _w_in_prep_kernel,
        grid=(D_MODEL // rows,),
        in_specs=[pl.BlockSpec((PROJ_COLS + N_DT, rows), lambda r: (0, r))],
        out_specs=[pl.BlockSpec((rows, PROJ_COLS), lambda r: (r, 0)),
                   pl.BlockSpec((rows, LANES), lambda r: (r, 0))],
        out_shape=[jax.ShapeDtypeStruct((D_MODEL, PROJ_COLS), BF16),
                   jax.ShapeDtypeStruct((D_MODEL, LANES), BF16)],
        compiler_params=_cparams(("arbitrary",)),
        name="w_in_prep",
    )(jnp.swapaxes(w_in[layer], 0, 1))


def _modulated_norm(x, nw, shift, scale):
    ms = jnp.mean(x * x, axis=-1, keepdims=True)
    return (x * lax.rsqrt(ms + EPS) * nw) * (1.0 + scale) + shift


def _inproj_kernel(n_col_steps, x0_ref, xn_ref, mod0_ref, modn_ref, nw_ref, w_ref, wdt_ref, o_ref, dt_ref, ha_scr, hb_scr):
    i = pl.program_id(0)
    j = pl.program_id(1)
    part = xn_ref.shape[0] // n_col_steps
    rows = pl.ds(pl.multiple_of(j * part, part), part)

    def norm(x, mod_ref):
        return _modulated_norm(x, nw_ref[...], mod_ref[0, 0:1, :], mod_ref[0, 1:2, :]).astype(BF16)

    @pl.when((i == 0) & (j == 0))
    def _():
        ha_scr[...] = norm(x0_ref[...], mod0_ref)

    def step(cur_scr, nxt_scr):
        nxt_scr[rows, :] = norm(xn_ref[rows, :], modn_ref)
        dt_ref[rows, :] = _dot(cur_scr[rows, :], wdt_ref[...])
        o_ref[...] = _dot(cur_scr[...], w_ref[...]).astype(BF16)

    @pl.when(i % 2 == 0)
    def _():
        step(ha_scr, hb_scr)

    @pl.when(i % 2 == 1)
    def _():
        step(hb_scr, ha_scr)


def _inproj(x, mod, nw, w_main, w_dt, tokens_per_mod):
    t = x.shape[0]
    tm, tn = 1024, PROJ_COLS // 4
    n_tiles = t // tm
    nxt = lambda i: jnp.minimum(i + 1, n_tiles - 1)
    return pl.pallas_call(
        functools.partial(_inproj_kernel, PROJ_COLS // tn),
        grid=(n_tiles, PROJ_COLS // tn),
        in_specs=[pl.BlockSpec((tm, D_MODEL), lambda i, j: (0, 0)),
                  pl.BlockSpec((tm, D_MODEL), lambda i, j: (nxt(i), 0)),
                  pl.BlockSpec((1, 6, D_MODEL), lambda i, j: (0, 0, 0)),
                  pl.BlockSpec((1, 6, D_MODEL), lambda i, j: ((nxt(i) * tm) // tokens_per_mod, 0, 0)),
                  pl.BlockSpec((1, D_MODEL), lambda i, j: (0, 0)),
                  pl.BlockSpec((D_MODEL, tn), lambda i, j: (0, j)),
                  pl.BlockSpec((D_MODEL, LANES), lambda i, j: (0, 0))],
        out_specs=[pl.BlockSpec((tm, tn), lambda i, j: (i, j)),
                   pl.BlockSpec((tm, LANES), lambda i, j: (i, 0))],
        out_shape=[jax.ShapeDtypeStruct((t, PROJ_COLS), BF16),
                   jax.ShapeDtypeStruct((t, LANES), F32)],
        scratch_shapes=[pltpu.VMEM((tm, D_MODEL), BF16), pltpu.VMEM((tm, D_MODEL), BF16)],
        compiler_params=_cparams(("arbitrary", "arbitrary")),
        name="inproj",
    )(x, x, mod, mod, nw, w_main, w_dt)


def _softplus(x):
    return jnp.maximum(x, 0.0) + jnp.log1p(jnp.exp(-jnp.abs(x)))


def _cumsum_rows(a):
    n = a.shape[0]
    rid = lax.broadcasted_iota(jnp.int32, a.shape, 0)
    s = 1
    while s < n:
        a = a + jnp.where(rid >= s, pltpu.roll(a, s, axis=0), 0.0)
        s *= 2
    return a


def _rev_cumsum_rows(a):
    n = a.shape[0]
    rid = lax.broadcasted_iota(jnp.int32, a.shape, 0)
    s = 1
    while s < n:
        a = a + jnp.where(rid < n - s, pltpu.roll(a, n - s, axis=0), 0.0)
        s *= 2
    return a


def _expand_heads(w, e_ref):
    hi = w.astype(BF16)
    lo = (w - hi.astype(F32)).astype(BF16)
    e = e_ref[...]
    return _dot(hi, e) + _dot(lo, e)


def _row_shift_matrix(n):
    r = jnp.arange(n)[:, None]
    c = jnp.arange(n)[None, :]
    return jnp.concatenate([c == r - 1, c == r + 1], axis=0).astype(BF16)


def _conv3_silu(main_ref, prev_ref, next_ref, w_ref, b_ref, shift_ref, first, last):
    x_bf = main_ref[...]
    n = x_bf.shape[0]
    x = x_bf.astype(F32)
    sh = _dot(shift_ref[...], x_bf)
    xm1 = sh[:n, :]
    xp1 = sh[n:, :]
    prow = jnp.where(first, 0.0, prev_ref[...].astype(F32)[-1:, :])
    nrow = jnp.where(last, 0.0, next_ref[...].astype(F32)[0:1, :])
    rid = lax.broadcasted_iota(jnp.int32, (8, x.shape[1]), 0)
    xm1 = jnp.concatenate([jnp.where(rid == 0, prow, xm1[:8, :]), xm1[8:, :]], axis=0)
    xp1 = jnp.concatenate([xp1[:-8, :], jnp.where(rid == 7, nrow, xp1[-8:, :])], axis=0)
    w = w_ref[...]
    y = w[0:1, :] * xm1 + w[1:2, :] * x + w[2:3, :] * xp1 + b_ref[...]
    return _silu(y)


def _dt_and_decay_rates(dt_ref, dtb_ref, alog_ref):
    dt = _softplus(dt_ref[...] + dtb_ref[...])
    a = dt * (-jnp.exp(alog_ref[...]))
    return dt, a


def _pair_rhs(xs_bf, pair, left):
    xp = xs_bf[:, pair * LANES:(pair + 1) * LANES]
    zero = jnp.zeros_like(xp)
    return jnp.concatenate([jnp.where(left, xp, zero), jnp.where(left, zero, xp)], axis=0)


def _ssd_fwd_kernel(has_h0, cps, *refs):
    (xm_ref, bcm_ref, xp_ref, bcp_ref, xn_ref, bcn_ref, dt_ref,
     cwx_ref, cwbc_ref, cbx_ref, cbbc_ref, dtb_ref, alog_ref, ef_ref, s3_ref) = refs[:15]
    pos = 15
    h0_ref = None
    if has_h0:
        h0_ref = refs[pos]
        pos += 1
    xs_ref, bc_ref, hprev_ref, hfin_ref, h_scr = refs[pos:pos + 5]

    c = pl.program_id(1)
    nsteps = pl.num_programs(1)
    first = c == 0
    last = c == nsteps - 1

    @pl.when(first)
    def _():
        if has_h0:
            h_scr[...] = h0_ref[0].T
        else:
            h_scr[...] = jnp.zeros_like(h_scr)

    xs_all = _conv3_silu(xm_ref, xp_ref, xn_ref, cwx_ref, cbx_ref, s3_ref, first, last)
    bc_all = _conv3_silu(bcm_ref, bcp_ref, bcn_ref, cwbc_ref, cbbc_ref, s3_ref, first, last)
    xs_all_bf = xs_all.astype(BF16)
    xs_ref[...] = xs_all_bf
    bc_ref[...] = bc_all.astype(BF16)

    left = lax.broadcasted_iota(jnp.int32, (CHUNK, LANES), 1) < HEADDIM
    h_cur = h_scr[...]
    for u in range(cps):
        rows = slice(u * CHUNK, (u + 1) * CHUNK)
        xs_bf = xs_all_bf[rows, :]
        bc = bc_all[rows, :]
        dt = _softplus(dt_ref[rows, :] + dtb_ref[...])
        a = dt * (-jnp.exp(alog_ref[...]))
        acs = _cumsum_rows(a)
        acs_t = acs.T
        w2_t = dt.T * jnp.exp(acs_t[:, CHUNK - 1:CHUNK] - acs_t)
        cdec = _expand_heads(jnp.broadcast_to(jnp.exp(acs[CHUNK - 1:CHUNK, :]), (8, LANES)), ef_ref)[0:1, :]

        parts = []
        for g in range(GROUPS):
            bt = bc[:, g * STATE:(g + 1) * STATE].T
            for k in range(HEADS_PER_GROUP // 2):
                h = g * HEADS_PER_GROUP + 2 * k
                lhs = jnp.concatenate([(bt * w2_t[h:h + 1, :]).astype(BF16),
                                       (bt * w2_t[h + 1:h + 2, :]).astype(BF16)], axis=1)
                parts.append(_dot(lhs, _pair_rhs(xs_bf, h // 2, left)))
        s_loc = jnp.concatenate(parts, axis=1)

        hprev_ref[u] = h_cur.astype(BF16)
        h_cur = cdec * h_cur + s_loc
    h_scr[...] = h_cur

    @pl.when(last)
    def _():
        hfin_ref[0] = h_cur.T


def _ssd_fwd(proj, dt_raw, h0, consts, nseq, nchunks):
    t = nseq * nchunks * CHUNK
    cps = min(SSD_FWD_CHUNKS_PER_STEP, nchunks)
    nsteps = nchunks // cps
    rows = cps * CHUNK
    halo = 16
    per = rows // halo
    n_halo = t // halo
    gidx = lambda b, c: b * nsteps + c
    prev_blk = lambda b, c: jnp.maximum(gidx(b, c) * per - 1, 0)
    next_blk = lambda b, c: jnp.minimum((gidx(b, c) + 1) * per, n_halo - 1)
    in_specs = [
        pl.BlockSpec((rows, D_SSD), lambda b, c: (gidx(b, c), 1)),
        pl.BlockSpec((rows, 2 * BC_W), lambda b, c: (gidx(b, c), 4)),
        pl.BlockSpec((halo, D_SSD), lambda b, c: (prev_blk(b, c), 1)),
        pl.BlockSpec((halo, 2 * BC_W), lambda b, c: (prev_blk(b, c), 4)),
        pl.BlockSpec((halo, D_SSD), lambda b, c: (next_blk(b, c), 1)),
        pl.BlockSpec((halo, 2 * BC_W), lambda b, c: (next_blk(b, c), 4)),
        pl.BlockSpec((rows, LANES), lambda b, c: (gidx(b, c), 0)),
        pl.BlockSpec((3, D_SSD), lambda b, c: (0, 0)),
        pl.BlockSpec((3, 2 * BC_W), lambda b, c: (0, 0)),
        pl.BlockSpec((1, D_SSD), lambda b, c: (0, 0)),
        pl.BlockSpec((1, 2 * BC_W), lambda b, c: (0, 0)),
        pl.BlockSpec((1, LANES), lambda b, c: (0, 0)),
        pl.BlockSpec((1, LANES), lambda b, c: (0, 0)),
        pl.BlockSpec((LANES, D_SSD), lambda b, c: (0, 0)),
        pl.BlockSpec((2 * rows, rows), lambda b, c: (0, 0)),
    ]
    args = [proj, proj, proj, proj, proj, proj, dt_raw,
            consts["cw_x"], consts["cw_bc"], consts["cb_x"], consts["cb_bc"],
            consts["dt_bias"], consts["a_log"], consts["e_fwd"], _row_shift_matrix(rows)]
    has_h0 = h0 is not None
    if has_h0:
        in_specs.append(pl.BlockSpec((1, D_SSD, STATE), lambda b, c: (b, 0, 0)))
        args.append(h0)
    return pl.pallas_call(
        functools.partial(_ssd_fwd_kernel, has_h0, cps),
        grid=(nseq, nsteps),
        in_specs=in_specs,
        out_specs=[pl.BlockSpec((rows, D_SSD), lambda b, c: (gidx(b, c), 0)),
                   pl.BlockSpec((rows, 2 * BC_W), lambda b, c: (gidx(b, c), 0)),
                   pl.BlockSpec((cps, STATE, D_SSD), lambda b, c: (gidx(b, c), 0, 0)),
                   pl.BlockSpec((1, D_SSD, STATE), lambda b, c: (b, 0, 0))],
        out_shape=[jax.ShapeDtypeStruct((t, D_SSD), BF16),
                   jax.ShapeDtypeStruct((t, 2 * BC_W), BF16),
                   jax.ShapeDtypeStruct((nseq * nchunks, STATE, D_SSD), BF16),
                   jax.ShapeDtypeStruct((nseq, D_SSD, STATE), F32)],
        scratch_shapes=[pltpu.VMEM((STATE, D_SSD), F32)],
        compiler_params=_cparams(("arbitrary", "arbitrary")),
        name="ssd_fwd",
    )(*args)


def _ssd_bwd_kernel(has_h0, cps, *refs):
    (xs_ref, bc_ref, z_ref, dt_ref, hprev_ref, dtb_ref, alog_ref, dvec_ref, nw_ref,
     eb_ref) = refs[:10]
    pos = 10
    h0_ref = None
    if has_h0:
        h0_ref = refs[pos]
        pos += 1
    y_ref, hfin_ref, h_scr = refs[pos:pos + 3]

    c = pl.program_id(1)
    nsteps = pl.num_programs(1)

    @pl.when(c == 0)
    def _():
        if has_h0:
            h_scr[...] = h0_ref[0].T
        else:
            h_scr[...] = jnp.zeros_like(h_scr)

    ri = lax.broadcasted_iota(jnp.int32, (CHUNK, CHUNK), 0)
    ci = lax.broadcasted_iota(jnp.int32, (CHUNK, CHUNK), 1)
    lower = ri >= ci
    diag = ri == ci
    left = lax.broadcasted_iota(jnp.int32, (CHUNK, LANES), 1) < HEADDIM

    h_b = h_scr[...]
    for u in reversed(range(cps)):
        rows = slice(u * CHUNK, (u + 1) * CHUNK)
        xs_bf = xs_ref[rows, :]
        xs = xs_bf.astype(F32)
        bc_bf = bc_ref[rows, :]

        dt = _softplus(dt_ref[rows, :] + dtb_ref[...])
        a = dt * (-jnp.exp(alog_ref[...]))
        acs = _cumsum_rows(a) * LOG2E
        rcs = _rev_cumsum_rows(a) * LOG2E
        acs_t = acs.T
        rcs_t = rcs.T
        dt_t = dt.T
        lg_t = jnp.log2(dt_t)
        rf_t = acs_t - lg_t
        rb_t = rcs_t - lg_t
        w2b_t = dt_t * jnp.exp2(rcs_t[:, 0:1] - rcs_t)
        cdec = _expand_heads(jnp.broadcast_to(jnp.exp2(rcs[0:1, :]), (8, LANES)), eb_ref)[0:1, :]

        h_f = hprev_ref[u]
        h_b_bf = h_b.astype(BF16)

        y_parts = []
        s_parts = []
        for g in range(GROUPS):
            b_g = bc_bf[:, g * STATE:(g + 1) * STATE]
            c_g = bc_bf[:, BC_W + g * STATE:BC_W + (g + 1) * STATE]
            cb = _dot_nt(c_g, b_g)
            bt = b_g.astype(F32).T
            off_f = _dot(c_g, h_f[:, g * GROUP_W:(g + 1) * GROUP_W])
            off_b = _dot(c_g, h_b_bf[:, g * GROUP_W:(g + 1) * GROUP_W])
            for k in range(HEADS_PER_GROUP // 2):
                ms, bs, cfs, cbs = [], [], [], []
                for h in (g * HEADS_PER_GROUP + 2 * k, g * HEADS_PER_GROUP + 2 * k + 1):
                    hb = HEADS + h
                    cf = acs[:, h:h + 1]
                    cbk = rcs[:, hb:hb + 1]
                    e = jnp.exp2(jnp.where(lower, cf - rf_t[h:h + 1, :], cbk - rb_t[hb:hb + 1, :]))
                    e = e + jnp.where(diag, dt_t[hb:hb + 1, :], 0.0)
                    ms.append((cb * e).astype(BF16))
                    bs.append((bt * w2b_t[hb:hb + 1, :]).astype(BF16))
                    cfs.append(cf)
                    cbs.append(cbk)
                lhs = jnp.concatenate([jnp.concatenate(ms, axis=1), jnp.concatenate(bs, axis=1)], axis=0)
                out = _dot(lhs, _pair_rhs(xs_bf, g * HEADS_PER_GROUP // 2 + k, left))
                ef = jnp.exp2(jnp.where(left, cfs[0], cfs[1]))
                eb = jnp.exp2(jnp.where(left, cbs[0], cbs[1]))
                cols = slice(k * LANES, (k + 1) * LANES)
                y_parts.append(out[:CHUNK, :] + ef * off_f[:, cols] + eb * off_b[:, cols])
                s_parts.append(out[CHUNK:, :])
        y = jnp.concatenate(y_parts, axis=1) + dvec_ref[...] * xs
        y = y * _silu(z_ref[rows, :].astype(F32))
        ms_y = jnp.mean(y * y, axis=-1, keepdims=True)
        y_ref[rows, :] = (y * lax.rsqrt(ms_y + EPS) * nw_ref[...]).astype(BF16)

        h_b = cdec * h_b + jnp.concatenate(s_parts, axis=1)
    h_scr[...] = h_b

    @pl.when(c == nsteps - 1)
    def _():
        hfin_ref[0] = h_b.T


def _ssd_bwd(xs, bc, proj, dt_raw, hprev, h0, consts, nseq, nchunks):
    t = nseq * nchunks * CHUNK
    cps = min(SSD_BWD_CHUNKS_PER_STEP, nchunks)
    nsteps = nchunks // cps
    rows = cps * CHUNK
    gidx = lambda b, c: b * nsteps + (nsteps - 1 - c)
    in_specs = [
        pl.BlockSpec((rows, D_SSD), lambda b, c: (gidx(b, c), 0)),
        pl.BlockSpec((rows, 2 * BC_W), lambda b, c: (gidx(b, c), 0)),
        pl.BlockSpec((rows, D_SSD), lambda b, c: (gidx(b, c), 0)),
        pl.BlockSpec((rows, LANES), lambda b, c: (gidx(b, c), 0)),
        pl.BlockSpec((cps, STATE, D_SSD), lambda b, c: (gidx(b, c), 0, 0)),
        pl.BlockSpec((1, LANES), lambda b, c: (0, 0)),
        pl.BlockSpec((1, LANES), lambda b, c: (0, 0)),
        pl.BlockSpec((1, D_SSD), lambda b, c: (0, 0)),
        pl.BlockSpec((1, D_SSD), lambda b, c: (0, 0)),
        pl.BlockSpec((LANES, D_SSD), lambda b, c: (0, 0)),
    ]
    args = [xs, bc, proj, dt_raw, hprev, consts["dt_bias"], consts["a_log"], consts["d_vec"],
            consts["ssd_norm"], consts["e_bwd"]]
    has_h0 = h0 is not None
    if has_h0:
        in_specs.append(pl.BlockSpec((1, D_SSD, STATE), lambda b, c: (b, 0, 0)))
        args.append(h0)
    return pl.pallas_call(
        functools.partial(_ssd_bwd_kernel, has_h0, cps),
        grid=(nseq, nsteps),
        in_specs=in_specs,
        out_specs=[pl.BlockSpec((rows, D_SSD), lambda b, c: (gidx(b, c), 0)),
                   pl.BlockSpec((1, D_SSD, STATE), lambda b, c: (b, 0, 0))],
        out_shape=[jax.ShapeDtypeStruct((t, D_SSD), BF16),
                   jax.ShapeDtypeStruct((nseq, D_SSD, STATE), F32)],
        scratch_shapes=[pltpu.VMEM((STATE, D_SSD), F32)],
        compiler_params=_cparams(("arbitrary", "arbitrary")),
        name="ssd_bwd",
    )(*args)


def _mix_kernel(y_ref, u_ref, v_ref, ga_ref, gb_ref, x_ref, mod_ref, lnw_ref, lnb_ref, ws_ref, bst_ref,
                wbs_ref, wbg_ref, wout_ref, npost_ref, npre2_ref, o_ref, h2_ref, ysgu_scr):
    tm = x_ref.shape[0]
    v = v_ref[...].astype(F32)
    mu = jnp.mean(v, axis=-1, keepdims=True)
    vc = v - mu
    var = jnp.mean(vc * vc, axis=-1, keepdims=True)
    vn = (vc * lax.rsqrt(var + EPS) * lnw_ref[...] + lnb_ref[...]).astype(BF16)
    bst = bst_ref[...]
    for r in range(tm // CHUNK):
        rows = slice(r * CHUNK, (r + 1) * CHUNK)
        for g in range(SGU_GROUPS):
            cols = slice(g * LANES, (g + 1) * LANES)
            s = _dot(ws_ref[g], vn[rows, cols]) + bst[:, g:g + 1]
            ysgu_scr[rows, cols] = (u_ref[rows, cols].astype(F32) * s).astype(BF16)
    br_ssd = _dot(y_ref[...], wbs_ref[...])
    br_sgu = _dot(ysgu_scr[...], wbg_ref[...])
    merged = _sigmoid(ga_ref[...].astype(F32)) * br_ssd + _sigmoid(gb_ref[...].astype(F32)) * br_sgu
    merged = merged.astype(BF16)
    quarter = tm // 4
    for r in range(4):
        rows = slice(r * quarter, (r + 1) * quarter)
        mix = _dot(merged[rows, :], wout_ref[...])
        ms = jnp.mean(mix * mix, axis=-1, keepdims=True)
        x1 = x_ref[rows, :] + mod_ref[0, 2:3, :] * (mix * lax.rsqrt(ms + EPS) * npost_ref[...])
        o_ref[rows, :] = x1
        h2_ref[rows, :] = _modulated_norm(x1, npre2_ref[...], mod_ref[0, 3:4, :], mod_ref[0, 4:5, :]).astype(BF16)


def _mix(y_ssd, proj, x, mod, consts, tokens_per_mod):
    t = x.shape[0]
    tm = 512
    row = lambda i: (i, 0)
    const2 = lambda i: (0, 0)
    return pl.pallas_call(
        _mix_kernel,
        grid=(t // tm,),
        in_specs=[pl.BlockSpec((tm, D_SSD), row),
                  pl.BlockSpec((tm, D_MODEL), lambda i: (i, 5)),
                  pl.BlockSpec((tm, D_MODEL), lambda i: (i, 6)),
                  pl.BlockSpec((tm, D_MODEL), lambda i: (i, 7)),
                  pl.BlockSpec((tm, D_MODEL), lambda i: (i, 8)),
                  pl.BlockSpec((tm, D_MODEL), row),
                  pl.BlockSpec((1, 6, D_MODEL), lambda i: ((i * tm) // tokens_per_mod, 0, 0)),
                  pl.BlockSpec((1, D_MODEL), const2),
                  pl.BlockSpec((1, D_MODEL), const2),
                  pl.BlockSpec((SGU_GROUPS, CHUNK, CHUNK), lambda i: (0, 0, 0)),
                  pl.BlockSpec((CHUNK, SGU_GROUPS), const2),
                  pl.BlockSpec((D_SSD, D_MODEL), const2),
                  pl.BlockSpec((D_MODEL, D_MODEL), const2),
                  pl.BlockSpec((D_MODEL, D_MODEL), const2),
                  pl.BlockSpec((1, D_MODEL), const2),
                  pl.BlockSpec((1, D_MODEL), const2)],
        out_specs=[pl.BlockSpec((tm, D_MODEL), row), pl.BlockSpec((tm, D_MODEL), row)],
        out_shape=[jax.ShapeDtypeStruct((t, D_MODEL), F32), jax.ShapeDtypeStruct((t, D_MODEL), BF16)],
        scratch_shapes=[pltpu.VMEM((tm, D_MODEL), BF16)],
        compiler_params=_cparams(("arbitrary",)),
        name="mix",
    )(y_ssd, proj, proj, proj, proj, x, mod, consts["sgu_norm_w"], consts["sgu_norm_b"], consts["sgu_w"],
      consts["sgu_bt"], consts["w_branch_ssd"], consts["w_branch_sgu"], consts["w_out"], consts["norm_mix_post"],
      consts["norm_ffn_pre"])


def _matmul_kernel(h_ref, w_ref, o_ref):
    o_ref[...] = _dot(h_ref[...], w_ref[...]).astype(BF16)


def _ffn_up(h2, w_up):
    t = h2.shape[0]
    tm, tn = 1024, D_FF
    return pl.pallas_call(
        _matmul_kernel,
        grid=(t // tm, 2 * D_FF // tn),
        in_specs=[pl.BlockSpec((tm, D_MODEL), lambda i, j: (i, 0)),
                  pl.BlockSpec((D_MODEL, tn), lambda i, j: (0, j))],
        out_specs=pl.BlockSpec((tm, tn), lambda i, j: (i, j)),
        out_shape=jax.ShapeDtypeStruct((t, 2 * D_FF), BF16),
        compiler_params=_cparams(("arbitrary", "arbitrary")),
        name="ffn_up",
    )(h2, w_up)


def _gelu_tanh(x):
    h = 0.5 * x
    u = x * (GELU_C + (GELU_C * 0.044715) * (x * x))
    return h + h * jnp.tanh(u)


def _grid_conv(x_bf, prev_bf, next_bf, w, b, first, last):
    tm = x_bf.shape[0]
    zero = jnp.zeros_like(prev_bf)
    ext_bf = jnp.concatenate([jnp.where(first, zero, prev_bf), x_bf, jnp.where(last, zero, next_bf)], axis=0)
    ext = ext_bf.astype(F32)
    n = ext.shape[0]
    col = lax.broadcasted_iota(jnp.int32, ext.shape, 0) % GRID_W
    shifted = (jnp.where(col == 0, 0.0, pltpu.roll(ext, 1, axis=0)).astype(BF16),
               ext_bf,
               jnp.where(col == GRID_W - 1, 0.0, pltpu.roll(ext, n - 1, axis=0)).astype(BF16))
    wb = w.astype(BF16)
    acc = None
    for dy in range(3):
        for dx in range(3):
            term = wb[3 * dy + dx:3 * dy + dx + 1, :] * shifted[dx][dy * GRID_W:dy * GRID_W + tm, :]
            acc = term if acc is None else acc + term
    return acc + b.astype(BF16)


def _seq_conv(x_bf, w, b, seq_len):
    x = x_bf.astype(F32)
    n = x.shape[0]
    pos = lax.broadcasted_iota(jnp.int32, x.shape, 0) % seq_len
    xm1 = jnp.where(pos == 0, 0.0, pltpu.roll(x, 1, axis=0)).astype(BF16)
    xp1 = jnp.where(pos == seq_len - 1, 0.0, pltpu.roll(x, n - 1, axis=0)).astype(BF16)
    wb = w.astype(BF16)
    return wb[3:4, :] * xm1 + wb[4:5, :] * x_bf + wb[5:6, :] * xp1 + b.astype(BF16)


def _ffn_down_kernel(on_grid, seq_len, tiles_per_seq, *refs):
    if on_grid:
        (up_ref, upp_ref, upn_ref, w_ref, b_ref, wd_ref, x_ref, mod_ref, npost_ref, o_ref, g0_scr, g1_scr, acc_scr) = refs
    else:
        (up_ref, w_ref, b_ref, wd_ref, x_ref, mod_ref, npost_ref, o_ref, g0_scr, g1_scr, acc_scr) = refs
    i = pl.program_id(0)
    first = i % tiles_per_seq == 0
    last = i % tiles_per_seq == tiles_per_seq - 1

    def conv(c0):
        cols = pl.ds(c0, FF_BLK)
        if on_grid:
            return _grid_conv(up_ref[:, cols], upp_ref[:, cols], upn_ref[:, cols], w_ref[:, cols], b_ref[:, cols],
                              first, last)
        return _seq_conv(up_ref[:, cols], w_ref[:, cols], b_ref[:, cols], seq_len)

    def geglu_block(k, dst_ref):
        c0 = pl.multiple_of(k * FF_BLK, FF_BLK)
        a = conv(c0)
        val = conv(pl.multiple_of(c0 + D_FF, FF_BLK))
        dst_ref[...] = (_gelu_tanh(a) * val).astype(BF16)

    def down_block(k, src_ref):
        rows = pl.ds(pl.multiple_of(k * FF_BLK, FF_BLK), FF_BLK)
        acc_scr[...] += _dot(src_ref[...], wd_ref[rows, :])

    acc_scr[...] = jnp.zeros_like(acc_scr)
    geglu_block(0, g0_scr)

    def block_pair(m, carry):
        k = 2 * m + 1
        down_block(k - 1, g0_scr)
        geglu_block(k, g1_scr)
        down_block(k, g1_scr)
        geglu_block(k + 1, g0_scr)
        return carry

    lax.fori_loop(0, (N_FF_BLK - 1) // 2, block_pair, 0)
    down_block(N_FF_BLK - 1, g0_scr)
    f = acc_scr[...]
    ms = jnp.mean(f * f, axis=-1, keepdims=True)
    o_ref[...] = x_ref[...] + mod_ref[0, 5:6, :] * (f * lax.rsqrt(ms + EPS) * npost_ref[...])


def _ffn_down(up, x, mod, consts, tokens_per_mod, on_grid, seq_len):
    t = x.shape[0]
    tm = 512
    tiles_per_seq = max(seq_len // tm, 1)
    rows_per_tile = tm // GRID_W
    n_rows = t // GRID_W
    in_specs = [pl.BlockSpec((tm, 2 * D_FF), lambda i: (i, 0))]
    args = [up]
    if on_grid:
        in_specs += [pl.BlockSpec((GRID_W, 2 * D_FF), lambda i: (jnp.maximum(i * rows_per_tile - 1, 0), 0)),
                     pl.BlockSpec((GRID_W, 2 * D_FF), lambda i: (jnp.minimum((i + 1) * rows_per_tile, n_rows - 1), 0))]
        args += [up, up]
    in_specs += [pl.BlockSpec((9, 2 * D_FF), lambda i: (0, 0)),
                 pl.BlockSpec((1, 2 * D_FF), lambda i: (0, 0)),
                 pl.BlockSpec((D_FF, D_MODEL), lambda i: (0, 0)),
                 pl.BlockSpec((tm, D_MODEL), lambda i: (i, 0)),
                 pl.BlockSpec((1, 6, D_MODEL), lambda i: ((i * tm) // tokens_per_mod, 0, 0)),
                 pl.BlockSpec((1, D_MODEL), lambda i: (0, 0))]
    args += [consts["ffn_conv_w"], consts["ffn_conv_b"], consts["ffn_w_down"], x, mod, consts["norm_ffn_post"]]
    return pl.pallas_call(
        functools.partial(_ffn_down_kernel, on_grid, seq_len, tiles_per_seq),
        grid=(t // tm,),
        in_specs=in_specs,
        out_specs=pl.BlockSpec((tm, D_MODEL), lambda i: (i, 0)),
        out_shape=jax.ShapeDtypeStruct((t, D_MODEL), F32),
        scratch_shapes=[pltpu.VMEM((tm, FF_BLK), BF16), pltpu.VMEM((tm, FF_BLK), BF16),
                        pltpu.VMEM((tm, D_MODEL), F32)],
        compiler_params=_cparams(("arbitrary",)),
        name="ffn_down",
    )(*args)


def _trunk_path(x, mod, h0_f, h0_b, on_grid, consts):
    nseq, seq_len, _ = x.shape
    t = nseq * seq_len
    nchunks = seq_len // CHUNK
    tokens_per_mod = t // mod.shape[0]
    x2d = x.reshape(t, D_MODEL)

    proj, dt_raw = _inproj(x2d, mod, consts["norm_mix_pre"], consts["w_in_main"], consts["w_in_dt"], tokens_per_mod)
    xs, bc, hprev, hf = _ssd_fwd(proj, dt_raw, h0_f, consts, nseq, nchunks)
    y_ssd, hb = _ssd_bwd(xs, bc, proj, dt_raw, hprev, h0_b, consts, nseq, nchunks)
    x1, h2 = _mix(y_ssd, proj, x2d, mod, consts, tokens_per_mod)
    up = _ffn_up(h2, consts["ffn_w_up"])
    x2 = _ffn_down(up, x1, mod, consts, tokens_per_mod, on_grid, seq_len)
    return x2.reshape(nseq, seq_len, D_MODEL), hf, hb


def _head_expansion(offset):
    rows = jnp.arange(LANES)[:, None]
    cols = jnp.arange(D_SSD)[None, :] // HEADDIM
    return (rows == cols + offset).astype(BF16)


def _layer_consts(i, p):
    w_in_main, w_in_dt = _w_in_prep(p["w_in"], i)
    row = lambda v: v.reshape(1, -1).astype(F32)
    pad_lanes = lambda v: jnp.pad(v.reshape(1, -1).astype(F32), ((0, 0), (0, LANES - N_DT)))
    conv_w = p["ssd_conv_w"][i]
    conv_b = p["ssd_conv_b"][i]
    return {
        "norm_mix_pre": row(p["norm_mix_pre"][i]),
        "norm_mix_post": row(p["norm_mix_post"][i]),
        "norm_ffn_pre": row(p["norm_ffn_pre"][i]),
        "norm_ffn_post": row(p["norm_ffn_post"][i]),
        "w_in_main": w_in_main, "w_in_dt": w_in_dt,
        "cw_x": conv_w[:, :D_SSD], "cw_bc": conv_w[:, D_SSD:],
        "cb_x": row(conv_b[:D_SSD]), "cb_bc": row(conv_b[D_SSD:]),
        "dt_bias": pad_lanes(p["ssd_dt_bias"][i]),
        "a_log": pad_lanes(p["ssd_a_log"][i]),
        "d_vec": row(jnp.repeat(p["ssd_d"][i], HEADDIM)),
        "ssd_norm": row(p["ssd_norm"][i]),
        "e_fwd": _head_expansion(0), "e_bwd": _head_expansion(HEADS),
        "sgu_norm_w": row(p["sgu_norm_w"][i]), "sgu_norm_b": row(p["sgu_norm_b"][i]),
        "sgu_w": p["sgu_w"][i].astype(BF16),
        "sgu_bt": jnp.transpose(p["sgu_b"][i]).astype(F32),
        "w_branch_ssd": p["w_branch_ssd"][i].astype(BF16),
        "w_branch_sgu": p["w_branch_sgu"][i].astype(BF16),
        "w_out": p["w_out"][i].astype(BF16),
        "ffn_w_up": p["ffn_w_up"][i].astype(BF16),
        "ffn_conv_w": p["ffn_conv_w"][i].reshape(9, 2 * D_FF).astype(F32),
        "ffn_conv_b": row(p["ffn_conv_b"][i]),
        "ffn_w_down": p["ffn_w_down"][i].astype(BF16),
    }


def kernel(x_prompt, x_sample, state_ssd_fwd, state_ssd_bwd, c, c_ctx, w_mod, b_mod, norm_mix_pre, norm_mix_post, norm_ffn_pre, norm_ffn_post, w_in, ssd_conv_w, ssd_conv_b, ssd_a_log, ssd_dt_bias, ssd_d, ssd_norm, sgu_norm_w, sgu_norm_b, sgu_w, sgu_b, w_branch_ssd, w_branch_sgu, w_out, ffn_w_up, ffn_conv_w, ffn_conv_b, ffn_w_down):
    params = dict(norm_mix_pre=norm_mix_pre, norm_mix_post=norm_mix_post, norm_ffn_pre=norm_ffn_pre,
                  norm_ffn_post=norm_ffn_post, w_in=w_in, ssd_conv_w=ssd_conv_w, ssd_conv_b=ssd_conv_b,
                  ssd_a_log=ssd_a_log, ssd_dt_bias=ssd_dt_bias, ssd_d=ssd_d, ssd_norm=ssd_norm,
                  sgu_norm_w=sgu_norm_w, sgu_norm_b=sgu_norm_b, sgu_w=sgu_w, sgu_b=sgu_b,
                  w_branch_ssd=w_branch_ssd, w_branch_sgu=w_branch_sgu, w_out=w_out, ffn_w_up=ffn_w_up,
                  ffn_conv_w=ffn_conv_w, ffn_conv_b=ffn_conv_b, ffn_w_down=ffn_w_down)
    depth = w_mod.shape[0]
    n_lat = c.shape[0]
    c_rows = jnp.concatenate([c_ctx[None, :], c, jnp.zeros((8 - 1 - n_lat, D_MODEL), F32)], axis=0)
    xp, xs = x_prompt, x_sample
    new_f, new_b = [], []
    for i in range(depth):
        consts = _layer_consts(i, params)
        mod = _mod_vectors(c_rows, w_mod[i], b_mod[i]).reshape(8, 6, D_MODEL)
        xp, hf, hb = _trunk_path(xp, mod[0:1], None, None, False, consts)
        new_f.append(hf.reshape(-1, HEADS, HEADDIM, STATE))
        new_b.append(hb.reshape(-1, HEADS, HEADDIM, STATE))
        xs, _, _ = _trunk_path(xs, mod[1:1 + n_lat],
                               state_ssd_fwd[:, i].reshape(n_lat, D_SSD, STATE),
                               state_ssd_bwd[:, i].reshape(n_lat, D_SSD, STATE), True, consts)
    return (xp, xs, jnp.stack(new_f, axis=1).astype(x_prompt.dtype), jnp.stack(new_b, axis=1).astype(x_prompt.dtype))
```

```python
import functools

import jax
import jax.numpy as jnp
from jax import lax
from jax.experimental import pallas as pl
from jax.experimental.pallas import tpu as pltpu

F32 = jnp.float32
BF16 = jnp.bfloat16

D_MODEL = 1024
GRID_W = 64
EPS = 1e-6
LOG2E = 1.4426950408889634
GELU_C = 0.7978845608028654
D_SSD = 2 * D_MODEL
HEADDIM = 64
HEADS = D_SSD // HEADDIM
STATE = 128
GROUPS = 4
HEADS_PER_GROUP = HEADS // GROUPS
GROUP_W = HEADS_PER_GROUP * HEADDIM
BC_W = GROUPS * STATE
CONV_CH = D_SSD + 2 * BC_W
CHUNK = 128
SSD_FWD_CHUNKS_PER_STEP = 2
SSD_BWD_CHUNKS_PER_STEP = 4
SGU_GROUPS = 8
D_FF = 2816
SPLIT_XBC = D_SSD + CONV_CH
N_DT = 2 * HEADS
LANES = 128
PROJ_COLS = 9216
FF_BLK = 256
N_FF_BLK = D_FF // FF_BLK

VMEM_LIMIT_BYTES = 56 * 1024 * 1024


def _cparams(sem):
    return pltpu.CompilerParams(dimension_semantics=sem, vmem_limit_bytes=VMEM_LIMIT_BYTES)


def _sigmoid(x):
    return 0.5 + 0.5 * jnp.tanh(0.5 * x)


def _silu(x):
    h = 0.5 * x
    return h + h * jnp.tanh(h)


def _dot(a, b):
    return jnp.dot(a, b, preferred_element_type=F32)


def _dot_nt(a, b):
    return lax.dot_general(a, b, (((1,), (1,)), ((), ())), preferred_element_type=F32)


def _mod_kernel(c_ref, w_ref, b_ref, o_ref):
    c = c_ref[...]
    o_ref[...] = jnp.dot(_silu(c), w_ref[...], preferred_element_type=F32,
                         precision=lax.Precision.HIGHEST) + b_ref[...]


def _mod_vectors(c_rows, w_mod, b_mod):
    rows = c_rows.shape[0]
    tn = 1024
    return pl.pallas_call(
        _mod_kernel,
        grid=(6 * D_MODEL // tn,),
        in_specs=[pl.BlockSpec((rows, D_MODEL), lambda j: (0, 0)),
                  pl.BlockSpec((D_MODEL, tn), lambda j: (0, j)),
                  pl.BlockSpec((1, tn), lambda j: (0, j))],
        out_specs=pl.BlockSpec((rows, tn), lambda j: (0, j)),
        out_shape=jax.ShapeDtypeStruct((rows, 6 * D_MODEL), F32),
        compiler_params=_cparams(("arbitrary",)),
        name="mod",
    )(c_rows, w_mod, b_mod.reshape(1, -1))


def _w_in_prep_kernel(wt_ref, main_ref, dt_ref):
    main_ref[:, 0:SPLIT_XBC] = wt_ref[0:SPLIT_XBC, :].T.astype(BF16)
    main_ref[:, SPLIT_XBC:] = wt_ref[SPLIT_XBC + N_DT:, :].T.astype(BF16)
    dt = wt_ref[SPLIT_XBC:SPLIT_XBC + LANES, :].T
    lane = lax.broadcasted_iota(jnp.int32, dt.shape, 1)
    dt_ref[...] = jnp.where(lane < N_DT, dt, 0.0).astype(BF16)


def _w_in_prep(w_in, layer):
    rows = 128
    return pl.pallas_call(
        _w_in_prep_kernel,
        grid=(D_MODEL // rows,),
        in_specs=[pl.BlockSpec((PROJ_COLS + N_DT, rows), lambda r: (0, r))],
        out_specs=[pl.BlockSpec((rows, PROJ_COLS), lambda r: (r, 0)),
                   pl.BlockSpec((rows, LANES), lambda r: (r, 0))],
        out_shape=[jax.ShapeDtypeStruct((D_MODEL, PROJ_COLS), BF16),
                   jax.ShapeDtypeStruct((D_MODEL, LANES), BF16)],
        compiler_params=_cparams(("arbitrary",)),
        name="w_in_prep",
    )(jnp.swapaxes(w_in[layer], 0, 1))


def _modulated_norm(x, nw, shift, scale):
    ms = jnp.mean(x * x, axis=-1, keepdims=True)
    return (x * lax.rsqrt(ms + EPS) * nw) * (1.0 + scale) + shift


def _inproj_kernel(n_col_steps, x0_ref, xn_ref, mod0_ref, modn_ref, nw_ref, w_ref, wdt_ref, o_ref, dt_ref,
                   ha_scr, hb_scr):
    i = pl.program_id(0)
    j = pl.program_id(1)
    part = xn_ref.shape[0] // n_col_steps
    rows = pl.ds(pl.multiple_of(j * part, part), part)

    def norm(x, mod_ref):
        return _modulated_norm(x, nw_ref[...], mod_ref[0, 0:1, :], mod_ref[0, 1:2, :]).astype(BF16)

    @pl.when((i == 0) & (j == 0))
    def _():
        ha_scr[...] = norm(x0_ref[...], mod0_ref)

    def step(cur_scr, nxt_scr):
        nxt_scr[rows, :] = norm(xn_ref[rows, :], modn_ref)
        dt_ref[rows, :] = _dot(cur_scr[rows, :], wdt_ref[...])
        o_ref[...] = _dot(cur_scr[...], w_ref[...]).astype(BF16)

    @pl.when(i % 2 == 0)
    def _():
        step(ha_scr, hb_scr)

    @pl.when(i % 2 == 1)
    def _():
        step(hb_scr, ha_scr)


def _inproj(x, mod, nw, w_main, w_dt, tokens_per_mod):
    t = x.shape[0]
    tm, tn = 1024, PROJ_COLS // 4
    n_tiles = t // tm
    nxt = lambda i: jnp.minimum(i + 1, n_tiles - 1)
    return pl.pallas_call(
        functools.partial(_inproj_kernel, PROJ_COLS // tn),
        grid=(n_tiles, PROJ_COLS // tn),
        in_specs=[pl.BlockSpec((tm, D_MODEL), lambda i, j: (0, 0)),
                  pl.BlockSpec((tm, D_MODEL), lambda i, j: (nxt(i), 0)),
                  pl.BlockSpec((1, 6, D_MODEL), lambda i, j: (0, 0, 0)),
                  pl.BlockSpec((1, 6, D_MODEL), lambda i, j: ((nxt(i) * tm) // tokens_per_mod, 0, 0)),
                  pl.BlockSpec((1, D_MODEL), lambda i, j: (0, 0)),
                  pl.BlockSpec((D_MODEL, tn), lambda i, j: (0, j)),
                  pl.BlockSpec((D_MODEL, LANES), lambda i, j: (0, 0))],
        out_specs=[pl.BlockSpec((tm, tn), lambda i, j: (i, j)),
                   pl.BlockSpec((tm, LANES), lambda i, j: (i, 0))],
        out_shape=[jax.ShapeDtypeStruct((t, PROJ_COLS), BF16),
                   jax.ShapeDtypeStruct((t, LANES), F32)],
        scratch_shapes=[pltpu.VMEM((tm, D_MODEL), BF16), pltpu.VMEM((tm, D_MODEL), BF16)],
        compiler_params=_cparams(("arbitrary", "arbitrary")),
        name="inproj",
    )(x, x, mod, mod, nw, w_main, w_dt)


def _softplus(x):
    return jnp.maximum(x, 0.0) + jnp.log1p(jnp.exp(-jnp.abs(x)))


def _cumsum_rows(a):
    n = a.shape[0]
    rid = lax.broadcasted_iota(jnp.int32, a.shape, 0)
    s = 1
    while s < n:
        a = a + jnp.where(rid >= s, pltpu.roll(a, s, axis=0), 0.0)
        s *= 2
    return a


def _rev_cumsum_rows(a):
    n = a.shape[0]
    rid = lax.broadcasted_iota(jnp.int32, a.shape, 0)
    s = 1
    while s < n:
        a = a + jnp.where(rid < n - s, pltpu.roll(a, n - s, axis=0), 0.0)
        s *= 2
    return a


def _expand_heads(w, e_ref):
    hi = w.astype(BF16)
    lo = (w - hi.astype(F32)).astype(BF16)
    e = e_ref[...]
    return _dot(hi, e) + _dot(lo, e)


def _row_shift_matrix(n):
    r = jnp.arange(n)[:, None]
    c = jnp.arange(n)[None, :]
    return jnp.concatenate([c == r - 1, c == r + 1], axis=0).astype(BF16)


def _conv3_silu(main_ref, prev_ref, next_ref, w_ref, b_ref, shift_ref, first, last):
    x_bf = main_ref[...]
    n = x_bf.shape[0]
    x = x_bf.astype(F32)
    sh = _dot(shift_ref[...], x_bf)
    xm1 = sh[:n, :]
    xp1 = sh[n:, :]
    prow = jnp.where(first, 0.0, prev_ref[...].astype(F32)[-1:, :])
    nrow = jnp.where(last, 0.0, next_ref[...].astype(F32)[0:1, :])
    rid = lax.broadcasted_iota(jnp.int32, (8, x.shape[1]), 0)
    xm1 = jnp.concatenate([jnp.where(rid == 0, prow, xm1[:8, :]), xm1[8:, :]], axis=0)
    xp1 = jnp.concatenate([xp1[:-8, :], jnp.where(rid == 7, nrow, xp1[-8:, :])], axis=0)
    w = w_ref[...]
    y = w[0:1, :] * xm1 + w[1:2, :] * x + w[2:3, :] * xp1 + b_ref[...]
    return _silu(y)


def _dt_and_log_decay(dt_raw, dtb_ref, alog_ref):
    dt = _softplus(dt_raw + dtb_ref[...])
    return dt, dt * (-jnp.exp(alog_ref[...]))


def _pair_rhs(xs_bf, pair, left):
    xp = xs_bf[:, pair * LANES:(pair + 1) * LANES]
    zero = jnp.zeros_like(xp)
    return jnp.concatenate([jnp.where(left, xp, zero), jnp.where(left, zero, xp)], axis=0)


def _ssd_fwd_kernel(has_h0, cps, *refs):
    (xm_ref, bcm_ref, xp_ref, bcp_ref, xn_ref, bcn_ref, dt_ref,
     cwx_ref, cwbc_ref, cbx_ref, cbbc_ref, dtb_ref, alog_ref, ef_ref, s3_ref) = refs[:15]
    pos = 15
    h0_ref = None
    if has_h0:
        h0_ref = refs[pos]
        pos += 1
    xs_ref, bc_ref, hprev_ref, hfin_ref, h_scr = refs[pos:pos + 5]

    c = pl.program_id(1)
    nsteps = pl.num_programs(1)
    first = c == 0
    last = c == nsteps - 1

    @pl.when(first)
    def _():
        if has_h0:
            h_scr[...] = h0_ref[0].T
        else:
            h_scr[...] = jnp.zeros_like(h_scr)

    xs_all = _conv3_silu(xm_ref, xp_ref, xn_ref, cwx_ref, cbx_ref, s3_ref, first, last)
    bc_all = _conv3_silu(bcm_ref, bcp_ref, bcn_ref, cwbc_ref, cbbc_ref, s3_ref, first, last)
    xs_all_bf = xs_all.astype(BF16)
    xs_ref[...] = xs_all_bf
    bc_ref[...] = bc_all.astype(BF16)

    left = lax.broadcasted_iota(jnp.int32, (CHUNK, LANES), 1) < HEADDIM
    h_cur = h_scr[...]
    for u in range(cps):
        rows = slice(u * CHUNK, (u + 1) * CHUNK)
        xs_bf = xs_all_bf[rows, :]
        bc = bc_all[rows, :]
        dt, a = _dt_and_log_decay(dt_ref[rows, :], dtb_ref, alog_ref)
        acs = _cumsum_rows(a)
        acs_t = acs.T
        w2_t = dt.T * jnp.exp(acs_t[:, CHUNK - 1:CHUNK] - acs_t)
        cdec = _expand_heads(jnp.broadcast_to(jnp.exp(acs[CHUNK - 1:CHUNK, :]), (8, LANES)), ef_ref)[0:1, :]

        parts = []
        for g in range(GROUPS):
            bt = bc[:, g * STATE:(g + 1) * STATE].T
            for k in range(HEADS_PER_GROUP // 2):
                h = g * HEADS_PER_GROUP + 2 * k
                lhs = jnp.concatenate([(bt * w2_t[h:h + 1, :]).astype(BF16),
                                       (bt * w2_t[h + 1:h + 2, :]).astype(BF16)], axis=1)
                parts.append(_dot(lhs, _pair_rhs(xs_bf, h // 2, left)))
        s_loc = jnp.concatenate(parts, axis=1)

        hprev_ref[u] = h_cur.astype(BF16)
        h_cur = cdec * h_cur + s_loc
    h_scr[...] = h_cur

    @pl.when(last)
    def _():
        hfin_ref[0] = h_cur.T


def _ssd_fwd(proj, dt_raw, h0, consts, nseq, nchunks):
    t = nseq * nchunks * CHUNK
    cps = min(SSD_FWD_CHUNKS_PER_STEP, nchunks)
    nsteps = nchunks // cps
    rows = cps * CHUNK
    halo = 16
    per = rows // halo
    n_halo = t // halo
    gidx = lambda b, c: b * nsteps + c
    prev_blk = lambda b, c: jnp.maximum(gidx(b, c) * per - 1, 0)
    next_blk = lambda b, c: jnp.minimum((gidx(b, c) + 1) * per, n_halo - 1)
    in_specs = [
        pl.BlockSpec((rows, D_SSD), lambda b, c: (gidx(b, c), 1)),
        pl.BlockSpec((rows, 2 * BC_W), lambda b, c: (gidx(b, c), 4)),
        pl.BlockSpec((halo, D_SSD), lambda b, c: (prev_blk(b, c), 1)),
        pl.BlockSpec((halo, 2 * BC_W), lambda b, c: (prev_blk(b, c), 4)),
        pl.BlockSpec((halo, D_SSD), lambda b, c: (next_blk(b, c), 1)),
        pl.BlockSpec((halo, 2 * BC_W), lambda b, c: (next_blk(b, c), 4)),
        pl.BlockSpec((rows, LANES), lambda b, c: (gidx(b, c), 0)),
        pl.BlockSpec((3, D_SSD), lambda b, c: (0, 0)),
        pl.BlockSpec((3, 2 * BC_W), lambda b, c: (0, 0)),
        pl.BlockSpec((1, D_SSD), lambda b, c: (0, 0)),
        pl.BlockSpec((1, 2 * BC_W), lambda b, c: (0, 0)),
        pl.BlockSpec((1, LANES), lambda b, c: (0, 0)),
        pl.BlockSpec((1, LANES), lambda b, c: (0, 0)),
        pl.BlockSpec((LANES, D_SSD), lambda b, c: (0, 0)),
        pl.BlockSpec((2 * rows, rows), lambda b, c: (0, 0)),
    ]
    args = [proj, proj, proj, proj, proj, proj, dt_raw,
            consts["cw_x"], consts["cw_bc"], consts["cb_x"], consts["cb_bc"],
            consts["dt_bias"], consts["a_log"], consts["e_fwd"], _row_shift_matrix(rows)]
    has_h0 = h0 is not None
    if has_h0:
        in_specs.append(pl.BlockSpec((1, D_SSD, STATE), lambda b, c: (b, 0, 0)))
        args.append(h0)
    return pl.pallas_call(
        functools.partial(_ssd_fwd_kernel, has_h0, cps),
        grid=(nseq, nsteps),
        in_specs=in_specs,
        out_specs=[pl.BlockSpec((rows, D_SSD), lambda b, c: (gidx(b, c), 0)),
                   pl.BlockSpec((rows, 2 * BC_W), lambda b, c: (gidx(b, c), 0)),
                   pl.BlockSpec((cps, STATE, D_SSD), lambda b, c: (gidx(b, c), 0, 0)),
                   pl.BlockSpec((1, D_SSD, STATE), lambda b, c: (b, 0, 0))],
        out_shape=[jax.ShapeDtypeStruct((t, D_SSD), BF16),
                   jax.ShapeDtypeStruct((t, 2 * BC_W), BF16),
                   jax.ShapeDtypeStruct((nseq * nchunks, STATE, D_SSD), BF16),
                   jax.ShapeDtypeStruct((nseq, D_SSD, STATE), F32)],
        scratch_shapes=[pltpu.VMEM((STATE, D_SSD), F32)],
        compiler_params=_cparams(("arbitrary", "arbitrary")),
        name="ssd_fwd",
    )(*args)


def _ssd_bwd_kernel(has_h0, cps, *refs):
    (xs_ref, bc_ref, z_ref, dt_ref, hprev_ref, dtb_ref, alog_ref, dvec_ref, nw_ref,
     eb_ref) = refs[:10]
    pos = 10
    h0_ref = None
    if has_h0:
        h0_ref = refs[pos]
        pos += 1
    y_ref, hfin_ref, h_scr = refs[pos:pos + 3]

    c = pl.program_id(1)
    nsteps = pl.num_programs(1)

    @pl.when(c == 0)
    def _():
        if has_h0:
            h_scr[...] = h0_ref[0].T
        else:
            h_scr[...] = jnp.zeros_like(h_scr)

    ri = lax.broadcasted_iota(jnp.int32, (CHUNK, CHUNK), 0)
    ci = lax.broadcasted_iota(jnp.int32, (CHUNK, CHUNK), 1)
    lower = ri >= ci
    diag = ri == ci
    left = lax.broadcasted_iota(jnp.int32, (CHUNK, LANES), 1) < HEADDIM

    h_b = h_scr[...]
    for u in reversed(range(cps)):
        rows = slice(u * CHUNK, (u + 1) * CHUNK)
        xs_bf = xs_ref[rows, :]
        xs = xs_bf.astype(F32)
        bc_bf = bc_ref[rows, :]

        dt, a = _dt_and_log_decay(dt_ref[rows, :], dtb_ref, alog_ref)
        acs = _cumsum_rows(a) * LOG2E
        rcs = _rev_cumsum_rows(a) * LOG2E
        acs_t = acs.T
        rcs_t = rcs.T
        dt_t = dt.T
        lg_t = jnp.log2(dt_t)
        rf_t = acs_t - lg_t
        rb_t = rcs_t - lg_t
        w2b_t = dt_t * jnp.exp2(rcs_t[:, 0:1] - rcs_t)
        cdec = _expand_heads(jnp.broadcast_to(jnp.exp2(rcs[0:1, :]), (8, LANES)), eb_ref)[0:1, :]

        h_f = hprev_ref[u]
        h_b_bf = h_b.astype(BF16)

        y_parts = []
        s_parts = []
        for g in range(GROUPS):
            b_g = bc_bf[:, g * STATE:(g + 1) * STATE]
            c_g = bc_bf[:, BC_W + g * STATE:BC_W + (g + 1) * STATE]
            cb = _dot_nt(c_g, b_g)
            bt = b_g.astype(F32).T
            off_f = _dot(c_g, h_f[:, g * GROUP_W:(g + 1) * GROUP_W])
            off_b = _dot(c_g, h_b_bf[:, g * GROUP_W:(g + 1) * GROUP_W])
            for k in range(HEADS_PER_GROUP // 2):
                ms, bs, cfs, cbs = [], [], [], []
                for h in (g * HEADS_PER_GROUP + 2 * k, g * HEADS_PER_GROUP + 2 * k + 1):
                    hb = HEADS + h
                    cf = acs[:, h:h + 1]
                    cbk = rcs[:, hb:hb + 1]
                    e = jnp.exp2(jnp.where(lower, cf - rf_t[h:h + 1, :], cbk - rb_t[hb:hb + 1, :]))
                    e = e + jnp.where(diag, dt_t[hb:hb + 1, :], 0.0)
                    ms.append((cb * e).astype(BF16))
                    bs.append((bt * w2b_t[hb:hb + 1, :]).astype(BF16))
                    cfs.append(cf)
                    cbs.append(cbk)
                lhs = jnp.concatenate([jnp.concatenate(ms, axis=1), jnp.concatenate(bs, axis=1)], axis=0)
                out = _dot(lhs, _pair_rhs(xs_bf, g * HEADS_PER_GROUP // 2 + k, left))
                ef = jnp.exp2(jnp.where(left, cfs[0], cfs[1]))
                eb = jnp.exp2(jnp.where(left, cbs[0], cbs[1]))
                cols = slice(k * LANES, (k + 1) * LANES)
                y_parts.append(out[:CHUNK, :] + ef * off_f[:, cols] + eb * off_b[:, cols])
                s_parts.append(out[CHUNK:, :])
        y = jnp.concatenate(y_parts, axis=1) + dvec_ref[...] * xs
        y = y * _silu(z_ref[rows, :].astype(F32))
        ms_y = jnp.mean(y * y, axis=-1, keepdims=True)
        y_ref[rows, :] = (y * lax.rsqrt(ms_y + EPS) * nw_ref[...]).astype(BF16)

        h_b = cdec * h_b + jnp.concatenate(s_parts, axis=1)
    h_scr[...] = h_b

    @pl.when(c == nsteps - 1)
    def _():
        hfin_ref[0] = h_b.T


def _ssd_bwd(xs, bc, proj, dt_raw, hprev, h0, consts, nseq, nchunks):
    t = nseq * nchunks * CHUNK
    cps = min(SSD_BWD_CHUNKS_PER_STEP, nchunks)
    nsteps = nchunks // cps
    rows = cps * CHUNK
    gidx = lambda b, c: b * nsteps + (nsteps - 1 - c)
    in_specs = [
        pl.BlockSpec((rows, D_SSD), lambda b, c: (gidx(b, c), 0)),
        pl.BlockSpec((rows, 2 * BC_W), lambda b, c: (gidx(b, c), 0)),
        pl.BlockSpec((rows, D_SSD), lambda b, c: (gidx(b, c), 0)),
        pl.BlockSpec((rows, LANES), lambda b, c: (gidx(b, c), 0)),
        pl.BlockSpec((cps, STATE, D_SSD), lambda b, c: (gidx(b, c), 0, 0)),
        pl.BlockSpec((1, LANES), lambda b, c: (0, 0)),
        pl.BlockSpec((1, LANES), lambda b, c: (0, 0)),
        pl.BlockSpec((1, D_SSD), lambda b, c: (0, 0)),
        pl.BlockSpec((1, D_SSD), lambda b, c: (0, 0)),
        pl.BlockSpec((LANES, D_SSD), lambda b, c: (0, 0)),
    ]
    args = [xs, bc, proj, dt_raw, hprev, consts["dt_bias"], consts["a_log"], consts["d_vec"],
            consts["ssd_norm"], consts["e_bwd"]]
    has_h0 = h0 is not None
    if has_h0:
        in_specs.append(pl.BlockSpec((1, D_SSD, STATE), lambda b, c: (b, 0, 0)))
        args.append(h0)
    return pl.pallas_call(
        functools.partial(_ssd_bwd_kernel, has_h0, cps),
        grid=(nseq, nsteps),
        in_specs=in_specs,
        out_specs=[pl.BlockSpec((rows, D_SSD), lambda b, c: (gidx(b, c), 0)),
                   pl.BlockSpec((1, D_SSD, STATE), lambda b, c: (b, 0, 0))],
        out_shape=[jax.ShapeDtypeStruct((t, D_SSD), BF16),
                   jax.ShapeDtypeStruct((nseq, D_SSD, STATE), F32)],
        scratch_shapes=[pltpu.VMEM((STATE, D_SSD), F32)],
        compiler_params=_cparams(("arbitrary", "arbitrary")),
        name="ssd_bwd",
    )(*args)


def _mix_kernel(y_ref, u_ref, v_ref, ga_ref, gb_ref, x_ref, mod_ref, lnw_ref, lnb_ref, ws_ref, bst_ref,
                wbs_ref, wbg_ref, wout_ref, npost_ref, npre2_ref, o_ref, h2_ref, ysgu_scr):
    tm = x_ref.shape[0]
    v = v_ref[...].astype(F32)
    mu = jnp.mean(v, axis=-1, keepdims=True)
    vc = v - mu
    var = jnp.mean(vc * vc, axis=-1, keepdims=True)
    vn = (vc * lax.rsqrt(var + EPS) * lnw_ref[...] + lnb_ref[...]).astype(BF16)
    bst = bst_ref[...]
    for r in range(tm // CHUNK):
        rows = slice(r * CHUNK, (r + 1) * CHUNK)
        for g in range(SGU_GROUPS):
            cols = slice(g * LANES, (g + 1) * LANES)
            s = _dot(ws_ref[g], vn[rows, cols]) + bst[:, g:g + 1]
            ysgu_scr[rows, cols] = (u_ref[rows, cols].astype(F32) * s).astype(BF16)
    br_ssd = _dot(y_ref[...], wbs_ref[...])
    br_sgu = _dot(ysgu_scr[...], wbg_ref[...])
    merged = _sigmoid(ga_ref[...].astype(F32)) * br_ssd + _sigmoid(gb_ref[...].astype(F32)) * br_sgu
    merged = merged.astype(BF16)
    quarter = tm // 4
    for r in range(4):
        rows = slice(r * quarter, (r + 1) * quarter)
        mix = _dot(merged[rows, :], wout_ref[...])
        ms = jnp.mean(mix * mix, axis=-1, keepdims=True)
        x1 = x_ref[rows, :] + mod_ref[0, 2:3, :] * (mix * lax.rsqrt(ms + EPS) * npost_ref[...])
        o_ref[rows, :] = x1
        h2_ref[rows, :] = _modulated_norm(x1, npre2_ref[...], mod_ref[0, 3:4, :], mod_ref[0, 4:5, :]).astype(BF16)


def _mix(y_ssd, proj, x, mod, consts, tokens_per_mod):
    t = x.shape[0]
    tm = 512
    row = lambda i: (i, 0)
    const2 = lambda i: (0, 0)
    return pl.pallas_call(
        _mix_kernel,
        grid=(t // tm,),
        in_specs=[pl.BlockSpec((tm, D_SSD), row),
                  pl.BlockSpec((tm, D_MODEL), lambda i: (i, 5)),
                  pl.BlockSpec((tm, D_MODEL), lambda i: (i, 6)),
                  pl.BlockSpec((tm, D_MODEL), lambda i: (i, 7)),
                  pl.BlockSpec((tm, D_MODEL), lambda i: (i, 8)),
                  pl.BlockSpec((tm, D_MODEL), row),
                  pl.BlockSpec((1, 6, D_MODEL), lambda i: ((i * tm) // tokens_per_mod, 0, 0)),
                  pl.BlockSpec((1, D_MODEL), const2),
                  pl.BlockSpec((1, D_MODEL), const2),
                  pl.BlockSpec((SGU_GROUPS, CHUNK, CHUNK), lambda i: (0, 0, 0)),
                  pl.BlockSpec((CHUNK, SGU_GROUPS), const2),
                  pl.BlockSpec((D_SSD, D_MODEL), const2),
                  pl.BlockSpec((D_MODEL, D_MODEL), const2),
                  pl.BlockSpec((D_MODEL, D_MODEL), const2),
                  pl.BlockSpec((1, D_MODEL), const2),
                  pl.BlockSpec((1, D_MODEL), const2)],
        out_specs=[pl.BlockSpec((tm, D_MODEL), row), pl.BlockSpec((tm, D_MODEL), row)],
        out_shape=[jax.ShapeDtypeStruct((t, D_MODEL), F32), jax.ShapeDtypeStruct((t, D_MODEL), BF16)],
        scratch_shapes=[pltpu.VMEM((tm, D_MODEL), BF16)],
        compiler_params=_cparams(("arbitrary",)),
        name="mix",
    )(y_ssd, proj, proj, proj, proj, x, mod, consts["sgu_norm_w"], consts["sgu_norm_b"], consts["sgu_w"],
      consts["sgu_bt"], consts["w_branch_ssd"], consts["w_branch_sgu"], consts["w_out"], consts["norm_mix_post"],
      consts["norm_ffn_pre"])


def _matmul_kernel(h_ref, w_ref, o_ref):
    o_ref[...] = _dot(h_ref[...], w_ref[...]).astype(BF16)


def _ffn_up(h2, w_up):
    t = h2.shape[0]
    tm, tn = 1024, D_FF
    return pl.pallas_call(
        _matmul_kernel,
        grid=(t // tm, 2 * D_FF // tn),
        in_specs=[pl.BlockSpec((tm, D_MODEL), lambda i, j: (i, 0)),
                  pl.BlockSpec((D_MODEL, tn), lambda i, j: (0, j))],
        out_specs=pl.BlockSpec((tm, tn), lambda i, j: (i, j)),
        out_shape=jax.ShapeDtypeStruct((t, 2 * D_FF), BF16),
        compiler_params=_cparams(("arbitrary", "arbitrary")),
        name="ffn_up",
    )(h2, w_up)


def _gelu_tanh(x):
    h = 0.5 * x
    u = x * (GELU_C + (GELU_C * 0.044715) * (x * x))
    return h + h * jnp.tanh(u)


def _grid_conv(x_bf, prev_bf, next_bf, w, b, first, last):
    tm = x_bf.shape[0]
    zero = jnp.zeros_like(prev_bf)
    ext_bf = jnp.concatenate([jnp.where(first, zero, prev_bf), x_bf, jnp.where(last, zero, next_bf)], axis=0)
    ext = ext_bf.astype(F32)
    n = ext.shape[0]
    col = lax.broadcasted_iota(jnp.int32, ext.shape, 0) % GRID_W
    shifted = (jnp.where(col == 0, 0.0, pltpu.roll(ext, 1, axis=0)).astype(BF16),
               ext_bf,
               jnp.where(col == GRID_W - 1, 0.0, pltpu.roll(ext, n - 1, axis=0)).astype(BF16))
    wb = w.astype(BF16)
    acc = None
    for dy in range(3):
        for dx in range(3):
            term = wb[3 * dy + dx:3 * dy + dx + 1, :] * shifted[dx][dy * GRID_W:dy * GRID_W + tm, :]
            acc = term if acc is None else acc + term
    return acc + b.astype(BF16)


def _seq_conv(x_bf, w, b, seq_len):
    x = x_bf.astype(F32)
    n = x.shape[0]
    pos = lax.broadcasted_iota(jnp.int32, x.shape, 0) % seq_len
    xm1 = jnp.where(pos == 0, 0.0, pltpu.roll(x, 1, axis=0)).astype(BF16)
    xp1 = jnp.where(pos == seq_len - 1, 0.0, pltpu.roll(x, n - 1, axis=0)).astype(BF16)
    wb = w.astype(BF16)
    return wb[3:4, :] * xm1 + wb[4:5, :] * x_bf + wb[5:6, :] * xp1 + b.astype(BF16)


def _ffn_down_kernel(on_grid, seq_len, tiles_per_seq, *refs):
    if on_grid:
        (up_ref, upp_ref, upn_ref, w_ref, b_ref, wd_ref, x_ref, mod_ref, npost_ref, o_ref,
         g0_scr, g1_scr, acc_scr) = refs
    else:
        (up_ref, w_ref, b_ref, wd_ref, x_ref, mod_ref, npost_ref, o_ref, g0_scr, g1_scr, acc_scr) = refs
    i = pl.program_id(0)
    first = i % tiles_per_seq == 0
    last = i % tiles_per_seq == tiles_per_seq - 1

    def conv(c0):
        cols = pl.ds(c0, FF_BLK)
        if on_grid:
            return _grid_conv(up_ref[:, cols], upp_ref[:, cols], upn_ref[:, cols], w_ref[:, cols], b_ref[:, cols],
                              first, last)
        return _seq_conv(up_ref[:, cols], w_ref[:, cols], b_ref[:, cols], seq_len)

    def geglu_block(k, dst_ref):
        c0 = pl.multiple_of(k * FF_BLK, FF_BLK)
        a = conv(c0)
        val = conv(pl.multiple_of(c0 + D_FF, FF_BLK))
        dst_ref[...] = (_gelu_tanh(a) * val).astype(BF16)

    def down_block(k, src_ref):
        rows = pl.ds(pl.multiple_of(k * FF_BLK, FF_BLK), FF_BLK)
        acc_scr[...] += _dot(src_ref[...], wd_ref[rows, :])

    acc_scr[...] = jnp.zeros_like(acc_scr)
    geglu_block(0, g0_scr)

    def block_pair(m, carry):
        k = 2 * m + 1
        down_block(k - 1, g0_scr)
        geglu_block(k, g1_scr)
        down_block(k, g1_scr)
        geglu_block(k + 1, g0_scr)
        return carry

    lax.fori_loop(0, (N_FF_BLK - 1) // 2, block_pair, 0)
    down_block(N_FF_BLK - 1, g0_scr)
    f = acc_scr[...]
    ms = jnp.mean(f * f, axis=-1, keepdims=True)
    o_ref[...] = x_ref[...] + mod_ref[0, 5:6, :] * (f * lax.rsqrt(ms + EPS) * npost_ref[...])


def _ffn_down(up, x, mod, consts, tokens_per_mod, on_grid, seq_len):
    t = x.shape[0]
    tm = 512
    tiles_per_seq = max(seq_len // tm, 1)
    rows_per_tile = tm // GRID_W
    n_rows = t // GRID_W
    in_specs = [pl.BlockSpec((tm, 2 * D_FF), lambda i: (i, 0))]
    args = [up]
    if on_grid:
        in_specs += [pl.BlockSpec((GRID_W, 2 * D_FF), lambda i: (jnp.maximum(i * rows_per_tile - 1, 0), 0)),
                     pl.BlockSpec((GRID_W, 2 * D_FF), lambda i: (jnp.minimum((i + 1) * rows_per_tile, n_rows - 1), 0))]
        args += [up, up]
    in_specs += [pl.BlockSpec((9, 2 * D_FF), lambda i: (0, 0)),
                 pl.BlockSpec((1, 2 * D_FF), lambda i: (0, 0)),
                 pl.BlockSpec((D_FF, D_MODEL), lambda i: (0, 0)),
                 pl.BlockSpec((tm, D_MODEL), lambda i: (i, 0)),
                 pl.BlockSpec((1, 6, D_MODEL), lambda i: ((i * tm) // tokens_per_mod, 0, 0)),
                 pl.BlockSpec((1, D_MODEL), lambda i: (0, 0))]
    args += [consts["ffn_conv_w"], consts["ffn_conv_b"], consts["ffn_w_down"], x, mod, consts["norm_ffn_post"]]
    return pl.pallas_call(
        functools.partial(_ffn_down_kernel, on_grid, seq_len, tiles_per_seq),
        grid=(t // tm,),
        in_specs=in_specs,
        out_specs=pl.BlockSpec((tm, D_MODEL), lambda i: (i, 0)),
        out_shape=jax.ShapeDtypeStruct((t, D_MODEL), F32),
        scratch_shapes=[pltpu.VMEM((tm, FF_BLK), BF16), pltpu.VMEM((tm, FF_BLK), BF16),
                        pltpu.VMEM((tm, D_MODEL), F32)],
        compiler_params=_cparams(("arbitrary",)),
        name="ffn_down",
    )(*args)


def _trunk_path(x, mod, h0_f, h0_b, on_grid, consts):
    nseq, seq_len, _ = x.shape
    t = nseq * seq_len
    nchunks = seq_len // CHUNK
    tokens_per_mod = t // mod.shape[0]
    x2d = x.reshape(t, D_MODEL)

    proj, dt_raw = _inproj(x2d, mod, consts["norm_mix_pre"], consts["w_in_main"], consts["w_in_dt"], tokens_per_mod)
    xs, bc, hprev, hf = _ssd_fwd(proj, dt_raw, h0_f, consts, nseq, nchunks)
    y_ssd, hb = _ssd_bwd(xs, bc, proj, dt_raw, hprev, h0_b, consts, nseq, nchunks)
    x1, h2 = _mix(y_ssd, proj, x2d, mod, consts, tokens_per_mod)
    up = _ffn_up(h2, consts["ffn_w_up"])
    x2 = _ffn_down(up, x1, mod, consts, tokens_per_mod, on_grid, seq_len)
    return x2.reshape(nseq, seq_len, D_MODEL), hf, hb


def _head_expansion(offset):
    rows = jnp.arange(LANES)[:, None]
    cols = jnp.arange(D_SSD)[None, :] // HEADDIM
    return (rows == cols + offset).astype(BF16)


def _layer_consts(i, p):
    w_in_main, w_in_dt = _w_in_prep(p["w_in"], i)
    row = lambda v: v.reshape(1, -1).astype(F32)
    pad_lanes = lambda v: jnp.pad(v.reshape(1, -1).astype(F32), ((0, 0), (0, LANES - N_DT)))
    conv_w = p["ssd_conv_w"][i]
    conv_b = p["ssd_conv_b"][i]
    return {
        "norm_mix_pre": row(p["norm_mix_pre"][i]),
        "norm_mix_post": row(p["norm_mix_post"][i]),
        "norm_ffn_pre": row(p["norm_ffn_pre"][i]),
        "norm_ffn_post": row(p["norm_ffn_post"][i]),
        "w_in_main": w_in_main, "w_in_dt": w_in_dt,
        "cw_x": conv_w[:, :D_SSD], "cw_bc": conv_w[:, D_SSD:],
        "cb_x": row(conv_b[:D_SSD]), "cb_bc": row(conv_b[D_SSD:]),
        "dt_bias": pad_lanes(p["ssd_dt_bias"][i]),
        "a_log": pad_lanes(p["ssd_a_log"][i]),
        "d_vec": row(jnp.repeat(p["ssd_d"][i], HEADDIM)),
        "ssd_norm": row(p["ssd_norm"][i]),
        "e_fwd": _head_expansion(0), "e_bwd": _head_expansion(HEADS),
        "sgu_norm_w": row(p["sgu_norm_w"][i]), "sgu_norm_b": row(p["sgu_norm_b"][i]),
        "sgu_w": p["sgu_w"][i].astype(BF16),
        "sgu_bt": jnp.transpose(p["sgu_b"][i]).astype(F32),
        "w_branch_ssd": p["w_branch_ssd"][i].astype(BF16),
        "w_branch_sgu": p["w_branch_sgu"][i].astype(BF16),
        "w_out": p["w_out"][i].astype(BF16),
        "ffn_w_up": p["ffn_w_up"][i].astype(BF16),
        "ffn_conv_w": p["ffn_conv_w"][i].reshape(9, 2 * D_FF).astype(F32),
        "ffn_conv_b": row(p["ffn_conv_b"][i]),
        "ffn_w_down": p["ffn_w_down"][i].astype(BF16),
    }


def kernel(x_prompt, x_sample, state_ssd_fwd, state_ssd_bwd, c, c_ctx, w_mod, b_mod, norm_mix_pre, norm_mix_post, norm_ffn_pre, norm_ffn_post, w_in, ssd_conv_w, ssd_conv_b, ssd_a_log, ssd_dt_bias, ssd_d, ssd_norm, sgu_norm_w, sgu_norm_b, sgu_w, sgu_b, w_branch_ssd, w_branch_sgu, w_out, ffn_w_up, ffn_conv_w, ffn_conv_b, ffn_w_down):
    params = dict(norm_mix_pre=norm_mix_pre, norm_mix_post=norm_mix_post, norm_ffn_pre=norm_ffn_pre,
                  norm_ffn_post=norm_ffn_post, w_in=w_in, ssd_conv_w=ssd_conv_w, ssd_conv_b=ssd_conv_b,
                  ssd_a_log=ssd_a_log, ssd_dt_bias=ssd_dt_bias, ssd_d=ssd_d, ssd_norm=ssd_norm,
                  sgu_norm_w=sgu_norm_w, sgu_norm_b=sgu_norm_b, sgu_w=sgu_w, sgu_b=sgu_b,
                  w_branch_ssd=w_branch_ssd, w_branch_sgu=w_branch_sgu, w_out=w_out, ffn_w_up=ffn_w_up,
                  ffn_conv_w=ffn_conv_w, ffn_conv_b=ffn_conv_b, ffn_w_down=ffn_w_down)
    depth = w_mod.shape[0]
    n_lat = c.shape[0]
    c_rows = jnp.concatenate([c_ctx[None, :], c, jnp.zeros((8 - 1 - n_lat, D_MODEL), F32)], axis=0)
    xp, xs = x_prompt, x_sample
    new_f, new_b = [], []
    for i in range(depth):
        consts = _layer_consts(i, params)
        mod = _mod_vectors(c_rows, w_mod[i], b_mod[i]).reshape(8, 6, D_MODEL)
        xp, hf, hb = _trunk_path(xp, mod[0:1], None, None, False, consts)
        new_f.append(hf.reshape(-1, HEADS, HEADDIM, STATE))
        new_b.append(hb.reshape(-1, HEADS, HEADDIM, STATE))
        xs, _, _ = _trunk_path(xs, mod[1:1 + n_lat],
                               state_ssd_fwd[:, i].reshape(n_lat, D_SSD, STATE),
                               state_ssd_bwd[:, i].reshape(n_lat, D_SSD, STATE), True, consts)
    return (xp, xs, jnp.stack(new_f, axis=1).astype(x_prompt.dtype), jnp.stack(new_b, axis=1).astype(x_prompt.dtype))
```

```python
import functools

import jax
import jax.numpy as jnp
from jax import lax
from jax.experimental import pallas as pl
from jax.experimental.pallas import tpu as pltpu

F32 = jnp.float32
BF16 = jnp.bfloat16

D_MODEL = 1024
GRID_W = 64
EPS = 1e-6
LOG2E = 1.4426950408889634
GELU_C = 0.7978845608028654
D_SSD = 2 * D_MODEL
HEADDIM = 64
HEADS = D_SSD // HEADDIM
STATE = 128
GROUPS = 4
HEADS_PER_GROUP = HEADS // GROUPS
GROUP_W = HEADS_PER_GROUP * HEADDIM
BC_W = GROUPS * STATE
CONV_CH = D_SSD + 2 * BC_W
CHUNK = 128
SSD_FWD_CHUNKS_PER_STEP = 2
SSD_BWD_CHUNKS_PER_STEP = 4
SGU_GROUPS = 8
D_FF = 2816
SPLIT_XBC = D_SSD + CONV_CH
N_DT = 2 * HEADS
LANES = 128
PROJ_COLS = 9216
FF_BLK = 256
N_FF_BLK = D_FF // FF_BLK

VMEM_LIMIT_BYTES = 56 * 1024 * 1024


def _cparams(sem):
    return pltpu.CompilerParams(dimension_semantics=sem, vmem_limit_bytes=VMEM_LIMIT_BYTES)


def _sigmoid(x):
    return 0.5 + 0.5 * jnp.tanh(0.5 * x)


def _silu(x):
    h = 0.5 * x
    return h + h * jnp.tanh(h)


def _dot(a, b):
    return jnp.dot(a, b, preferred_element_type=F32)


def _dot_nt(a, b):
    return lax.dot_general(a, b, (((1,), (1,)), ((), ())), preferred_element_type=F32)


def _mod_kernel(c_ref, w_ref, b_ref, o_ref):
    s = _silu(c_ref[...])
    w = w_ref[...]
    s_hi = s.astype(BF16)
    s_lo = (s - s_hi.astype(F32)).astype(BF16)
    w_hi = w.astype(BF16)
    w_lo = (w - w_hi.astype(F32)).astype(BF16)
    o_ref[...] = _dot(s_hi, w_hi) + (_dot(s_lo, w_hi) + _dot(s_hi, w_lo)) + b_ref[...]


def _mod_vectors(c_rows, w_mod, b_mod):
    rows = c_rows.shape[0]
    tn = 1024
    return pl.pallas_call(
        _mod_kernel,
        grid=(6 * D_MODEL // tn,),
        in_specs=[pl.BlockSpec((rows, D_MODEL), lambda j: (0, 0)),
                  pl.BlockSpec((D_MODEL, tn), lambda j: (0, j)),
                  pl.BlockSpec((1, tn), lambda j: (0, j))],
        out_specs=pl.BlockSpec((rows, tn), lambda j: (0, j)),
        out_shape=jax.ShapeDtypeStruct((rows, 6 * D_MODEL), F32),
        compiler_params=_cparams(("arbitrary",)),
        name="mod",
    )(c_rows, w_mod, b_mod.reshape(1, -1))


def _w_in_prep_kernel(wt_ref, main_ref, dt_ref):
    main_ref[:, 0:SPLIT_XBC] = wt_ref[0:SPLIT_XBC, :].T.astype(BF16)
    main_ref[:, SPLIT_XBC:] = wt_ref[SPLIT_XBC + N_DT:, :].T.astype(BF16)
    dt = wt_ref[SPLIT_XBC:SPLIT_XBC + LANES, :].T
    lane = lax.broadcasted_iota(jnp.int32, dt.shape, 1)
    dt_ref[...] = jnp.where(lane < N_DT, dt, 0.0).astype(BF16)


def _w_in_prep(w_in, layer):
    rows = 128
    return pl.pallas_call(
        _w_in_prep_kernel,
        grid=(D_MODEL // rows,),
        in_specs=[pl.BlockSpec((PROJ_COLS + N_DT, rows), lambda r: (0, r))],
        out_specs=[pl.BlockSpec((rows, PROJ_COLS), lambda r: (r, 0)),
                   pl.BlockSpec((rows, LANES), lambda r: (r, 0))],
        out_shape=[jax.ShapeDtypeStruct((D_MODEL, PROJ_COLS), BF16),
                   jax.ShapeDtypeStruct((D_MODEL, LANES), BF16)],
        compiler_params=_cparams(("arbitrary",)),
        name="w_in_prep",
    )(jnp.swapaxes(w_in[layer], 0, 1))


def _modulated_norm(x, nw, shift, scale):
    ms = jnp.mean(x * x, axis=-1, keepdims=True)
    return (x * lax.rsqrt(ms + EPS) * nw) * (1.0 + scale) + shift


def _inproj_kernel(n_col_steps, x0_ref, xn_ref, mod0_ref, modn_ref, nw_ref, w_ref, wdt_ref, o_ref, dt_ref,
                   ha_scr, hb_scr):
    i = pl.program_id(0)
    j = pl.program_id(1)
    part = xn_ref.shape[0] // n_col_steps
    rows = pl.ds(pl.multiple_of(j * part, part), part)

    def norm(x, mod_ref):
        return _modulated_norm(x, nw_ref[...], mod_ref[0, 0:1, :], mod_ref[0, 1:2, :]).astype(BF16)

    @pl.when((i == 0) & (j == 0))
    def _():
        ha_scr[...] = norm(x0_ref[...], mod0_ref)

    def step(cur_scr, nxt_scr):
        nxt_scr[rows, :] = norm(xn_ref[rows, :], modn_ref)
        dt_ref[rows, :] = _dot(cur_scr[rows, :], wdt_ref[...])
        o_ref[...] = _dot(cur_scr[...], w_ref[...]).astype(BF16)

    @pl.when(i % 2 == 0)
    def _():
        step(ha_scr, hb_scr)

    @pl.when(i % 2 == 1)
    def _():
        step(hb_scr, ha_scr)


def _inproj(x, mod, nw, w_main, w_dt, tokens_per_mod):
    t = x.shape[0]
    tm, tn = 1024, PROJ_COLS // 4
    n_tiles = t // tm
    nxt = lambda i: jnp.minimum(i + 1, n_tiles - 1)
    return pl.pallas_call(
        functools.partial(_inproj_kernel, PROJ_COLS // tn),
        grid=(n_tiles, PROJ_COLS // tn),
        in_specs=[pl.BlockSpec((tm, D_MODEL), lambda i, j: (0, 0)),
                  pl.BlockSpec((tm, D_MODEL), lambda i, j: (nxt(i), 0)),
                  pl.BlockSpec((1, 6, D_MODEL), lambda i, j: (0, 0, 0)),
                  pl.BlockSpec((1, 6, D_MODEL), lambda i, j: ((nxt(i) * tm) // tokens_per_mod, 0, 0)),
                  pl.BlockSpec((1, D_MODEL), lambda i, j: (0, 0)),
                  pl.BlockSpec((D_MODEL, tn), lambda i, j: (0, j)),
                  pl.BlockSpec((D_MODEL, LANES), lambda i, j: (0, 0))],
        out_specs=[pl.BlockSpec((tm, tn), lambda i, j: (i, j)),
                   pl.BlockSpec((tm, LANES), lambda i, j: (i, 0))],
        out_shape=[jax.ShapeDtypeStruct((t, PROJ_COLS), BF16),
                   jax.ShapeDtypeStruct((t, LANES), F32)],
        scratch_shapes=[pltpu.VMEM((tm, D_MODEL), BF16), pltpu.VMEM((tm, D_MODEL), BF16)],
        compiler_params=_cparams(("arbitrary", "arbitrary")),
        name="inproj",
    )(x, x, mod, mod, nw, w_main, w_dt)


def _softplus(x):
    return jnp.maximum(x, 0.0) + jnp.log1p(jnp.exp(-jnp.abs(x)))


def _cumsum_rows(a):
    n = a.shape[0]
    rid = lax.broadcasted_iota(jnp.int32, a.shape, 0)
    s = 1
    while s < n:
        a = a + jnp.where(rid >= s, pltpu.roll(a, s, axis=0), 0.0)
        s *= 2
    return a


def _rev_cumsum_rows(a):
    n = a.shape[0]
    rid = lax.broadcasted_iota(jnp.int32, a.shape, 0)
    s = 1
    while s < n:
        a = a + jnp.where(rid < n - s, pltpu.roll(a, n - s, axis=0), 0.0)
        s *= 2
    return a


def _expand_heads(w, e_ref):
    hi = w.astype(BF16)
    lo = (w - hi.astype(F32)).astype(BF16)
    e = e_ref[...]
    return _dot(hi, e) + _dot(lo, e)


def _row_shift_matrix(n):
    r = jnp.arange(n)[:, None]
    c = jnp.arange(n)[None, :]
    return jnp.concatenate([c == r - 1, c == r + 1], axis=0).astype(BF16)


def _conv3_silu(main_ref, prev_ref, next_ref, w_ref, b_ref, shift_ref, first, last):
    x_bf = main_ref[...]
    n = x_bf.shape[0]
    x = x_bf.astype(F32)
    sh = _dot(shift_ref[...], x_bf)
    xm1 = sh[:n, :]
    xp1 = sh[n:, :]
    prow = jnp.where(first, 0.0, prev_ref[...].astype(F32)[-1:, :])
    nrow = jnp.where(last, 0.0, next_ref[...].astype(F32)[0:1, :])
    rid = lax.broadcasted_iota(jnp.int32, (8, x.shape[1]), 0)
    xm1 = jnp.concatenate([jnp.where(rid == 0, prow, xm1[:8, :]), xm1[8:, :]], axis=0)
    xp1 = jnp.concatenate([xp1[:-8, :], jnp.where(rid == 7, nrow, xp1[-8:, :])], axis=0)
    w = w_ref[...]
    y = w[0:1, :] * xm1 + w[1:2, :] * x + w[2:3, :] * xp1 + b_ref[...]
    return _silu(y)


def _dt_and_log_decay(dt_raw, dtb_ref, alog_ref):
    dt = _softplus(dt_raw + dtb_ref[...])
    return dt, dt * (-jnp.exp(alog_ref[...]))


def _pair_rhs(xs_bf, pair, left):
    xp = xs_bf[:, pair * LANES:(pair + 1) * LANES]
    zero = jnp.zeros_like(xp)
    return jnp.concatenate([jnp.where(left, xp, zero), jnp.where(left, zero, xp)], axis=0)


def _ssd_fwd_kernel(has_h0, cps, *refs):
    (xm_ref, bcm_ref, xp_ref, bcp_ref, xn_ref, bcn_ref, dt_ref,
     cwx_ref, cwbc_ref, cbx_ref, cbbc_ref, dtb_ref, alog_ref, ef_ref, s3_ref) = refs[:15]
    pos = 15
    h0_ref = None
    if has_h0:
        h0_ref = refs[pos]
        pos += 1
    xs_ref, bc_ref, hprev_ref, hfin_ref, h_scr = refs[pos:pos + 5]

    c = pl.program_id(1)
    nsteps = pl.num_programs(1)
    first = c == 0
    last = c == nsteps - 1

    @pl.when(first)
    def _():
        if has_h0:
            h_scr[...] = h0_ref[0].T
        else:
            h_scr[...] = jnp.zeros_like(h_scr)

    xs_all = _conv3_silu(xm_ref, xp_ref, xn_ref, cwx_ref, cbx_ref, s3_ref, first, last)
    bc_all = _conv3_silu(bcm_ref, bcp_ref, bcn_ref, cwbc_ref, cbbc_ref, s3_ref, first, last)
    xs_all_bf = xs_all.astype(BF16)
    xs_ref[...] = xs_all_bf
    bc_ref[...] = bc_all.astype(BF16)

    left = lax.broadcasted_iota(jnp.int32, (CHUNK, LANES), 1) < HEADDIM
    h_cur = h_scr[...]
    for u in range(cps):
        rows = slice(u * CHUNK, (u + 1) * CHUNK)
        xs_bf = xs_all_bf[rows, :]
        bc = bc_all[rows, :]
        dt, a = _dt_and_log_decay(dt_ref[rows, :], dtb_ref, alog_ref)
        acs = _cumsum_rows(a)
        acs_t = acs.T
        w2_t = dt.T * jnp.exp(acs_t[:, CHUNK - 1:CHUNK] - acs_t)
        cdec = _expand_heads(jnp.broadcast_to(jnp.exp(acs[CHUNK - 1:CHUNK, :]), (8, LANES)), ef_ref)[0:1, :]

        parts = []
        for g in range(GROUPS):
            bt = bc[:, g * STATE:(g + 1) * STATE].T
            for k in range(HEADS_PER_GROUP // 2):
                h = g * HEADS_PER_GROUP + 2 * k
                lhs = jnp.concatenate([(bt * w2_t[h:h + 1, :]).astype(BF16),
                                       (bt * w2_t[h + 1:h + 2, :]).astype(BF16)], axis=1)
                parts.append(_dot(lhs, _pair_rhs(xs_bf, h // 2, left)))
        s_loc = jnp.concatenate(parts, axis=1)

        hprev_ref[u] = h_cur.astype(BF16)
        h_cur = cdec * h_cur + s_loc
    h_scr[...] = h_cur

    @pl.when(last)
    def _():
        hfin_ref[0] = h_cur.T


def _ssd_fwd(proj, dt_raw, h0, consts, nseq, nchunks):
    t = nseq * nchunks * CHUNK
    cps = min(SSD_FWD_CHUNKS_PER_STEP, nchunks)
    nsteps = nchunks // cps
    rows = cps * CHUNK
    halo = 16
    per = rows // halo
    n_halo = t // halo
    gidx = lambda b, c: b * nsteps + c
    prev_blk = lambda b, c: jnp.maximum(gidx(b, c) * per - 1, 0)
    next_blk = lambda b, c: jnp.minimum((gidx(b, c) + 1) * per, n_halo - 1)
    in_specs = [
        pl.BlockSpec((rows, D_SSD), lambda b, c: (gidx(b, c), 1)),
        pl.BlockSpec((rows, 2 * BC_W), lambda b, c: (gidx(b, c), 4)),
        pl.BlockSpec((halo, D_SSD), lambda b, c: (prev_blk(b, c), 1)),
        pl.BlockSpec((halo, 2 * BC_W), lambda b, c: (prev_blk(b, c), 4)),
        pl.BlockSpec((halo, D_SSD), lambda b, c: (next_blk(b, c), 1)),
        pl.BlockSpec((halo, 2 * BC_W), lambda b, c: (next_blk(b, c), 4)),
        pl.BlockSpec((rows, LANES), lambda b, c: (gidx(b, c), 0)),
        pl.BlockSpec((3, D_SSD), lambda b, c: (0, 0)),
        pl.BlockSpec((3, 2 * BC_W), lambda b, c: (0, 0)),
        pl.BlockSpec((1, D_SSD), lambda b, c: (0, 0)),
        pl.BlockSpec((1, 2 * BC_W), lambda b, c: (0, 0)),
        pl.BlockSpec((1, LANES), lambda b, c: (0, 0)),
        pl.BlockSpec((1, LANES), lambda b, c: (0, 0)),
        pl.BlockSpec((LANES, D_SSD), lambda b, c: (0, 0)),
        pl.BlockSpec((2 * rows, rows), lambda b, c: (0, 0)),
    ]
    args = [proj, proj, proj, proj, proj, proj, dt_raw,
            consts["cw_x"], consts["cw_bc"], consts["cb_x"], consts["cb_bc"],
            consts["dt_bias"], consts["a_log"], consts["e_fwd"], _row_shift_matrix(rows)]
    has_h0 = h0 is not None
    if has_h0:
        in_specs.append(pl.BlockSpec((1, D_SSD, STATE), lambda b, c: (b, 0, 0)))
        args.append(h0)
    return pl.pallas_call(
        functools.partial(_ssd_fwd_kernel, has_h0, cps),
        grid=(nseq, nsteps),
        in_specs=in_specs,
        out_specs=[pl.BlockSpec((rows, D_SSD), lambda b, c: (gidx(b, c), 0)),
                   pl.BlockSpec((rows, 2 * BC_W), lambda b, c: (gidx(b, c), 0)),
                   pl.BlockSpec((cps, STATE, D_SSD), lambda b, c: (gidx(b, c), 0, 0)),
                   pl.BlockSpec((1, D_SSD, STATE), lambda b, c: (b, 0, 0))],
        out_shape=[jax.ShapeDtypeStruct((t, D_SSD), BF16),
                   jax.ShapeDtypeStruct((t, 2 * BC_W), BF16),
                   jax.ShapeDtypeStruct((nseq * nchunks, STATE, D_SSD), BF16),
                   jax.ShapeDtypeStruct((nseq, D_SSD, STATE), F32)],
        scratch_shapes=[pltpu.VMEM((STATE, D_SSD), F32)],
        compiler_params=_cparams(("arbitrary", "arbitrary")),
        name="ssd_fwd",
    )(*args)


def _ssd_bwd_kernel(has_h0, cps, *refs):
    (xs_ref, bc_ref, z_ref, dt_ref, hprev_ref, dtb_ref, alog_ref, dvec_ref, nw_ref,
     eb_ref) = refs[:10]
    pos = 10
    h0_ref = None
    if has_h0:
        h0_ref = refs[pos]
        pos += 1
    y_ref, hfin_ref, h_scr = refs[pos:pos + 3]

    c = pl.program_id(1)
    nsteps = pl.num_programs(1)

    @pl.when(c == 0)
    def _():
        if has_h0:
            h_scr[...] = h0_ref[0].T
        else:
            h_scr[...] = jnp.zeros_like(h_scr)

    ri = lax.broadcasted_iota(jnp.int32, (CHUNK, CHUNK), 0)
    ci = lax.broadcasted_iota(jnp.int32, (CHUNK, CHUNK), 1)
    lower = ri >= ci
    diag = ri == ci
    left = lax.broadcasted_iota(jnp.int32, (CHUNK, LANES), 1) < HEADDIM

    h_b = h_scr[...]
    for u in reversed(range(cps)):
        rows = slice(u * CHUNK, (u + 1) * CHUNK)
        xs_bf = xs_ref[rows, :]
        xs = xs_bf.astype(F32)
        bc_bf = bc_ref[rows, :]

        dt, a = _dt_and_log_decay(dt_ref[rows, :], dtb_ref, alog_ref)
        acs = _cumsum_rows(a) * LOG2E
        rcs = _rev_cumsum_rows(a) * LOG2E
        acs_t = acs.T
        rcs_t = rcs.T
        dt_t = dt.T
        lg_t = jnp.log2(dt_t)
        rf_t = acs_t - lg_t
        rb_t = rcs_t - lg_t
        w2b_t = dt_t * jnp.exp2(rcs_t[:, 0:1] - rcs_t)
        cdec = _expand_heads(jnp.broadcast_to(jnp.exp2(rcs[0:1, :]), (8, LANES)), eb_ref)[0:1, :]

        h_f = hprev_ref[u]
        h_b_bf = h_b.astype(BF16)

        y_parts = []
        s_parts = []
        for g in range(GROUPS):
            b_g = bc_bf[:, g * STATE:(g + 1) * STATE]
            c_g = bc_bf[:, BC_W + g * STATE:BC_W + (g + 1) * STATE]
            cb = _dot_nt(c_g, b_g)
            bt = b_g.astype(F32).T
            off_f = _dot(c_g, h_f[:, g * GROUP_W:(g + 1) * GROUP_W])
            off_b = _dot(c_g, h_b_bf[:, g * GROUP_W:(g + 1) * GROUP_W])
            for k in range(HEADS_PER_GROUP // 2):
                ms, bs, cfs, cbs = [], [], [], []
                for h in (g * HEADS_PER_GROUP + 2 * k, g * HEADS_PER_GROUP + 2 * k + 1):
                    hb = HEADS + h
                    cf = acs[:, h:h + 1]
                    cbk = rcs[:, hb:hb + 1]
                    e = jnp.exp2(jnp.where(lower, cf - rf_t[h:h + 1, :], cbk - rb_t[hb:hb + 1, :]))
                    e = e + jnp.where(diag, dt_t[hb:hb + 1, :], 0.0)
                    ms.append((cb * e).astype(BF16))
                    bs.append((bt * w2b_t[hb:hb + 1, :]).astype(BF16))
                    cfs.append(cf)
                    cbs.append(cbk)
                lhs = jnp.concatenate([jnp.concatenate(ms, axis=1), jnp.concatenate(bs, axis=1)], axis=0)
                out = _dot(lhs, _pair_rhs(xs_bf, g * HEADS_PER_GROUP // 2 + k, left))
                ef = jnp.exp2(jnp.where(left, cfs[0], cfs[1]))
                eb = jnp.exp2(jnp.where(left, cbs[0], cbs[1]))
                cols = slice(k * LANES, (k + 1) * LANES)
                y_parts.append(out[:CHUNK, :] + ef * off_f[:, cols] + eb * off_b[:, cols])
                s_parts.append(out[CHUNK:, :])
        y = jnp.concatenate(y_parts, axis=1) + dvec_ref[...] * xs
        y = y * _silu(z_ref[rows, :].astype(F32))
        ms_y = jnp.mean(y * y, axis=-1, keepdims=True)
        y_ref[rows, :] = (y * lax.rsqrt(ms_y + EPS) * nw_ref[...]).astype(BF16)

        h_b = cdec * h_b + jnp.concatenate(s_parts, axis=1)
    h_scr[...] = h_b

    @pl.when(c == nsteps - 1)
    def _():
        hfin_ref[0] = h_b.T


def _ssd_bwd(xs, bc, proj, dt_raw, hprev, h0, consts, nseq, nchunks):
    t = nseq * nchunks * CHUNK
    cps = min(SSD_BWD_CHUNKS_PER_STEP, nchunks)
    nsteps = nchunks // cps
    rows = cps * CHUNK
    gidx = lambda b, c: b * nsteps + (nsteps - 1 - c)
    in_specs = [
        pl.BlockSpec((rows, D_SSD), lambda b, c: (gidx(b, c), 0)),
        pl.BlockSpec((rows, 2 * BC_W), lambda b, c: (gidx(b, c), 0)),
        pl.BlockSpec((rows, D_SSD), lambda b, c: (gidx(b, c), 0)),
        pl.BlockSpec((rows, LANES), lambda b, c: (gidx(b, c), 0)),
        pl.BlockSpec((cps, STATE, D_SSD), lambda b, c: (gidx(b, c), 0, 0)),
        pl.BlockSpec((1, LANES), lambda b, c: (0, 0)),
        pl.BlockSpec((1, LANES), lambda b, c: (0, 0)),
        pl.BlockSpec((1, D_SSD), lambda b, c: (0, 0)),
        pl.BlockSpec((1, D_SSD), lambda b, c: (0, 0)),
        pl.BlockSpec((LANES, D_SSD), lambda b, c: (0, 0)),
    ]
    args = [xs, bc, proj, dt_raw, hprev, consts["dt_bias"], consts["a_log"], consts["d_vec"],
            consts["ssd_norm"], consts["e_bwd"]]
    has_h0 = h0 is not None
    if has_h0:
        in_specs.append(pl.BlockSpec((1, D_SSD, STATE), lambda b, c: (b, 0, 0)))
        args.append(h0)
    return pl.pallas_call(
        functools.partial(_ssd_bwd_kernel, has_h0, cps),
        grid=(nseq, nsteps),
        in_specs=in_specs,
        out_specs=[pl.BlockSpec((rows, D_SSD), lambda b, c: (gidx(b, c), 0)),
                   pl.BlockSpec((1, D_SSD, STATE), lambda b, c: (b, 0, 0))],
        out_shape=[jax.ShapeDtypeStruct((t, D_SSD), BF16),
                   jax.ShapeDtypeStruct((nseq, D_SSD, STATE), F32)],
        scratch_shapes=[pltpu.VMEM((STATE, D_SSD), F32)],
        compiler_params=_cparams(("arbitrary", "arbitrary")),
        name="ssd_bwd",
    )(*args)


def _mix_kernel(y_ref, u_ref, v_ref, ga_ref, gb_ref, x_ref, mod_ref, lnw_ref, lnb_ref, ws_ref, bst_ref,
                wbs_ref, wbg_ref, wout_ref, npost_ref, npre2_ref, o_ref, h2_ref, ysgu_scr):
    tm = x_ref.shape[0]
    v = v_ref[...].astype(F32)
    mu = jnp.mean(v, axis=-1, keepdims=True)
    vc = v - mu
    var = jnp.mean(vc * vc, axis=-1, keepdims=True)
    vn = (vc * lax.rsqrt(var + EPS) * lnw_ref[...] + lnb_ref[...]).astype(BF16)
    bst = bst_ref[...]
    for r in range(tm // CHUNK):
        rows = slice(r * CHUNK, (r + 1) * CHUNK)
        for g in range(SGU_GROUPS):
            cols = slice(g * LANES, (g + 1) * LANES)
            s = _dot(ws_ref[g], vn[rows, cols]) + bst[:, g:g + 1]
            ysgu_scr[rows, cols] = (u_ref[rows, cols].astype(F32) * s).astype(BF16)
    br_ssd = _dot(y_ref[...], wbs_ref[...])
    br_sgu = _dot(ysgu_scr[...], wbg_ref[...])
    merged = _sigmoid(ga_ref[...].astype(F32)) * br_ssd + _sigmoid(gb_ref[...].astype(F32)) * br_sgu
    merged = merged.astype(BF16)
    quarter = tm // 4
    for r in range(4):
        rows = slice(r * quarter, (r + 1) * quarter)
        mix = _dot(merged[rows, :], wout_ref[...])
        ms = jnp.mean(mix * mix, axis=-1, keepdims=True)
        x1 = x_ref[rows, :] + mod_ref[0, 2:3, :] * (mix * lax.rsqrt(ms + EPS) * npost_ref[...])
        o_ref[rows, :] = x1
        h2_ref[rows, :] = _modulated_norm(x1, npre2_ref[...], mod_ref[0, 3:4, :], mod_ref[0, 4:5, :]).astype(BF16)


def _mix(y_ssd, proj, x, mod, consts, tokens_per_mod):
    t = x.shape[0]
    tm = 512
    row = lambda i: (i, 0)
    const2 = lambda i: (0, 0)
    return pl.pallas_call(
        _mix_kernel,
        grid=(t // tm,),
        in_specs=[pl.BlockSpec((tm, D_SSD), row),
                  pl.BlockSpec((tm, D_MODEL), lambda i: (i, 5)),
                  pl.BlockSpec((tm, D_MODEL), lambda i: (i, 6)),
                  pl.BlockSpec((tm, D_MODEL), lambda i: (i, 7)),
                  pl.BlockSpec((tm, D_MODEL), lambda i: (i, 8)),
                  pl.BlockSpec((tm, D_MODEL), row),
                  pl.BlockSpec((1, 6, D_MODEL), lambda i: ((i * tm) // tokens_per_mod, 0, 0)),
                  pl.BlockSpec((1, D_MODEL), const2),
                  pl.BlockSpec((1, D_MODEL), const2),
                  pl.BlockSpec((SGU_GROUPS, CHUNK, CHUNK), lambda i: (0, 0, 0)),
                  pl.BlockSpec((CHUNK, SGU_GROUPS), const2),
                  pl.BlockSpec((D_SSD, D_MODEL), const2),
                  pl.BlockSpec((D_MODEL, D_MODEL), const2),
                  pl.BlockSpec((D_MODEL, D_MODEL), const2),
                  pl.BlockSpec((1, D_MODEL), const2),
                  pl.BlockSpec((1, D_MODEL), const2)],
        out_specs=[pl.BlockSpec((tm, D_MODEL), row), pl.BlockSpec((tm, D_MODEL), row)],
        out_shape=[jax.ShapeDtypeStruct((t, D_MODEL), F32), jax.ShapeDtypeStruct((t, D_MODEL), BF16)],
        scratch_shapes=[pltpu.VMEM((tm, D_MODEL), BF16)],
        compiler_params=_cparams(("arbitrary",)),
        name="mix",
    )(y_ssd, proj, proj, proj, proj, x, mod, consts["sgu_norm_w"], consts["sgu_norm_b"], consts["sgu_w"],
      consts["sgu_bt"], consts["w_branch_ssd"], consts["w_branch_sgu"], consts["w_out"], consts["norm_mix_post"],
      consts["norm_ffn_pre"])


def _matmul_kernel(h_ref, w_ref, o_ref):
    o_ref[...] = _dot(h_ref[...], w_ref[...]).astype(BF16)


def _ffn_up(h2, w_up):
    t = h2.shape[0]
    tm, tn = 1024, D_FF
    return pl.pallas_call(
        _matmul_kernel,
        grid=(t // tm, 2 * D_FF // tn),
        in_specs=[pl.BlockSpec((tm, D_MODEL), lambda i, j: (i, 0)),
                  pl.BlockSpec((D_MODEL, tn), lambda i, j: (0, j))],
        out_specs=pl.BlockSpec((tm, tn), lambda i, j: (i, j)),
        out_shape=jax.ShapeDtypeStruct((t, 2 * D_FF), BF16),
        compiler_params=_cparams(("arbitrary", "arbitrary")),
        name="ffn_up",
    )(h2, w_up)


def _gelu_tanh(x):
    h = 0.5 * x
    u = x * (GELU_C + (GELU_C * 0.044715) * (x * x))
    return h + h * jnp.tanh(u)


def _grid_conv(x_bf, prev_bf, next_bf, w, b, first, last):
    tm = x_bf.shape[0]
    zero = jnp.zeros_like(prev_bf)
    ext_bf = jnp.concatenate([jnp.where(first, zero, prev_bf), x_bf, jnp.where(last, zero, next_bf)], axis=0)
    ext = ext_bf.astype(F32)
    n = ext.shape[0]
    col = lax.broadcasted_iota(jnp.int32, ext.shape, 0) % GRID_W
    shifted = (jnp.where(col == 0, 0.0, pltpu.roll(ext, 1, axis=0)).astype(BF16),
               ext_bf,
               jnp.where(col == GRID_W - 1, 0.0, pltpu.roll(ext, n - 1, axis=0)).astype(BF16))
    wb = w.astype(BF16)
    acc = None
    for dy in range(3):
        for dx in range(3):
            term = wb[3 * dy + dx:3 * dy + dx + 1, :] * shifted[dx][dy * GRID_W:dy * GRID_W + tm, :]
            acc = term if acc is None else acc + term
    return acc + b.astype(BF16)


def _seq_conv(x_bf, w, b, seq_len):
    x = x_bf.astype(F32)
    n = x.shape[0]
    pos = lax.broadcasted_iota(jnp.int32, x.shape, 0) % seq_len
    xm1 = jnp.where(pos == 0, 0.0, pltpu.roll(x, 1, axis=0)).astype(BF16)
    xp1 = jnp.where(pos == seq_len - 1, 0.0, pltpu.roll(x, n - 1, axis=0)).astype(BF16)
    wb = w.astype(BF16)
    return wb[3:4, :] * xm1 + wb[4:5, :] * x_bf + wb[5:6, :] * xp1 + b.astype(BF16)


def _ffn_down_kernel(on_grid, seq_len, tiles_per_seq, *refs):
    if on_grid:
        (up_ref, upp_ref, upn_ref, w_ref, b_ref, wd_ref, x_ref, mod_ref, npost_ref, o_ref,
         g0_scr, g1_scr, acc_scr) = refs
    else:
        (up_ref, w_ref, b_ref, wd_ref, x_ref, mod_ref, npost_ref, o_ref, g0_scr, g1_scr, acc_scr) = refs
    i = pl.program_id(0)
    first = i % tiles_per_seq == 0
    last = i % tiles_per_seq == tiles_per_seq - 1

    def conv(c0):
        cols = pl.ds(c0, FF_BLK)
        if on_grid:
            return _grid_conv(up_ref[:, cols], upp_ref[:, cols], upn_ref[:, cols], w_ref[:, cols], b_ref[:, cols],
                              first, last)
        return _seq_conv(up_ref[:, cols], w_ref[:, cols], b_ref[:, cols], seq_len)

    def geglu_block(k, dst_ref):
        c0 = pl.multiple_of(k * FF_BLK, FF_BLK)
        a = conv(c0)
        val = conv(pl.multiple_of(c0 + D_FF, FF_BLK))
        dst_ref[...] = (_gelu_tanh(a) * val).astype(BF16)

    def down_block(k, src_ref):
        rows = pl.ds(pl.multiple_of(k * FF_BLK, FF_BLK), FF_BLK)
        acc_scr[...] += _dot(src_ref[...], wd_ref[rows, :])

    acc_scr[...] = jnp.zeros_like(acc_scr)
    geglu_block(0, g0_scr)

    def block_pair(m, carry):
        k = 2 * m + 1
        down_block(k - 1, g0_scr)
        geglu_block(k, g1_scr)
        down_block(k, g1_scr)
        geglu_block(k + 1, g0_scr)
        return carry

    lax.fori_loop(0, (N_FF_BLK - 1) // 2, block_pair, 0)
    down_block(N_FF_BLK - 1, g0_scr)
    f = acc_scr[...]
    ms = jnp.mean(f * f, axis=-1, keepdims=True)
    o_ref[...] = x_ref[...] + mod_ref[0, 5:6, :] * (f * lax.rsqrt(ms + EPS) * npost_ref[...])


def _ffn_down(up, x, mod, consts, tokens_per_mod, on_grid, seq_len):
    t = x.shape[0]
    tm = 512
    tiles_per_seq = max(seq_len // tm, 1)
    rows_per_tile = tm // GRID_W
    n_rows = t // GRID_W
    in_specs = [pl.BlockSpec((tm, 2 * D_FF), lambda i: (i, 0))]
    args = [up]
    if on_grid:
        in_specs += [pl.BlockSpec((GRID_W, 2 * D_FF), lambda i: (jnp.maximum(i * rows_per_tile - 1, 0), 0)),
                     pl.BlockSpec((GRID_W, 2 * D_FF), lambda i: (jnp.minimum((i + 1) * rows_per_tile, n_rows - 1), 0))]
        args += [up, up]
    in_specs += [pl.BlockSpec((9, 2 * D_FF), lambda i: (0, 0)),
                 pl.BlockSpec((1, 2 * D_FF), lambda i: (0, 0)),
                 pl.BlockSpec((D_FF, D_MODEL), lambda i: (0, 0)),
                 pl.BlockSpec((tm, D_MODEL), lambda i: (i, 0)),
                 pl.BlockSpec((1, 6, D_MODEL), lambda i: ((i * tm) // tokens_per_mod, 0, 0)),
                 pl.BlockSpec((1, D_MODEL), lambda i: (0, 0))]
    args += [consts["ffn_conv_w"], consts["ffn_conv_b"], consts["ffn_w_down"], x, mod, consts["norm_ffn_post"]]
    return pl.pallas_call(
        functools.partial(_ffn_down_kernel, on_grid, seq_len, tiles_per_seq),
        grid=(t // tm,),
        in_specs=in_specs,
        out_specs=pl.BlockSpec((tm, D_MODEL), lambda i: (i, 0)),
        out_shape=jax.ShapeDtypeStruct((t, D_MODEL), F32),
        scratch_shapes=[pltpu.VMEM((tm, FF_BLK), BF16), pltpu.VMEM((tm, FF_BLK), BF16),
                        pltpu.VMEM((tm, D_MODEL), F32)],
        compiler_params=_cparams(("arbitrary",)),
        name="ffn_down",
    )(*args)


def _trunk_path(x, mod, h0_f, h0_b, on_grid, consts):
    nseq, seq_len, _ = x.shape
    t = nseq * seq_len
    nchunks = seq_len // CHUNK
    tokens_per_mod = t // mod.shape[0]
    x2d = x.reshape(t, D_MODEL)

    proj, dt_raw = _inproj(x2d, mod, consts["norm_mix_pre"], consts["w_in_main"], consts["w_in_dt"], tokens_per_mod)
    xs, bc, hprev, hf = _ssd_fwd(proj, dt_raw, h0_f, consts, nseq, nchunks)
    y_ssd, hb = _ssd_bwd(xs, bc, proj, dt_raw, hprev, h0_b, consts, nseq, nchunks)
    x1, h2 = _mix(y_ssd, proj, x2d, mod, consts, tokens_per_mod)
    up = _ffn_up(h2, consts["ffn_w_up"])
    x2 = _ffn_down(up, x1, mod, consts, tokens_per_mod, on_grid, seq_len)
    return x2.reshape(nseq, seq_len, D_MODEL), hf, hb


def _head_expansion(offset):
    rows = jnp.arange(LANES)[:, None]
    cols = jnp.arange(D_SSD)[None, :] // HEADDIM
    return (rows == cols + offset).astype(BF16)


def _layer_consts(i, p):
    w_in_main, w_in_dt = _w_in_prep(p["w_in"], i)
    row = lambda v: v.reshape(1, -1).astype(F32)
    pad_lanes = lambda v: jnp.pad(v.reshape(1, -1).astype(F32), ((0, 0), (0, LANES - N_DT)))
    conv_w = p["ssd_conv_w"][i]
    conv_b = p["ssd_conv_b"][i]
    return {
        "norm_mix_pre": row(p["norm_mix_pre"][i]),
        "norm_mix_post": row(p["norm_mix_post"][i]),
        "norm_ffn_pre": row(p["norm_ffn_pre"][i]),
        "norm_ffn_post": row(p["norm_ffn_post"][i]),
        "w_in_main": w_in_main, "w_in_dt": w_in_dt,
        "cw_x": conv_w[:, :D_SSD], "cw_bc": conv_w[:, D_SSD:],
        "cb_x": row(conv_b[:D_SSD]), "cb_bc": row(conv_b[D_SSD:]),
        "dt_bias": pad_lanes(p["ssd_dt_bias"][i]),
        "a_log": pad_lanes(p["ssd_a_log"][i]),
        "d_vec": row(jnp.repeat(p["ssd_d"][i], HEADDIM)),
        "ssd_norm": row(p["ssd_norm"][i]),
        "e_fwd": _head_expansion(0), "e_bwd": _head_expansion(HEADS),
        "sgu_norm_w": row(p["sgu_norm_w"][i]), "sgu_norm_b": row(p["sgu_norm_b"][i]),
        "sgu_w": p["sgu_w"][i].astype(BF16),
        "sgu_bt": jnp.transpose(p["sgu_b"][i]).astype(F32),
        "w_branch_ssd": p["w_branch_ssd"][i].astype(BF16),
        "w_branch_sgu": p["w_branch_sgu"][i].astype(BF16),
        "w_out": p["w_out"][i].astype(BF16),
        "ffn_w_up": p["ffn_w_up"][i].astype(BF16),
        "ffn_conv_w": p["ffn_conv_w"][i].reshape(9, 2 * D_FF).astype(F32),
        "ffn_conv_b": row(p["ffn_conv_b"][i]),
        "ffn_w_down": p["ffn_w_down"][i].astype(BF16),
    }


def kernel(x_prompt, x_sample, state_ssd_fwd, state_ssd_bwd, c, c_ctx, w_mod, b_mod, norm_mix_pre, norm_mix_post, norm_ffn_pre, norm_ffn_post, w_in, ssd_conv_w, ssd_conv_b, ssd_a_log, ssd_dt_bias, ssd_d, ssd_norm, sgu_norm_w, sgu_norm_b, sgu_w, sgu_b, w_branch_ssd, w_branch_sgu, w_out, ffn_w_up, ffn_conv_w, ffn_conv_b, ffn_w_down):
    params = dict(norm_mix_pre=norm_mix_pre, norm_mix_post=norm_mix_post, norm_ffn_pre=norm_ffn_pre,
                  norm_ffn_post=norm_ffn_post, w_in=w_in, ssd_conv_w=ssd_conv_w, ssd_conv_b=ssd_conv_b,
                  ssd_a_log=ssd_a_log, ssd_dt_bias=ssd_dt_bias, ssd_d=ssd_d, ssd_norm=ssd_norm,
                  sgu_norm_w=sgu_norm_w, sgu_norm_b=sgu_norm_b, sgu_w=sgu_w, sgu_b=sgu_b,
                  w_branch_ssd=w_branch_ssd, w_branch_sgu=w_branch_sgu, w_out=w_out, ffn_w_up=ffn_w_up,
                  ffn_conv_w=ffn_conv_w, ffn_conv_b=ffn_conv_b, ffn_w_down=ffn_w_down)
    depth = w_mod.shape[0]
    n_lat = c.shape[0]
    c_rows = jnp.concatenate([c_ctx[None, :], c, jnp.zeros((8 - 1 - n_lat, D_MODEL), F32)], axis=0)
    xp, xs = x_prompt, x_sample
    new_f, new_b = [], []
    for i in range(depth):
        consts = _layer_consts(i, params)
        mod = _mod_vectors(c_rows, w_mod[i], b_mod[i]).reshape(8, 6, D_MODEL)
        xp, hf, hb = _trunk_path(xp, mod[0:1], None, None, False, consts)
        new_f.append(hf.reshape(-1, HEADS, HEADDIM, STATE))
        new_b.append(hb.reshape(-1, HEADS, HEADDIM, STATE))
        xs, _, _ = _trunk_path(xs, mod[1:1 + n_lat],
                               state_ssd_fwd[:, i].reshape(n_lat, D_SSD, STATE),
                               state_ssd_bwd[:, i].reshape(n_lat, D_SSD, STATE), True, consts)
    return (xp, xs, jnp.stack(new_f, axis=1).astype(x_prompt.dtype), jnp.stack(new_b, axis=1).astype(x_prompt.dtype))
```

```python
import functools

import jax
import jax.numpy as jnp
from jax import lax
from jax.experimental import pallas as pl
from jax.experimental.pallas import tpu as pltpu

F32 = jnp.float32
BF16 = jnp.bfloat16

D_MODEL = 1024
GRID_W = 64
EPS = 1e-6
LOG2E = 1.4426950408889634
GELU_C = 0.7978845608028654
D_SSD = 2 * D_MODEL
HEADDIM = 64
HEADS = D_SSD // HEADDIM
STATE = 128
GROUPS = 4
HEADS_PER_GROUP = HEADS // GROUPS
GROUP_W = HEADS_PER_GROUP * HEADDIM
BC_W = GROUPS * STATE
CONV_CH = D_SSD + 2 * BC_W
CHUNK = 128
SSD_FWD_CHUNKS_PER_STEP = 2
SSD_BWD_CHUNKS_PER_STEP = 4
SGU_GROUPS = 8
D_FF = 2816
SPLIT_XBC = D_SSD + CONV_CH
N_DT = 2 * HEADS
LANES = 128
PROJ_COLS = 9216
FF_BLK = 256
N_FF_BLK = D_FF // FF_BLK

VMEM_LIMIT_BYTES = 56 * 1024 * 1024


def _cparams(sem):
    return pltpu.CompilerParams(dimension_semantics=sem, vmem_limit_bytes=VMEM_LIMIT_BYTES)


def _sigmoid(x):
    return 0.5 + 0.5 * jnp.tanh(0.5 * x)


def _silu(x):
    h = 0.5 * x
    return h + h * jnp.tanh(h)


def _dot(a, b):
    return jnp.dot(a, b, preferred_element_type=F32)


def _dot_nt(a, b):
    return lax.dot_general(a, b, (((1,), (1,)), ((), ())), preferred_element_type=F32)


def _mod_kernel(c_ref, w_ref, b_ref, o_ref):
    c = c_ref[...]
    o_ref[...] = jnp.dot(_silu(c), w_ref[...], preferred_element_type=F32,
                         precision=lax.Precision.HIGHEST) + b_ref[...]


def _mod_vectors(c_rows, w_mod, b_mod):
    rows = c_rows.shape[0]
    tn = 1024
    return pl.pallas_call(
        _mod_kernel,
        grid=(6 * D_MODEL // tn,),
        in_specs=[pl.BlockSpec((rows, D_MODEL), lambda j: (0, 0)),
                  pl.BlockSpec((D_MODEL, tn), lambda j: (0, j)),
                  pl.BlockSpec((1, tn), lambda j: (0, j))],
        out_specs=pl.BlockSpec((rows, tn), lambda j: (0, j)),
        out_shape=jax.ShapeDtypeStruct((rows, 6 * D_MODEL), F32),
        compiler_params=_cparams(("arbitrary",)),
        name="mod",
    )(c_rows, w_mod, b_mod.reshape(1, -1))


def _w_in_prep_kernel(wt_ref, main_ref, dt_ref):
    main_ref[:, 0:SPLIT_XBC] = wt_ref[0:SPLIT_XBC, :].T.astype(BF16)
    main_ref[:, SPLIT_XBC:] = wt_ref[SPLIT_XBC + N_DT:, :].T.astype(BF16)
    dt = wt_ref[SPLIT_XBC:SPLIT_XBC + LANES, :].T
    lane = lax.broadcasted_iota(jnp.int32, dt.shape, 1)
    dt_ref[...] = jnp.where(lane < N_DT, dt, 0.0).astype(BF16)


def _w_in_prep(w_in, layer):
    rows = 128
    return pl.pallas_call(
        _w_in_prep_kernel,
        grid=(D_MODEL // rows,),
        in_specs=[pl.BlockSpec((PROJ_COLS + N_DT, rows), lambda r: (0, r))],
        out_specs=[pl.BlockSpec((rows, PROJ_COLS), lambda r: (r, 0)),
                   pl.BlockSpec((rows, LANES), lambda r: (r, 0))],
        out_shape=[jax.ShapeDtypeStruct((D_MODEL, PROJ_COLS), BF16),
                   jax.ShapeDtypeStruct((D_MODEL, LANES), BF16)],
        compiler_params=_cparams(("arbitrary",)),
        name="w_in_prep",
    )(jnp.swapaxes(w_in[layer], 0, 1))


def _modulated_norm(x, nw, shift, scale):
    ms = jnp.mean(x * x, axis=-1, keepdims=True)
    return (x * lax.rsqrt(ms + EPS) * nw) * (1.0 + scale) + shift


def _inproj_kernel(n_col_steps, x0_ref, xn_ref, mod0_ref, modn_ref, nw_ref, w_ref, wdt_ref, o_ref, dt_ref,
                   ha_scr, hb_scr):
    i = pl.program_id(0)
    j = pl.program_id(1)
    part = xn_ref.shape[0] // n_col_steps
    rows = pl.ds(pl.multiple_of(j * part, part), part)

    def norm(x, mod_ref):
        return _modulated_norm(x, nw_ref[...], mod_ref[0, 0:1, :], mod_ref[0, 1:2, :]).astype(BF16)

    @pl.when((i == 0) & (j == 0))
    def _():
        ha_scr[...] = norm(x0_ref[...], mod0_ref)

    def step(cur_scr, nxt_scr):
        nxt_scr[rows, :] = norm(xn_ref[rows, :], modn_ref)
        dt_ref[rows, :] = _dot(cur_scr[rows, :], wdt_ref[...])
        o_ref[...] = _dot(cur_scr[...], w_ref[...]).astype(BF16)

    @pl.when(i % 2 == 0)
    def _():
        step(ha_scr, hb_scr)

    @pl.when(i % 2 == 1)
    def _():
        step(hb_scr, ha_scr)


def _inproj(x, mod, nw, w_main, w_dt, tokens_per_mod):
    t = x.shape[0]
    tm, tn = 1024, PROJ_COLS // 4
    n_tiles = t // tm
    nxt = lambda i: jnp.minimum(i + 1, n_tiles - 1)
    return pl.pallas_call(
        functools.partial(_inproj_kernel, PROJ_COLS // tn),
        grid=(n_tiles, PROJ_COLS // tn),
        in_specs=[pl.BlockSpec((tm, D_MODEL), lambda i, j: (0, 0)),
                  pl.BlockSpec((tm, D_MODEL), lambda i, j: (nxt(i), 0)),
                  pl.BlockSpec((1, 6, D_MODEL), lambda i, j: (0, 0, 0)),
                  pl.BlockSpec((1, 6, D_MODEL), lambda i, j: ((nxt(i) * tm) // tokens_per_mod, 0, 0)),
                  pl.BlockSpec((1, D_MODEL), lambda i, j: (0, 0)),
                  pl.BlockSpec((D_MODEL, tn), lambda i, j: (0, j)),
                  pl.BlockSpec((D_MODEL, LANES), lambda i, j: (0, 0))],
        out_specs=[pl.BlockSpec((tm, tn), lambda i, j: (i, j)),
                   pl.BlockSpec((tm, LANES), lambda i, j: (i, 0))],
        out_shape=[jax.ShapeDtypeStruct((t, PROJ_COLS), BF16),
                   jax.ShapeDtypeStruct((t, LANES), F32)],
        scratch_shapes=[pltpu.VMEM((tm, D_MODEL), BF16), pltpu.VMEM((tm, D_MODEL), BF16)],
        compiler_params=_cparams(("arbitrary", "arbitrary")),
        name="inproj",
    )(x, x, mod, mod, nw, w_main, w_dt)


def _softplus(x):
    return jnp.maximum(x, 0.0) + jnp.log1p(jnp.exp(-jnp.abs(x)))


def _cumsum_rows(a):
    n = a.shape[0]
    rid = lax.broadcasted_iota(jnp.int32, a.shape, 0)
    s = 1
    while s < n:
        a = a + jnp.where(rid >= s, pltpu.roll(a, s, axis=0), 0.0)
        s *= 2
    return a


def _rev_cumsum_rows(a):
    n = a.shape[0]
    rid = lax.broadcasted_iota(jnp.int32, a.shape, 0)
    s = 1
    while s < n:
        a = a + jnp.where(rid < n - s, pltpu.roll(a, n - s, axis=0), 0.0)
        s *= 2
    return a


def _expand_heads(w, e_ref):
    hi = w.astype(BF16)
    lo = (w - hi.astype(F32)).astype(BF16)
    e = e_ref[...]
    return _dot(hi, e) + _dot(lo, e)


def _row_shift_matrix(n):
    r = jnp.arange(n)[:, None]
    c = jnp.arange(n)[None, :]
    return jnp.concatenate([c == r - 1, c == r + 1], axis=0).astype(BF16)


def _conv3_silu(main_ref, prev_ref, next_ref, w_ref, b_ref, shift_ref, first, last):
    x_bf = main_ref[...]
    n = x_bf.shape[0]
    x = x_bf.astype(F32)
    sh = _dot(shift_ref[...], x_bf)
    xm1 = sh[:n, :]
    xp1 = sh[n:, :]
    prow = jnp.where(first, 0.0, prev_ref[...].astype(F32)[-1:, :])
    nrow = jnp.where(last, 0.0, next_ref[...].astype(F32)[0:1, :])
    rid = lax.broadcasted_iota(jnp.int32, (8, x.shape[1]), 0)
    xm1 = jnp.concatenate([jnp.where(rid == 0, prow, xm1[:8, :]), xm1[8:, :]], axis=0)
    xp1 = jnp.concatenate([xp1[:-8, :], jnp.where(rid == 7, nrow, xp1[-8:, :])], axis=0)
    w = w_ref[...]
    y = w[0:1, :] * xm1 + w[1:2, :] * x + w[2:3, :] * xp1 + b_ref[...]
    return _silu(y)


def _dt_and_log_decay(dt_raw, dtb_ref, alog_ref):
    dt = _softplus(dt_raw + dtb_ref[...])
    return dt, dt * (-jnp.exp(alog_ref[...]))


def _pair_rhs(xs_bf, pair, left):
    xp = xs_bf[:, pair * LANES:(pair + 1) * LANES]
    zero = jnp.zeros_like(xp)
    return jnp.concatenate([jnp.where(left, xp, zero), jnp.where(left, zero, xp)], axis=0)


def _ssd_fwd_kernel(has_h0, cps, *refs):
    (xm_ref, bcm_ref, xp_ref, bcp_ref, xn_ref, bcn_ref, dt_ref,
     cwx_ref, cwbc_ref, cbx_ref, cbbc_ref, dtb_ref, alog_ref, ef_ref, s3_ref) = refs[:15]
    pos = 15
    h0_ref = None
    if has_h0:
        h0_ref = refs[pos]
        pos += 1
    xs_ref, bc_ref, hprev_ref, hfin_ref, h_scr = refs[pos:pos + 5]

    c = pl.program_id(1)
    nsteps = pl.num_programs(1)
    first = c == 0
    last = c == nsteps - 1

    @pl.when(first)
    def _():
        if has_h0:
            h_scr[...] = h0_ref[0].T
        else:
            h_scr[...] = jnp.zeros_like(h_scr)

    xs_all = _conv3_silu(xm_ref, xp_ref, xn_ref, cwx_ref, cbx_ref, s3_ref, first, last)
    bc_all = _conv3_silu(bcm_ref, bcp_ref, bcn_ref, cwbc_ref, cbbc_ref, s3_ref, first, last)
    xs_all_bf = xs_all.astype(BF16)
    xs_ref[...] = xs_all_bf
    bc_ref[...] = bc_all.astype(BF16)

    left = lax.broadcasted_iota(jnp.int32, (CHUNK, LANES), 1) < HEADDIM
    h_cur = h_scr[...]
    for u in range(cps):
        rows = slice(u * CHUNK, (u + 1) * CHUNK)
        xs_bf = xs_all_bf[rows, :]
        bc = bc_all[rows, :]
        dt, a = _dt_and_log_decay(dt_ref[rows, :], dtb_ref, alog_ref)
        acs = _cumsum_rows(a)
        acs_t = acs.T
        w2_t = dt.T * jnp.exp(acs_t[:, CHUNK - 1:CHUNK] - acs_t)
        cdec = _expand_heads(jnp.broadcast_to(jnp.exp(acs[CHUNK - 1:CHUNK, :]), (8, LANES)), ef_ref)[0:1, :]

        parts = []
        for g in range(GROUPS):
            bt = bc[:, g * STATE:(g + 1) * STATE].T
            for k in range(HEADS_PER_GROUP // 2):
                h = g * HEADS_PER_GROUP + 2 * k
                lhs = jnp.concatenate([(bt * w2_t[h:h + 1, :]).astype(BF16),
                                       (bt * w2_t[h + 1:h + 2, :]).astype(BF16)], axis=1)
                parts.append(_dot(lhs, _pair_rhs(xs_bf, h // 2, left)))
        s_loc = jnp.concatenate(parts, axis=1)

        hprev_ref[u] = h_cur.astype(BF16)
        h_cur = cdec * h_cur + s_loc
    h_scr[...] = h_cur

    @pl.when(last)
    def _():
        hfin_ref[0] = h_cur.T


def _ssd_fwd(proj, dt_raw, h0, consts, nseq, nchunks):
    t = nseq * nchunks * CHUNK
    cps = min(SSD_FWD_CHUNKS_PER_STEP, nchunks)
    nsteps = nchunks // cps
    rows = cps * CHUNK
    halo = 16
    per = rows // halo
    n_halo = t // halo
    gidx = lambda b, c: b * nsteps + c
    prev_blk = lambda b, c: jnp.maximum(gidx(b, c) * per - 1, 0)
    next_blk = lambda b, c: jnp.minimum((gidx(b, c) + 1) * per, n_halo - 1)
    in_specs = [
        pl.BlockSpec((rows, D_SSD), lambda b, c: (gidx(b, c), 1)),
        pl.BlockSpec((rows, 2 * BC_W), lambda b, c: (gidx(b, c), 4)),
        pl.BlockSpec((halo, D_SSD), lambda b, c: (prev_blk(b, c), 1)),
        pl.BlockSpec((halo, 2 * BC_W), lambda b, c: (prev_blk(b, c), 4)),
        pl.BlockSpec((halo, D_SSD), lambda b, c: (next_blk(b, c), 1)),
        pl.BlockSpec((halo, 2 * BC_W), lambda b, c: (next_blk(b, c), 4)),
        pl.BlockSpec((rows, LANES), lambda b, c: (gidx(b, c), 0)),
        pl.BlockSpec((3, D_SSD), lambda b, c: (0, 0)),
        pl.BlockSpec((3, 2 * BC_W), lambda b, c: (0, 0)),
        pl.BlockSpec((1, D_SSD), lambda b, c: (0, 0)),
        pl.BlockSpec((1, 2 * BC_W), lambda b, c: (0, 0)),
        pl.BlockSpec((1, LANES), lambda b, c: (0, 0)),
        pl.BlockSpec((1, LANES), lambda b, c: (0, 0)),
        pl.BlockSpec((LANES, D_SSD), lambda b, c: (0, 0)),
        pl.BlockSpec((2 * rows, rows), lambda b, c: (0, 0)),
    ]
    args = [proj, proj, proj, proj, proj, proj, dt_raw,
            consts["cw_x"], consts["cw_bc"], consts["cb_x"], consts["cb_bc"],
            consts["dt_bias"], consts["a_log"], consts["e_fwd"], _row_shift_matrix(rows)]
    has_h0 = h0 is not None
    if has_h0:
        in_specs.append(pl.BlockSpec((1, D_SSD, STATE), lambda b, c: (b, 0, 0)))
        args.append(h0)
    return pl.pallas_call(
        functools.partial(_ssd_fwd_kernel, has_h0, cps),
        grid=(nseq, nsteps),
        in_specs=in_specs,
        out_specs=[pl.BlockSpec((rows, D_SSD), lambda b, c: (gidx(b, c), 0)),
                   pl.BlockSpec((rows, 2 * BC_W), lambda b, c: (gidx(b, c), 0)),
                   pl.BlockSpec((cps, STATE, D_SSD), lambda b, c: (gidx(b, c), 0, 0)),
                   pl.BlockSpec((1, D_SSD, STATE), lambda b, c: (b, 0, 0))],
        out_shape=[jax.ShapeDtypeStruct((t, D_SSD), BF16),
                   jax.ShapeDtypeStruct((t, 2 * BC_W), BF16),
                   jax.ShapeDtypeStruct((nseq * nchunks, STATE, D_SSD), BF16),
                   jax.ShapeDtypeStruct((nseq, D_SSD, STATE), F32)],
        scratch_shapes=[pltpu.VMEM((STATE, D_SSD), F32)],
        compiler_params=_cparams(("arbitrary", "arbitrary")),
        name="ssd_fwd",
    )(*args)


def _ssd_bwd_kernel(has_h0, cps, *refs):
    (xs_ref, bc_ref, z_ref, dt_ref, hprev_ref, dtb_ref, alog_ref, dvec_ref, nw_ref,
     eb_ref) = refs[:10]
    pos = 10
    h0_ref = None
    if has_h0:
        h0_ref = refs[pos]
        pos += 1
    y_ref, hfin_ref, h_scr = refs[pos:pos + 3]

    c = pl.program_id(1)
    nsteps = pl.num_programs(1)

    @pl.when(c == 0)
    def _():
        if has_h0:
            h_scr[...] = h0_ref[0].T
        else:
            h_scr[...] = jnp.zeros_like(h_scr)

    ri = lax.broadcasted_iota(jnp.int32, (CHUNK, CHUNK), 0)
    ci = lax.broadcasted_iota(jnp.int32, (CHUNK, CHUNK), 1)
    lower = ri >= ci
    diag = ri == ci
    left = lax.broadcasted_iota(jnp.int32, (CHUNK, LANES), 1) < HEADDIM

    h_b = h_scr[...]
    for u in reversed(range(cps)):
        rows = slice(u * CHUNK, (u + 1) * CHUNK)
        xs_bf = xs_ref[rows, :]
        xs = xs_bf.astype(F32)
        bc_bf = bc_ref[rows, :]

        dt, a = _dt_and_log_decay(dt_ref[rows, :], dtb_ref, alog_ref)
        acs = _cumsum_rows(a) * LOG2E
        rcs = _rev_cumsum_rows(a) * LOG2E
        acs_t = acs.T
        rcs_t = rcs.T
        dt_t = dt.T
        lg_t = jnp.log2(dt_t)
        rf_t = acs_t - lg_t
        rb_t = rcs_t - lg_t
        w2b_t = dt_t * jnp.exp2(rcs_t[:, 0:1] - rcs_t)
        cdec = _expand_heads(jnp.broadcast_to(jnp.exp2(rcs[0:1, :]), (8, LANES)), eb_ref)[0:1, :]

        h_f = hprev_ref[u]
        h_b_bf = h_b.astype(BF16)

        y_parts = []
        s_parts = []
        for g in range(GROUPS):
            b_g = bc_bf[:, g * STATE:(g + 1) * STATE]
            c_g = bc_bf[:, BC_W + g * STATE:BC_W + (g + 1) * STATE]
            cb = _dot_nt(c_g, b_g)
            bt = b_g.astype(F32).T
            off_f = _dot(c_g, h_f[:, g * GROUP_W:(g + 1) * GROUP_W])
            off_b = _dot(c_g, h_b_bf[:, g * GROUP_W:(g + 1) * GROUP_W])
            for k in range(HEADS_PER_GROUP // 2):
                ms, bs, cfs, cbs = [], [], [], []
                for h in (g * HEADS_PER_GROUP + 2 * k, g * HEADS_PER_GROUP + 2 * k + 1):
                    hb = HEADS + h
                    cf = acs[:, h:h + 1]
                    cbk = rcs[:, hb:hb + 1]
                    e = jnp.exp2(jnp.where(lower, cf - rf_t[h:h + 1, :], cbk - rb_t[hb:hb + 1, :]))
                    e = e + jnp.where(diag, dt_t[hb:hb + 1, :], 0.0)
                    ms.append((cb * e).astype(BF16))
                    bs.append((bt * w2b_t[hb:hb + 1, :]).astype(BF16))
                    cfs.append(cf)
                    cbs.append(cbk)
                lhs = jnp.concatenate([jnp.concatenate(ms, axis=1), jnp.concatenate(bs, axis=1)], axis=0)
                out = _dot(lhs, _pair_rhs(xs_bf, g * HEADS_PER_GROUP // 2 + k, left))
                ef = jnp.exp2(jnp.where(left, cfs[0], cfs[1]))
                eb = jnp.exp2(jnp.where(left, cbs[0], cbs[1]))
                cols = slice(k * LANES, (k + 1) * LANES)
                y_parts.append(out[:CHUNK, :] + ef * off_f[:, cols] + eb * off_b[:, cols])
                s_parts.append(out[CHUNK:, :])
        y = jnp.concatenate(y_parts, axis=1) + dvec_ref[...] * xs
        y = y * _silu(z_ref[rows, :].astype(F32))
        ms_y = jnp.mean(y * y, axis=-1, keepdims=True)
        y_ref[rows, :] = (y * lax.rsqrt(ms_y + EPS) * nw_ref[...]).astype(BF16)

        h_b = cdec * h_b + jnp.concatenate(s_parts, axis=1)
    h_scr[...] = h_b

    @pl.when(c == nsteps - 1)
    def _():
        hfin_ref[0] = h_b.T


def _ssd_bwd(xs, bc, proj, dt_raw, hprev, h0, consts, nseq, nchunks):
    t = nseq * nchunks * CHUNK
    cps = min(SSD_BWD_CHUNKS_PER_STEP, nchunks)
    nsteps = nchunks // cps
    rows = cps * CHUNK
    gidx = lambda b, c: b * nsteps + (nsteps - 1 - c)
    in_specs = [
        pl.BlockSpec((rows, D_SSD), lambda b, c: (gidx(b, c), 0)),
        pl.BlockSpec((rows, 2 * BC_W), lambda b, c: (gidx(b, c), 0)),
        pl.BlockSpec((rows, D_SSD), lambda b, c: (gidx(b, c), 0)),
        pl.BlockSpec((rows, LANES), lambda b, c: (gidx(b, c), 0)),
        pl.BlockSpec((cps, STATE, D_SSD), lambda b, c: (gidx(b, c), 0, 0)),
        pl.BlockSpec((1, LANES), lambda b, c: (0, 0)),
        pl.BlockSpec((1, LANES), lambda b, c: (0, 0)),
        pl.BlockSpec((1, D_SSD), lambda b, c: (0, 0)),
        pl.BlockSpec((1, D_SSD), lambda b, c: (0, 0)),
        pl.BlockSpec((LANES, D_SSD), lambda b, c: (0, 0)),
    ]
    args = [xs, bc, proj, dt_raw, hprev, consts["dt_bias"], consts["a_log"], consts["d_vec"],
            consts["ssd_norm"], consts["e_bwd"]]
    has_h0 = h0 is not None
    if has_h0:
        in_specs.append(pl.BlockSpec((1, D_SSD, STATE), lambda b, c: (b, 0, 0)))
        args.append(h0)
    return pl.pallas_call(
        functools.partial(_ssd_bwd_kernel, has_h0, cps),
        grid=(nseq, nsteps),
        in_specs=in_specs,
        out_specs=[pl.BlockSpec((rows, D_SSD), lambda b, c: (gidx(b, c), 0)),
                   pl.BlockSpec((1, D_SSD, STATE), lambda b, c: (b, 0, 0))],
        out_shape=[jax.ShapeDtypeStruct((t, D_SSD), BF16),
                   jax.ShapeDtypeStruct((nseq, D_SSD, STATE), F32)],
        scratch_shapes=[pltpu.VMEM((STATE, D_SSD), F32)],
        compiler_params=_cparams(("arbitrary", "arbitrary")),
        name="ssd_bwd",
    )(*args)


def _mix_kernel(y_ref, u_ref, v_ref, ga_ref, gb_ref, x_ref, mod_ref, lnw_ref, lnb_ref, ws_ref, bst_ref,
                wbs_ref, wbg_ref, wout_ref, npost_ref, npre2_ref, o_ref, h2_ref, ysgu_scr):
    tm = x_ref.shape[0]
    v = v_ref[...].astype(F32)
    mu = jnp.mean(v, axis=-1, keepdims=True)
    vc = v - mu
    var = jnp.mean(vc * vc, axis=-1, keepdims=True)
    vn = (vc * lax.rsqrt(var + EPS) * lnw_ref[...] + lnb_ref[...]).astype(BF16)
    bst = bst_ref[...]
    for r in range(tm // CHUNK):
        rows = slice(r * CHUNK, (r + 1) * CHUNK)
        for g in range(SGU_GROUPS):
            cols = slice(g * LANES, (g + 1) * LANES)
            s = _dot(ws_ref[g], vn[rows, cols]) + bst[:, g:g + 1]
            ysgu_scr[rows, cols] = (u_ref[rows, cols].astype(F32) * s).astype(BF16)
    br_ssd = _dot(y_ref[...], wbs_ref[...])
    br_sgu = _dot(ysgu_scr[...], wbg_ref[...])
    merged = _sigmoid(ga_ref[...].astype(F32)) * br_ssd + _sigmoid(gb_ref[...].astype(F32)) * br_sgu
    merged = merged.astype(BF16)
    quarter = tm // 4
    for r in range(4):
        rows = slice(r * quarter, (r + 1) * quarter)
        mix = _dot(merged[rows, :], wout_ref[...])
        ms = jnp.mean(mix * mix, axis=-1, keepdims=True)
        x1 = x_ref[rows, :] + mod_ref[0, 2:3, :] * (mix * lax.rsqrt(ms + EPS) * npost_ref[...])
        o_ref[rows, :] = x1
        h2_ref[rows, :] = _modulated_norm(x1, npre2_ref[...], mod_ref[0, 3:4, :], mod_ref[0, 4:5, :]).astype(BF16)


def _mix(y_ssd, proj, x, mod, consts, tokens_per_mod):
    t = x.shape[0]
    tm = 512
    row = lambda i: (i, 0)
    const2 = lambda i: (0, 0)
    return pl.pallas_call(
        _mix_kernel,
        grid=(t // tm,),
        in_specs=[pl.BlockSpec((tm, D_SSD), row),
                  pl.BlockSpec((tm, D_MODEL), lambda i: (i, 5)),
                  pl.BlockSpec((tm, D_MODEL), lambda i: (i, 6)),
                  pl.BlockSpec((tm, D_MODEL), lambda i: (i, 7)),
                  pl.BlockSpec((tm, D_MODEL), lambda i: (i, 8)),
                  pl.BlockSpec((tm, D_MODEL), row),
                  pl.BlockSpec((1, 6, D_MODEL), lambda i: ((i * tm) // tokens_per_mod, 0, 0)),
                  pl.BlockSpec((1, D_MODEL), const2),
                  pl.BlockSpec((1, D_MODEL), const2),
                  pl.BlockSpec((SGU_GROUPS, CHUNK, CHUNK), lambda i: (0, 0, 0)),
                  pl.BlockSpec((CHUNK, SGU_GROUPS), const2),
                  pl.BlockSpec((D_SSD, D_MODEL), const2),
                  pl.BlockSpec((D_MODEL, D_MODEL), const2),
                  pl.BlockSpec((D_MODEL, D_MODEL), const2),
                  pl.BlockSpec((1, D_MODEL), const2),
                  pl.BlockSpec((1, D_MODEL), const2)],
        out_specs=[pl.BlockSpec((tm, D_MODEL), row), pl.BlockSpec((tm, D_MODEL), row)],
        out_shape=[jax.ShapeDtypeStruct((t, D_MODEL), F32), jax.ShapeDtypeStruct((t, D_MODEL), BF16)],
        scratch_shapes=[pltpu.VMEM((tm, D_MODEL), BF16)],
        compiler_params=_cparams(("arbitrary",)),
        name="mix",
    )(y_ssd, proj, proj, proj, proj, x, mod, consts["sgu_norm_w"], consts["sgu_norm_b"], consts["sgu_w"],
      consts["sgu_bt"], consts["w_branch_ssd"], consts["w_branch_sgu"], consts["w_out"], consts["norm_mix_post"],
      consts["norm_ffn_pre"])


def _matmul_kernel(h_ref, w_ref, o_ref):
    o_ref[...] = _dot(h_ref[...], w_ref[...]).astype(BF16)


def _ffn_up(h2, w_up):
    t = h2.shape[0]
    tm, tn = 1024, D_FF
    return pl.pallas_call(
        _matmul_kernel,
        grid=(t // tm, 2 * D_FF // tn),
        in_specs=[pl.BlockSpec((tm, D_MODEL), lambda i, j: (i, 0)),
                  pl.BlockSpec((D_MODEL, tn), lambda i, j: (0, j))],
        out_specs=pl.BlockSpec((tm, tn), lambda i, j: (i, j)),
        out_shape=jax.ShapeDtypeStruct((t, 2 * D_FF), BF16),
        compiler_params=_cparams(("arbitrary", "arbitrary")),
        name="ffn_up",
    )(h2, w_up)


def _gelu_tanh(x):
    h = 0.5 * x
    u = x * (GELU_C + (GELU_C * 0.044715) * (x * x))
    return h + h * jnp.tanh(u)


def _shift_rows(x_bf, period):
    w = pltpu.bitcast(x_bf, jnp.uint32)
    nw = w.shape[0]
    pos = lax.broadcasted_iota(jnp.int32, w.shape, 0) % (period // 2)
    high = w << 16
    low = w >> 16
    down = jnp.where(pos == 0, high, (pltpu.roll(w, 1, axis=0) >> 16) | high)
    up = jnp.where(pos == period // 2 - 1, low, low | (pltpu.roll(w, nw - 1, axis=0) << 16))
    return pltpu.bitcast(down, BF16), pltpu.bitcast(up, BF16)


def _grid_conv(x_bf, prev_bf, next_bf, w, b, first, last):
    tm = x_bf.shape[0]
    zero = jnp.zeros_like(prev_bf)
    ext_bf = jnp.concatenate([jnp.where(first, zero, prev_bf), x_bf, jnp.where(last, zero, next_bf)], axis=0)
    down, up = _shift_rows(ext_bf, GRID_W)
    shifted = (down, ext_bf, up)
    wb = w.astype(BF16)
    acc = None
    for dy in range(3):
        for dx in range(3):
            term = wb[3 * dy + dx:3 * dy + dx + 1, :] * shifted[dx][dy * GRID_W:dy * GRID_W + tm, :]
            acc = term if acc is None else acc + term
    return acc + b.astype(BF16)


def _seq_conv(x_bf, w, b, seq_len):
    xm1, xp1 = _shift_rows(x_bf, seq_len)
    wb = w.astype(BF16)
    return wb[3:4, :] * xm1 + wb[4:5, :] * x_bf + wb[5:6, :] * xp1 + b.astype(BF16)


def _ffn_down_kernel(on_grid, seq_len, tiles_per_seq, *refs):
    if on_grid:
        (up_ref, upp_ref, upn_ref, w_ref, b_ref, wd_ref, x_ref, mod_ref, npost_ref, o_ref,
         g0_scr, g1_scr, acc_scr) = refs
    else:
        (up_ref, w_ref, b_ref, wd_ref, x_ref, mod_ref, npost_ref, o_ref, g0_scr, g1_scr, acc_scr) = refs
    i = pl.program_id(0)
    first = i % tiles_per_seq == 0
    last = i % tiles_per_seq == tiles_per_seq - 1

    def conv(c0):
        cols = pl.ds(c0, FF_BLK)
        if on_grid:
            return _grid_conv(up_ref[:, cols], upp_ref[:, cols], upn_ref[:, cols], w_ref[:, cols], b_ref[:, cols],
                              first, last)
        return _seq_conv(up_ref[:, cols], w_ref[:, cols], b_ref[:, cols], seq_len)

    def geglu_block(k, dst_ref):
        c0 = pl.multiple_of(k * FF_BLK, FF_BLK)
        a = conv(c0)
        val = conv(pl.multiple_of(c0 + D_FF, FF_BLK))
        dst_ref[...] = (_gelu_tanh(a) * val).astype(BF16)

    def down_block(k, src_ref):
        rows = pl.ds(pl.multiple_of(k * FF_BLK, FF_BLK), FF_BLK)
        acc_scr[...] += _dot(src_ref[...], wd_ref[rows, :])

    acc_scr[...] = jnp.zeros_like(acc_scr)
    geglu_block(0, g0_scr)

    def block_pair(m, carry):
        k = 2 * m + 1
        down_block(k - 1, g0_scr)
        geglu_block(k, g1_scr)
        down_block(k, g1_scr)
        geglu_block(k + 1, g0_scr)
        return carry

    lax.fori_loop(0, (N_FF_BLK - 1) // 2, block_pair, 0)
    down_block(N_FF_BLK - 1, g0_scr)
    f = acc_scr[...]
    ms = jnp.mean(f * f, axis=-1, keepdims=True)
    o_ref[...] = x_ref[...] + mod_ref[0, 5:6, :] * (f * lax.rsqrt(ms + EPS) * npost_ref[...])


def _ffn_down(up, x, mod, consts, tokens_per_mod, on_grid, seq_len):
    t = x.shape[0]
    tm = 512
    tiles_per_seq = max(seq_len // tm, 1)
    rows_per_tile = tm // GRID_W
    n_rows = t // GRID_W
    in_specs = [pl.BlockSpec((tm, 2 * D_FF), lambda i: (i, 0))]
    args = [up]
    if on_grid:
        in_specs += [pl.BlockSpec((GRID_W, 2 * D_FF), lambda i: (jnp.maximum(i * rows_per_tile - 1, 0), 0)),
                     pl.BlockSpec((GRID_W, 2 * D_FF), lambda i: (jnp.minimum((i + 1) * rows_per_tile, n_rows - 1), 0))]
        args += [up, up]
    in_specs += [pl.BlockSpec((9, 2 * D_FF), lambda i: (0, 0)),
                 pl.BlockSpec((1, 2 * D_FF), lambda i: (0, 0)),
                 pl.BlockSpec((D_FF, D_MODEL), lambda i: (0, 0)),
                 pl.BlockSpec((tm, D_MODEL), lambda i: (i, 0)),
                 pl.BlockSpec((1, 6, D_MODEL), lambda i: ((i * tm) // tokens_per_mod, 0, 0)),
                 pl.BlockSpec((1, D_MODEL), lambda i: (0, 0))]
    args += [consts["ffn_conv_w"], consts["ffn_conv_b"], consts["ffn_w_down"], x, mod, consts["norm_ffn_post"]]
    return pl.pallas_call(
        functools.partial(_ffn_down_kernel, on_grid, seq_len, tiles_per_seq),
        grid=(t // tm,),
        in_specs=in_specs,
        out_specs=pl.BlockSpec((tm, D_MODEL), lambda i: (i, 0)),
        out_shape=jax.ShapeDtypeStruct((t, D_MODEL), F32),
        scratch_shapes=[pltpu.VMEM((tm, FF_BLK), BF16), pltpu.VMEM((tm, FF_BLK), BF16),
                        pltpu.VMEM((tm, D_MODEL), F32)],
        compiler_params=_cparams(("arbitrary",)),
        name="ffn_down",
    )(*args)


def _trunk_path(x, mod, h0_f, h0_b, on_grid, consts):
    nseq, seq_len, _ = x.shape
    t = nseq * seq_len
    nchunks = seq_len // CHUNK
    tokens_per_mod = t // mod.shape[0]
    x2d = x.reshape(t, D_MODEL)

    proj, dt_raw = _inproj(x2d, mod, consts["norm_mix_pre"], consts["w_in_main"], consts["w_in_dt"], tokens_per_mod)
    xs, bc, hprev, hf = _ssd_fwd(proj, dt_raw, h0_f, consts, nseq, nchunks)
    y_ssd, hb = _ssd_bwd(xs, bc, proj, dt_raw, hprev, h0_b, consts, nseq, nchunks)
    x1, h2 = _mix(y_ssd, proj, x2d, mod, consts, tokens_per_mod)
    up = _ffn_up(h2, consts["ffn_w_up"])
    x2 = _ffn_down(up, x1, mod, consts, tokens_per_mod, on_grid, seq_len)
    return x2.reshape(nseq, seq_len, D_MODEL), hf, hb


def _head_expansion(offset):
    rows = jnp.arange(LANES)[:, None]
    cols = jnp.arange(D_SSD)[None, :] // HEADDIM
    return (rows == cols + offset).astype(BF16)


def _layer_consts(i, p):
    w_in_main, w_in_dt = _w_in_prep(p["w_in"], i)
    row = lambda v: v.reshape(1, -1).astype(F32)
    pad_lanes = lambda v: jnp.pad(v.reshape(1, -1).astype(F32), ((0, 0), (0, LANES - N_DT)))
    conv_w = p["ssd_conv_w"][i]
    conv_b = p["ssd_conv_b"][i]
    return {
        "norm_mix_pre": row(p["norm_mix_pre"][i]),
        "norm_mix_post": row(p["norm_mix_post"][i]),
        "norm_ffn_pre": row(p["norm_ffn_pre"][i]),
        "norm_ffn_post": row(p["norm_ffn_post"][i]),
        "w_in_main": w_in_main, "w_in_dt": w_in_dt,
        "cw_x": conv_w[:, :D_SSD], "cw_bc": conv_w[:, D_SSD:],
        "cb_x": row(conv_b[:D_SSD]), "cb_bc": row(conv_b[D_SSD:]),
        "dt_bias": pad_lanes(p["ssd_dt_bias"][i]),
        "a_log": pad_lanes(p["ssd_a_log"][i]),
        "d_vec": row(jnp.repeat(p["ssd_d"][i], HEADDIM)),
        "ssd_norm": row(p["ssd_norm"][i]),
        "e_fwd": _head_expansion(0), "e_bwd": _head_expansion(HEADS),
        "sgu_norm_w": row(p["sgu_norm_w"][i]), "sgu_norm_b": row(p["sgu_norm_b"][i]),
        "sgu_w": p["sgu_w"][i].astype(BF16),
        "sgu_bt": jnp.transpose(p["sgu_b"][i]).astype(F32),
        "w_branch_ssd": p["w_branch_ssd"][i].astype(BF16),
        "w_branch_sgu": p["w_branch_sgu"][i].astype(BF16),
        "w_out": p["w_out"][i].astype(BF16),
        "ffn_w_up": p["ffn_w_up"][i].astype(BF16),
        "ffn_conv_w": p["ffn_conv_w"][i].reshape(9, 2 * D_FF).astype(F32),
        "ffn_conv_b": row(p["ffn_conv_b"][i]),
        "ffn_w_down": p["ffn_w_down"][i].astype(BF16),
    }


def kernel(x_prompt, x_sample, state_ssd_fwd, state_ssd_bwd, c, c_ctx, w_mod, b_mod, norm_mix_pre, norm_mix_post, norm_ffn_pre, norm_ffn_post, w_in, ssd_conv_w, ssd_conv_b, ssd_a_log, ssd_dt_bias, ssd_d, ssd_norm, sgu_norm_w, sgu_norm_b, sgu_w, sgu_b, w_branch_ssd, w_branch_sgu, w_out, ffn_w_up, ffn_conv_w, ffn_conv_b, ffn_w_down):
    params = dict(norm_mix_pre=norm_mix_pre, norm_mix_post=norm_mix_post, norm_ffn_pre=norm_ffn_pre,
                  norm_ffn_post=norm_ffn_post, w_in=w_in, ssd_conv_w=ssd_conv_w, ssd_conv_b=ssd_conv_b,
                  ssd_a_log=ssd_a_log, ssd_dt_bias=ssd_dt_bias, ssd_d=ssd_d, ssd_norm=ssd_norm,
                  sgu_norm_w=sgu_norm_w, sgu_norm_b=sgu_norm_b, sgu_w=sgu_w, sgu_b=sgu_b,
                  w_branch_ssd=w_branch_ssd, w_branch_sgu=w_branch_sgu, w_out=w_out, ffn_w_up=ffn_w_up,
                  ffn_conv_w=ffn_conv_w, ffn_conv_b=ffn_conv_b, ffn_w_down=ffn_w_down)
    depth = w_mod.shape[0]
    n_lat = c.shape[0]
    c_rows = jnp.concatenate([c_ctx[None, :], c, jnp.zeros((8 - 1 - n_lat, D_MODEL), F32)], axis=0)
    xp, xs = x_prompt, x_sample
    new_f, new_b = [], []
    for i in range(depth):
        consts = _layer_consts(i, params)
        mod = _mod_vectors(c_rows, w_mod[i], b_mod[i]).reshape(8, 6, D_MODEL)
        xp, hf, hb = _trunk_path(xp, mod[0:1], None, None, False, consts)
        new_f.append(hf.reshape(-1, HEADS, HEADDIM, STATE))
        new_b.append(hb.reshape(-1, HEADS, HEADDIM, STATE))
        xs, _, _ = _trunk_path(xs, mod[1:1 + n_lat],
                               state_ssd_fwd[:, i].reshape(n_lat, D_SSD, STATE),
                               state_ssd_bwd[:, i].reshape(n_lat, D_SSD, STATE), True, consts)
    return (xp, xs, jnp.stack(new_f, axis=1).astype(x_prompt.dtype), jnp.stack(new_b, axis=1).astype(x_prompt.dtype))
```

```python
import functools

import jax
import jax.numpy as jnp
from jax import lax
from jax.experimental import pallas as pl
from jax.experimental.pallas import tpu as pltpu

F32 = jnp.float32
BF16 = jnp.bfloat16

D_MODEL = 1024
GRID_W = 64
EPS = 1e-6
LOG2E = 1.4426950408889634
GELU_C = 0.7978845608028654
D_SSD = 2 * D_MODEL
HEADDIM = 64
HEADS = D_SSD // HEADDIM
STATE = 128
GROUPS = 4
HEADS_PER_GROUP = HEADS // GROUPS
GROUP_W = HEADS_PER_GROUP * HEADDIM
BC_W = GROUPS * STATE
CONV_CH = D_SSD + 2 * BC_W
CHUNK = 128
SSD_FWD_CHUNKS_PER_STEP = 2
SSD_BWD_CHUNKS_PER_STEP = 4
SGU_GROUPS = 8
D_FF = 2816
SPLIT_XBC = D_SSD + CONV_CH
N_DT = 2 * HEADS
LANES = 128
PROJ_COLS = 9216
FF_BLK = 256
N_FF_BLK = D_FF // FF_BLK

VMEM_LIMIT_BYTES = 56 * 1024 * 1024


def _cparams(sem):
    return pltpu.CompilerParams(dimension_semantics=sem, vmem_limit_bytes=VMEM_LIMIT_BYTES)


def _sigmoid(x):
    return 0.5 + 0.5 * jnp.tanh(0.5 * x)


def _silu(x):
    h = 0.5 * x
    return h + h * jnp.tanh(h)


def _dot(a, b):
    return jnp.dot(a, b, preferred_element_type=F32)


def _dot_nt(a, b):
    return lax.dot_general(a, b, (((1,), (1,)), ((), ())), preferred_element_type=F32)


def _mod_kernel(c_ref, w_ref, b_ref, o_ref):
    c = c_ref[...]
    o_ref[...] = jnp.dot(_silu(c), w_ref[...], preferred_element_type=F32,
                         precision=lax.Precision.HIGHEST) + b_ref[...]


def _mod_vectors(c_rows, w_mod, b_mod):
    rows = c_rows.shape[0]
    tn = 1024
    return pl.pallas_call(
        _mod_kernel,
        grid=(6 * D_MODEL // tn,),
        in_specs=[pl.BlockSpec((rows, D_MODEL), lambda j: (0, 0)),
                  pl.BlockSpec((D_MODEL, tn), lambda j: (0, j)),
                  pl.BlockSpec((1, tn), lambda j: (0, j))],
        out_specs=pl.BlockSpec((rows, tn), lambda j: (0, j)),
        out_shape=jax.ShapeDtypeStruct((rows, 6 * D_MODEL), F32),
        compiler_params=_cparams(("arbitrary",)),
        name="mod",
    )(c_rows, w_mod, b_mod.reshape(1, -1))


def _w_in_prep_kernel(wt_ref, main_ref, dt_ref):
    main_ref[:, 0:SPLIT_XBC] = wt_ref[0:SPLIT_XBC, :].T.astype(BF16)
    main_ref[:, SPLIT_XBC:] = wt_ref[SPLIT_XBC + N_DT:, :].T.astype(BF16)
    dt = wt_ref[SPLIT_XBC:SPLIT_XBC + LANES, :].T
    lane = lax.broadcasted_iota(jnp.int32, dt.shape, 1)
    dt_ref[...] = jnp.where(lane < N_DT, dt, 0.0).astype(BF16)


def _w_in_prep(w_in, layer):
    rows = 128
    return pl.pallas_call(
        _w_in_prep_kernel,
        grid=(D_MODEL // rows,),
        in_specs=[pl.BlockSpec((PROJ_COLS + N_DT, rows), lambda r: (0, r))],
        out_specs=[pl.BlockSpec((rows, PROJ_COLS), lambda r: (r, 0)),
                   pl.BlockSpec((rows, LANES), lambda r: (r, 0))],
        out_shape=[jax.ShapeDtypeStruct((D_MODEL, PROJ_COLS), BF16),
                   jax.ShapeDtypeStruct((D_MODEL, LANES), BF16)],
        compiler_params=_cparams(("arbitrary",)),
        name="w_in_prep",
    )(jnp.swapaxes(w_in[layer], 0, 1))


def _modulated_norm(x, nw, shift, scale):
    ms = jnp.mean(x * x, axis=-1, keepdims=True)
    return (x * lax.rsqrt(ms + EPS) * nw) * (1.0 + scale) + shift


def _inproj_kernel(n_col_steps, x0_ref, xn_ref, mod0_ref, modn_ref, nw_ref, w_ref, wdt_ref, o_ref, dt_ref,
                   ha_scr, hb_scr):
    i = pl.program_id(0)
    j = pl.program_id(1)
    part = xn_ref.shape[0] // n_col_steps
    rows = pl.ds(pl.multiple_of(j * part, part), part)

    def norm(x, mod_ref):
        return _modulated_norm(x, nw_ref[...], mod_ref[0, 0:1, :], mod_ref[0, 1:2, :]).astype(BF16)

    @pl.when((i == 0) & (j == 0))
    def _():
        ha_scr[...] = norm(x0_ref[...], mod0_ref)

    def step(cur_scr, nxt_scr):
        nxt_scr[rows, :] = norm(xn_ref[rows, :], modn_ref)
        dt_ref[rows, :] = _dot(cur_scr[rows, :], wdt_ref[...])
        o_ref[...] = _dot(cur_scr[...], w_ref[...]).astype(BF16)

    @pl.when(i % 2 == 0)
    def _():
        step(ha_scr, hb_scr)

    @pl.when(i % 2 == 1)
    def _():
        step(hb_scr, ha_scr)


def _inproj(x, mod, nw, w_main, w_dt, tokens_per_mod):
    t = x.shape[0]
    tm, tn = 1024, PROJ_COLS // 4
    n_tiles = t // tm
    nxt = lambda i: jnp.minimum(i + 1, n_tiles - 1)
    return pl.pallas_call(
        functools.partial(_inproj_kernel, PROJ_COLS // tn),
        grid=(n_tiles, PROJ_COLS // tn),
        in_specs=[pl.BlockSpec((tm, D_MODEL), lambda i, j: (0, 0)),
                  pl.BlockSpec((tm, D_MODEL), lambda i, j: (nxt(i), 0)),
                  pl.BlockSpec((1, 6, D_MODEL), lambda i, j: (0, 0, 0)),
                  pl.BlockSpec((1, 6, D_MODEL), lambda i, j: ((nxt(i) * tm) // tokens_per_mod, 0, 0)),
                  pl.BlockSpec((1, D_MODEL), lambda i, j: (0, 0)),
                  pl.BlockSpec((D_MODEL, tn), lambda i, j: (0, j)),
                  pl.BlockSpec((D_MODEL, LANES), lambda i, j: (0, 0))],
        out_specs=[pl.BlockSpec((tm, tn), lambda i, j: (i, j)),
                   pl.BlockSpec((tm, LANES), lambda i, j: (i, 0))],
        out_shape=[jax.ShapeDtypeStruct((t, PROJ_COLS), BF16),
                   jax.ShapeDtypeStruct((t, LANES), F32)],
        scratch_shapes=[pltpu.VMEM((tm, D_MODEL), BF16), pltpu.VMEM((tm, D_MODEL), BF16)],
        compiler_params=_cparams(("arbitrary", "arbitrary")),
        name="inproj",
    )(x, x, mod, mod, nw, w_main, w_dt)


def _softplus(x):
    return jnp.maximum(x, 0.0) + jnp.log1p(jnp.exp(-jnp.abs(x)))


def _cumsum_rows(a):
    n = a.shape[0]
    rid = lax.broadcasted_iota(jnp.int32, a.shape, 0)
    s = 1
    while s < n:
        a = a + jnp.where(rid >= s, pltpu.roll(a, s, axis=0), 0.0)
        s *= 2
    return a


def _rev_cumsum_rows(a):
    n = a.shape[0]
    rid = lax.broadcasted_iota(jnp.int32, a.shape, 0)
    s = 1
    while s < n:
        a = a + jnp.where(rid < n - s, pltpu.roll(a, n - s, axis=0), 0.0)
        s *= 2
    return a


def _expand_heads(w, e_ref):
    hi = w.astype(BF16)
    lo = (w - hi.astype(F32)).astype(BF16)
    e = e_ref[...]
    return _dot(hi, e) + _dot(lo, e)


def _row_shift_matrix(n):
    r = jnp.arange(n)[:, None]
    c = jnp.arange(n)[None, :]
    return jnp.concatenate([c == r - 1, c == r + 1], axis=0).astype(BF16)


def _conv3_silu(main_ref, prev_ref, next_ref, w_ref, b_ref, shift_ref, first, last):
    x_bf = main_ref[...]
    n = x_bf.shape[0]
    x = x_bf.astype(F32)
    sh = _dot(shift_ref[...], x_bf)
    xm1 = sh[:n, :]
    xp1 = sh[n:, :]
    prow = jnp.where(first, 0.0, prev_ref[...].astype(F32)[-1:, :])
    nrow = jnp.where(last, 0.0, next_ref[...].astype(F32)[0:1, :])
    rid = lax.broadcasted_iota(jnp.int32, (8, x.shape[1]), 0)
    xm1 = jnp.concatenate([jnp.where(rid == 0, prow, xm1[:8, :]), xm1[8:, :]], axis=0)
    xp1 = jnp.concatenate([xp1[:-8, :], jnp.where(rid == 7, nrow, xp1[-8:, :])], axis=0)
    w = w_ref[...]
    y = w[0:1, :] * xm1 + w[1:2, :] * x + w[2:3, :] * xp1 + b_ref[...]
    return _silu(y)


def _dt_and_log_decay(dt_raw, dtb_ref, alog_ref):
    dt = _softplus(dt_raw + dtb_ref[...])
    return dt, dt * (-jnp.exp(alog_ref[...]))


def _pair_rhs(xs_bf, pair, left):
    xp = xs_bf[:, pair * LANES:(pair + 1) * LANES]
    zero = jnp.zeros_like(xp)
    return jnp.concatenate([jnp.where(left, xp, zero), jnp.where(left, zero, xp)], axis=0)


def _ssd_fwd_kernel(has_h0, cps, *refs):
    (xm_ref, bcm_ref, xp_ref, bcp_ref, xn_ref, bcn_ref, dt_ref,
     cwx_ref, cwbc_ref, cbx_ref, cbbc_ref, dtb_ref, alog_ref, ef_ref, s3_ref) = refs[:15]
    pos = 15
    h0_ref = None
    if has_h0:
        h0_ref = refs[pos]
        pos += 1
    xs_ref, bc_ref, hprev_ref, hfin_ref, h_scr = refs[pos:pos + 5]

    c = pl.program_id(1)
    nsteps = pl.num_programs(1)
    first = c == 0
    last = c == nsteps - 1

    @pl.when(first)
    def _():
        if has_h0:
            h_scr[...] = h0_ref[0].T
        else:
            h_scr[...] = jnp.zeros_like(h_scr)

    xs_all = _conv3_silu(xm_ref, xp_ref, xn_ref, cwx_ref, cbx_ref, s3_ref, first, last)
    bc_all = _conv3_silu(bcm_ref, bcp_ref, bcn_ref, cwbc_ref, cbbc_ref, s3_ref, first, last)
    xs_all_bf = xs_all.astype(BF16)
    xs_ref[...] = xs_all_bf
    bc_ref[...] = bc_all.astype(BF16)

    left = lax.broadcasted_iota(jnp.int32, (CHUNK, LANES), 1) < HEADDIM
    h_cur = h_scr[...]
    for u in range(cps):
        rows = slice(u * CHUNK, (u + 1) * CHUNK)
        xs_bf = xs_all_bf[rows, :]
        bc = bc_all[rows, :]
        dt, a = _dt_and_log_decay(dt_ref[rows, :], dtb_ref, alog_ref)
        acs = _cumsum_rows(a)
        acs_t = acs.T
        w2_t = dt.T * jnp.exp(acs_t[:, CHUNK - 1:CHUNK] - acs_t)
        cdec = _expand_heads(jnp.broadcast_to(jnp.exp(acs[CHUNK - 1:CHUNK, :]), (8, LANES)), ef_ref)[0:1, :]

        parts = []
        for g in range(GROUPS):
            bt = bc[:, g * STATE:(g + 1) * STATE].T
            for k in range(HEADS_PER_GROUP // 2):
                h = g * HEADS_PER_GROUP + 2 * k
                lhs = jnp.concatenate([(bt * w2_t[h:h + 1, :]).astype(BF16),
                                       (bt * w2_t[h + 1:h + 2, :]).astype(BF16)], axis=1)
                parts.append(_dot(lhs, _pair_rhs(xs_bf, h // 2, left)))
        s_loc = jnp.concatenate(parts, axis=1)

        hprev_ref[u] = h_cur.astype(BF16)
        h_cur = cdec * h_cur + s_loc
    h_scr[...] = h_cur

    @pl.when(last)
    def _():
        hfin_ref[0] = h_cur.T


def _ssd_fwd(proj, dt_raw, h0, consts, nseq, nchunks):
    t = nseq * nchunks * CHUNK
    cps = min(SSD_FWD_CHUNKS_PER_STEP, nchunks)
    nsteps = nchunks // cps
    rows = cps * CHUNK
    halo = 16
    per = rows // halo
    n_halo = t // halo
    gidx = lambda b, c: b * nsteps + c
    prev_blk = lambda b, c: jnp.maximum(gidx(b, c) * per - 1, 0)
    next_blk = lambda b, c: jnp.minimum((gidx(b, c) + 1) * per, n_halo - 1)
    in_specs = [
        pl.BlockSpec((rows, D_SSD), lambda b, c: (gidx(b, c), 1)),
        pl.BlockSpec((rows, 2 * BC_W), lambda b, c: (gidx(b, c), 4)),
        pl.BlockSpec((halo, D_SSD), lambda b, c: (prev_blk(b, c), 1)),
        pl.BlockSpec((halo, 2 * BC_W), lambda b, c: (prev_blk(b, c), 4)),
        pl.BlockSpec((halo, D_SSD), lambda b, c: (next_blk(b, c), 1)),
        pl.BlockSpec((halo, 2 * BC_W), lambda b, c: (next_blk(b, c), 4)),
        pl.BlockSpec((rows, LANES), lambda b, c: (gidx(b, c), 0)),
        pl.BlockSpec((3, D_SSD), lambda b, c: (0, 0)),
        pl.BlockSpec((3, 2 * BC_W), lambda b, c: (0, 0)),
        pl.BlockSpec((1, D_SSD), lambda b, c: (0, 0)),
        pl.BlockSpec((1, 2 * BC_W), lambda b, c: (0, 0)),
        pl.BlockSpec((1, LANES), lambda b, c: (0, 0)),
        pl.BlockSpec((1, LANES), lambda b, c: (0, 0)),
        pl.BlockSpec((LANES, D_SSD), lambda b, c: (0, 0)),
        pl.BlockSpec((2 * rows, rows), lambda b, c: (0, 0)),
    ]
    args = [proj, proj, proj, proj, proj, proj, dt_raw,
            consts["cw_x"], consts["cw_bc"], consts["cb_x"], consts["cb_bc"],
            consts["dt_bias"], consts["a_log"], consts["e_fwd"], _row_shift_matrix(rows)]
    has_h0 = h0 is not None
    if has_h0:
        in_specs.append(pl.BlockSpec((1, D_SSD, STATE), lambda b, c: (b, 0, 0)))
        args.append(h0)
    return pl.pallas_call(
        functools.partial(_ssd_fwd_kernel, has_h0, cps),
        grid=(nseq, nsteps),
        in_specs=in_specs,
        out_specs=[pl.BlockSpec((rows, D_SSD), lambda b, c: (gidx(b, c), 0)),
                   pl.BlockSpec((rows, 2 * BC_W), lambda b, c: (gidx(b, c), 0)),
                   pl.BlockSpec((cps, STATE, D_SSD), lambda b, c: (gidx(b, c), 0, 0)),
                   pl.BlockSpec((1, D_SSD, STATE), lambda b, c: (b, 0, 0))],
        out_shape=[jax.ShapeDtypeStruct((t, D_SSD), BF16),
                   jax.ShapeDtypeStruct((t, 2 * BC_W), BF16),
                   jax.ShapeDtypeStruct((nseq * nchunks, STATE, D_SSD), BF16),
                   jax.ShapeDtypeStruct((nseq, D_SSD, STATE), F32)],
        scratch_shapes=[pltpu.VMEM((STATE, D_SSD), F32)],
        compiler_params=_cparams(("arbitrary", "arbitrary")),
        name="ssd_fwd",
    )(*args)


def _ssd_bwd_kernel(has_h0, cps, *refs):
    (xs_ref, bc_ref, z_ref, dt_ref, hprev_ref, dtb_ref, alog_ref, dvec_ref, nw_ref,
     eb_ref) = refs[:10]
    pos = 10
    h0_ref = None
    if has_h0:
        h0_ref = refs[pos]
        pos += 1
    y_ref, hfin_ref, h_scr = refs[pos:pos + 3]

    c = pl.program_id(1)
    nsteps = pl.num_programs(1)

    @pl.when(c == 0)
    def _():
        if has_h0:
            h_scr[...] = h0_ref[0].T
        else:
            h_scr[...] = jnp.zeros_like(h_scr)

    ri = lax.broadcasted_iota(jnp.int32, (CHUNK, CHUNK), 0)
    ci = lax.broadcasted_iota(jnp.int32, (CHUNK, CHUNK), 1)
    lower = ri >= ci
    diag = ri == ci
    left = lax.broadcasted_iota(jnp.int32, (CHUNK, LANES), 1) < HEADDIM

    h_b = h_scr[...]
    for u in reversed(range(cps)):
        rows = slice(u * CHUNK, (u + 1) * CHUNK)
        xs_bf = xs_ref[rows, :]
        xs = xs_bf.astype(F32)
        bc_bf = bc_ref[rows, :]

        dt, a = _dt_and_log_decay(dt_ref[rows, :], dtb_ref, alog_ref)
        acs = _cumsum_rows(a) * LOG2E
        rcs = _rev_cumsum_rows(a) * LOG2E
        acs_t = acs.T
        rcs_t = rcs.T
        dt_t = dt.T
        lg_t = jnp.log2(dt_t)
        rf_t = acs_t - lg_t
        rb_t = rcs_t - lg_t
        w2b_t = dt_t * jnp.exp2(rcs_t[:, 0:1] - rcs_t)
        cdec = _expand_heads(jnp.broadcast_to(jnp.exp2(rcs[0:1, :]), (8, LANES)), eb_ref)[0:1, :]

        h_f = hprev_ref[u]
        h_b_bf = h_b.astype(BF16)

        y_parts = []
        s_parts = []
        for g in range(GROUPS):
            b_g = bc_bf[:, g * STATE:(g + 1) * STATE]
            c_g = bc_bf[:, BC_W + g * STATE:BC_W + (g + 1) * STATE]
            cb = _dot_nt(c_g, b_g)
            bt = b_g.astype(F32).T
            off_f = _dot(c_g, h_f[:, g * GROUP_W:(g + 1) * GROUP_W])
            off_b = _dot(c_g, h_b_bf[:, g * GROUP_W:(g + 1) * GROUP_W])
            for k in range(HEADS_PER_GROUP // 2):
                ms, bs, cfs, cbs = [], [], [], []
                for h in (g * HEADS_PER_GROUP + 2 * k, g * HEADS_PER_GROUP + 2 * k + 1):
                    hb = HEADS + h
                    cf = acs[:, h:h + 1]
                    cbk = rcs[:, hb:hb + 1]
                    e = jnp.exp2(jnp.where(lower, cf - rf_t[h:h + 1, :], cbk - rb_t[hb:hb + 1, :]))
                    e = e + jnp.where(diag, dt_t[hb:hb + 1, :], 0.0)
                    ms.append((cb * e).astype(BF16))
                    bs.append((bt * w2b_t[hb:hb + 1, :]).astype(BF16))
                    cfs.append(cf)
                    cbs.append(cbk)
                lhs = jnp.concatenate([jnp.concatenate(ms, axis=1), jnp.concatenate(bs, axis=1)], axis=0)
                out = _dot(lhs, _pair_rhs(xs_bf, g * HEADS_PER_GROUP // 2 + k, left))
                ef = jnp.exp2(jnp.where(left, cfs[0], cfs[1]))
                eb = jnp.exp2(jnp.where(left, cbs[0], cbs[1]))
                cols = slice(k * LANES, (k + 1) * LANES)
                y_parts.append(out[:CHUNK, :] + ef * off_f[:, cols] + eb * off_b[:, cols])
                s_parts.append(out[CHUNK:, :])
        y = jnp.concatenate(y_parts, axis=1) + dvec_ref[...] * xs
        y = y * _silu(z_ref[rows, :].astype(F32))
        ms_y = jnp.mean(y * y, axis=-1, keepdims=True)
        y_ref[rows, :] = (y * lax.rsqrt(ms_y + EPS) * nw_ref[...]).astype(BF16)

        h_b = cdec * h_b + jnp.concatenate(s_parts, axis=1)
    h_scr[...] = h_b

    @pl.when(c == nsteps - 1)
    def _():
        hfin_ref[0] = h_b.T


def _ssd_bwd(xs, bc, proj, dt_raw, hprev, h0, consts, nseq, nchunks):
    t = nseq * nchunks * CHUNK
    cps = min(SSD_BWD_CHUNKS_PER_STEP, nchunks)
    nsteps = nchunks // cps
    rows = cps * CHUNK
    gidx = lambda b, c: b * nsteps + (nsteps - 1 - c)
    in_specs = [
        pl.BlockSpec((rows, D_SSD), lambda b, c: (gidx(b, c), 0)),
        pl.BlockSpec((rows, 2 * BC_W), lambda b, c: (gidx(b, c), 0)),
        pl.BlockSpec((rows, D_SSD), lambda b, c: (gidx(b, c), 0)),
        pl.BlockSpec((rows, LANES), lambda b, c: (gidx(b, c), 0)),
        pl.BlockSpec((cps, STATE, D_SSD), lambda b, c: (gidx(b, c), 0, 0)),
        pl.BlockSpec((1, LANES), lambda b, c: (0, 0)),
        pl.BlockSpec((1, LANES), lambda b, c: (0, 0)),
        pl.BlockSpec((1, D_SSD), lambda b, c: (0, 0)),
        pl.BlockSpec((1, D_SSD), lambda b, c: (0, 0)),
        pl.BlockSpec((LANES, D_SSD), lambda b, c: (0, 0)),
    ]
    args = [xs, bc, proj, dt_raw, hprev, consts["dt_bias"], consts["a_log"], consts["d_vec"],
            consts["ssd_norm"], consts["e_bwd"]]
    has_h0 = h0 is not None
    if has_h0:
        in_specs.append(pl.BlockSpec((1, D_SSD, STATE), lambda b, c: (b, 0, 0)))
        args.append(h0)
    return pl.pallas_call(
        functools.partial(_ssd_bwd_kernel, has_h0, cps),
        grid=(nseq, nsteps),
        in_specs=in_specs,
        out_specs=[pl.BlockSpec((rows, D_SSD), lambda b, c: (gidx(b, c), 0)),
                   pl.BlockSpec((1, D_SSD, STATE), lambda b, c: (b, 0, 0))],
        out_shape=[jax.ShapeDtypeStruct((t, D_SSD), BF16),
                   jax.ShapeDtypeStruct((nseq, D_SSD, STATE), F32)],
        scratch_shapes=[pltpu.VMEM((STATE, D_SSD), F32)],
        compiler_params=_cparams(("arbitrary", "arbitrary")),
        name="ssd_bwd",
    )(*args)


def _mix_kernel(y_ref, u_ref, v_ref, ga_ref, gb_ref, x_ref, mod_ref, lnw_ref, lnb_ref, ws_ref, bst_ref,
                wbs_ref, wbg_ref, wout_ref, npost_ref, npre2_ref, o_ref, h2_ref, ysgu_scr):
    tm = x_ref.shape[0]
    v = v_ref[...].astype(F32)
    mu = jnp.mean(v, axis=-1, keepdims=True)
    vc = v - mu
    var = jnp.mean(vc * vc, axis=-1, keepdims=True)
    vn = (vc * lax.rsqrt(var + EPS) * lnw_ref[...] + lnb_ref[...]).astype(BF16)
    bst = bst_ref[...]
    for r in range(tm // CHUNK):
        rows = slice(r * CHUNK, (r + 1) * CHUNK)
        for g in range(SGU_GROUPS):
            cols = slice(g * LANES, (g + 1) * LANES)
            s = _dot(ws_ref[g], vn[rows, cols]) + bst[:, g:g + 1]
            ysgu_scr[rows, cols] = (u_ref[rows, cols].astype(F32) * s).astype(BF16)
    br_ssd = _dot(y_ref[...], wbs_ref[...])
    br_sgu = _dot(ysgu_scr[...], wbg_ref[...])
    merged = _sigmoid(ga_ref[...].astype(F32)) * br_ssd + _sigmoid(gb_ref[...].astype(F32)) * br_sgu
    merged = merged.astype(BF16)
    quarter = tm // 4
    for r in range(4):
        rows = slice(r * quarter, (r + 1) * quarter)
        mix = _dot(merged[rows, :], wout_ref[...])
        ms = jnp.mean(mix * mix, axis=-1, keepdims=True)
        x1 = x_ref[rows, :] + mod_ref[0, 2:3, :] * (mix * lax.rsqrt(ms + EPS) * npost_ref[...])
        o_ref[rows, :] = x1
        h2_ref[rows, :] = _modulated_norm(x1, npre2_ref[...], mod_ref[0, 3:4, :], mod_ref[0, 4:5, :]).astype(BF16)


def _mix(y_ssd, proj, x, mod, consts, tokens_per_mod):
    t = x.shape[0]
    tm = 512
    row = lambda i: (i, 0)
    const2 = lambda i: (0, 0)
    return pl.pallas_call(
        _mix_kernel,
        grid=(t // tm,),
        in_specs=[pl.BlockSpec((tm, D_SSD), row),
                  pl.BlockSpec((tm, D_MODEL), lambda i: (i, 5)),
                  pl.BlockSpec((tm, D_MODEL), lambda i: (i, 6)),
                  pl.BlockSpec((tm, D_MODEL), lambda i: (i, 7)),
                  pl.BlockSpec((tm, D_MODEL), lambda i: (i, 8)),
                  pl.BlockSpec((tm, D_MODEL), row),
                  pl.BlockSpec((1, 6, D_MODEL), lambda i: ((i * tm) // tokens_per_mod, 0, 0)),
                  pl.BlockSpec((1, D_MODEL), const2),
                  pl.BlockSpec((1, D_MODEL), const2),
                  pl.BlockSpec((SGU_GROUPS, CHUNK, CHUNK), lambda i: (0, 0, 0)),
                  pl.BlockSpec((CHUNK, SGU_GROUPS), const2),
                  pl.BlockSpec((D_SSD, D_MODEL), const2),
                  pl.BlockSpec((D_MODEL, D_MODEL), const2),
                  pl.BlockSpec((D_MODEL, D_MODEL), const2),
                  pl.BlockSpec((1, D_MODEL), const2),
                  pl.BlockSpec((1, D_MODEL), const2)],
        out_specs=[pl.BlockSpec((tm, D_MODEL), row), pl.BlockSpec((tm, D_MODEL), row)],
        out_shape=[jax.ShapeDtypeStruct((t, D_MODEL), F32), jax.ShapeDtypeStruct((t, D_MODEL), BF16)],
        scratch_shapes=[pltpu.VMEM((tm, D_MODEL), BF16)],
        compiler_params=_cparams(("arbitrary",)),
        name="mix",
    )(y_ssd, proj, proj, proj, proj, x, mod, consts["sgu_norm_w"], consts["sgu_norm_b"], consts["sgu_w"],
      consts["sgu_bt"], consts["w_branch_ssd"], consts["w_branch_sgu"], consts["w_out"], consts["norm_mix_post"],
      consts["norm_ffn_pre"])


def _matmul_kernel(h_ref, w_ref, o_ref):
    o_ref[...] = _dot(h_ref[...], w_ref[...]).astype(BF16)


def _ffn_up(h2, w_up):
    t = h2.shape[0]
    tm, tn = 1024, D_FF
    return pl.pallas_call(
        _matmul_kernel,
        grid=(t // tm, 2 * D_FF // tn),
        in_specs=[pl.BlockSpec((tm, D_MODEL), lambda i, j: (i, 0)),
                  pl.BlockSpec((D_MODEL, tn), lambda i, j: (0, j))],
        out_specs=pl.BlockSpec((tm, tn), lambda i, j: (i, j)),
        out_shape=jax.ShapeDtypeStruct((t, 2 * D_FF), BF16),
        compiler_params=_cparams(("arbitrary", "arbitrary")),
        name="ffn_up",
    )(h2, w_up)


def _gelu_tanh(x):
    h = 0.5 * x
    u = x * (GELU_C + (GELU_C * 0.044715) * (x * x))
    return h + h * jnp.tanh(u)


def _shift_rows(x_bf, period):
    w = pltpu.bitcast(x_bf, jnp.uint32)
    nw = w.shape[0]
    pos = lax.broadcasted_iota(jnp.int32, w.shape, 0) % (period // 2)
    high = w << 16
    low = w >> 16
    zero = jnp.zeros_like(w)
    down = pltpu.roll(jnp.where(pos == period // 2 - 1, zero, low), 1, axis=0) | high
    up = low | pltpu.roll(jnp.where(pos == 0, zero, high), nw - 1, axis=0)
    return pltpu.bitcast(down, BF16), pltpu.bitcast(up, BF16)


def _grid_conv(x_bf, prev_bf, next_bf, w, b, first, last):
    tm = x_bf.shape[0]
    zero = jnp.zeros_like(prev_bf)
    ext_bf = jnp.concatenate([jnp.where(first, zero, prev_bf), x_bf, jnp.where(last, zero, next_bf)], axis=0)
    down, up = _shift_rows(ext_bf, GRID_W)
    shifted = (down, ext_bf, up)
    wb = w.astype(BF16)
    acc = None
    for dy in range(3):
        for dx in range(3):
            term = wb[3 * dy + dx:3 * dy + dx + 1, :] * shifted[dx][dy * GRID_W:dy * GRID_W + tm, :]
            acc = term if acc is None else acc + term
    return acc + b.astype(BF16)


def _seq_conv(x_bf, w, b, seq_len):
    xm1, xp1 = _shift_rows(x_bf, seq_len)
    wb = w.astype(BF16)
    return wb[3:4, :] * xm1 + wb[4:5, :] * x_bf + wb[5:6, :] * xp1 + b.astype(BF16)


def _ffn_down_kernel(on_grid, seq_len, tiles_per_seq, *refs):
    if on_grid:
        (up_ref, upp_ref, upn_ref, w_ref, b_ref, wd_ref, x_ref, mod_ref, npost_ref, o_ref,
         g0_scr, g1_scr, acc_scr) = refs
    else:
        (up_ref, w_ref, b_ref, wd_ref, x_ref, mod_ref, npost_ref, o_ref, g0_scr, g1_scr, acc_scr) = refs
    i = pl.program_id(0)
    first = i % tiles_per_seq == 0
    last = i % tiles_per_seq == tiles_per_seq - 1

    def conv(c0):
        cols = pl.ds(c0, FF_BLK)
        if on_grid:
            return _grid_conv(up_ref[:, cols], upp_ref[:, cols], upn_ref[:, cols], w_ref[:, cols], b_ref[:, cols],
                              first, last)
        return _seq_conv(up_ref[:, cols], w_ref[:, cols], b_ref[:, cols], seq_len)

    def geglu_block(k, dst_ref):
        c0 = pl.multiple_of(k * FF_BLK, FF_BLK)
        a = conv(c0)
        val = conv(pl.multiple_of(c0 + D_FF, FF_BLK))
        dst_ref[...] = (_gelu_tanh(a) * val).astype(BF16)

    def down_block(k, src_ref):
        rows = pl.ds(pl.multiple_of(k * FF_BLK, FF_BLK), FF_BLK)
        acc_scr[...] += _dot(src_ref[...], wd_ref[rows, :])

    acc_scr[...] = jnp.zeros_like(acc_scr)
    geglu_block(0, g0_scr)

    def block_pair(m, carry):
        k = 2 * m + 1
        down_block(k - 1, g0_scr)
        geglu_block(k, g1_scr)
        down_block(k, g1_scr)
        geglu_block(k + 1, g0_scr)
        return carry

    lax.fori_loop(0, (N_FF_BLK - 1) // 2, block_pair, 0)
    down_block(N_FF_BLK - 1, g0_scr)
    f = acc_scr[...]
    ms = jnp.mean(f * f, axis=-1, keepdims=True)
    o_ref[...] = x_ref[...] + mod_ref[0, 5:6, :] * (f * lax.rsqrt(ms + EPS) * npost_ref[...])


def _ffn_down(up, x, mod, consts, tokens_per_mod, on_grid, seq_len):
    t = x.shape[0]
    tm = 512
    tiles_per_seq = max(seq_len // tm, 1)
    rows_per_tile = tm // GRID_W
    n_rows = t // GRID_W
    in_specs = [pl.BlockSpec((tm, 2 * D_FF), lambda i: (i, 0))]
    args = [up]
    if on_grid:
        in_specs += [pl.BlockSpec((GRID_W, 2 * D_FF), lambda i: (jnp.maximum(i * rows_per_tile - 1, 0), 0)),
                     pl.BlockSpec((GRID_W, 2 * D_FF), lambda i: (jnp.minimum((i + 1) * rows_per_tile, n_rows - 1), 0))]
        args += [up, up]
    in_specs += [pl.BlockSpec((9, 2 * D_FF), lambda i: (0, 0)),
                 pl.BlockSpec((1, 2 * D_FF), lambda i: (0, 0)),
                 pl.BlockSpec((D_FF, D_MODEL), lambda i: (0, 0)),
                 pl.BlockSpec((tm, D_MODEL), lambda i: (i, 0)),
                 pl.BlockSpec((1, 6, D_MODEL), lambda i: ((i * tm) // tokens_per_mod, 0, 0)),
                 pl.BlockSpec((1, D_MODEL), lambda i: (0, 0))]
    args += [consts["ffn_conv_w"], consts["ffn_conv_b"], consts["ffn_w_down"], x, mod, consts["norm_ffn_post"]]
    return pl.pallas_call(
        functools.partial(_ffn_down_kernel, on_grid, seq_len, tiles_per_seq),
        grid=(t // tm,),
        in_specs=in_specs,
        out_specs=pl.BlockSpec((tm, D_MODEL), lambda i: (i, 0)),
        out_shape=jax.ShapeDtypeStruct((t, D_MODEL), F32),
        scratch_shapes=[pltpu.VMEM((tm, FF_BLK), BF16), pltpu.VMEM((tm, FF_BLK), BF16),
                        pltpu.VMEM((tm, D_MODEL), F32)],
        compiler_params=_cparams(("arbitrary",)),
        name="ffn_down",
    )(*args)


def _trunk_path(x, mod, h0_f, h0_b, on_grid, consts):
    nseq, seq_len, _ = x.shape
    t = nseq * seq_len
    nchunks = seq_len // CHUNK
    tokens_per_mod = t // mod.shape[0]
    x2d = x.reshape(t, D_MODEL)

    proj, dt_raw = _inproj(x2d, mod, consts["norm_mix_pre"], consts["w_in_main"], consts["w_in_dt"], tokens_per_mod)
    xs, bc, hprev, hf = _ssd_fwd(proj, dt_raw, h0_f, consts, nseq, nchunks)
    y_ssd, hb = _ssd_bwd(xs, bc, proj, dt_raw, hprev, h0_b, consts, nseq, nchunks)
    x1, h2 = _mix(y_ssd, proj, x2d, mod, consts, tokens_per_mod)
    up = _ffn_up(h2, consts["ffn_w_up"])
    x2 = _ffn_down(up, x1, mod, consts, tokens_per_mod, on_grid, seq_len)
    return x2.reshape(nseq, seq_len, D_MODEL), hf, hb


def _head_expansion(offset):
    rows = jnp.arange(LANES)[:, None]
    cols = jnp.arange(D_SSD)[None, :] // HEADDIM
    return (rows == cols + offset).astype(BF16)


def _layer_consts(i, p):
    w_in_main, w_in_dt = _w_in_prep(p["w_in"], i)
    row = lambda v: v.reshape(1, -1).astype(F32)
    pad_lanes = lambda v: jnp.pad(v.reshape(1, -1).astype(F32), ((0, 0), (0, LANES - N_DT)))
    conv_w = p["ssd_conv_w"][i]
    conv_b = p["ssd_conv_b"][i]
    return {
        "norm_mix_pre": row(p["norm_mix_pre"][i]),
        "norm_mix_post": row(p["norm_mix_post"][i]),
        "norm_ffn_pre": row(p["norm_ffn_pre"][i]),
        "norm_ffn_post": row(p["norm_ffn_post"][i]),
        "w_in_main": w_in_main, "w_in_dt": w_in_dt,
        "cw_x": conv_w[:, :D_SSD], "cw_bc": conv_w[:, D_SSD:],
        "cb_x": row(conv_b[:D_SSD]), "cb_bc": row(conv_b[D_SSD:]),
        "dt_bias": pad_lanes(p["ssd_dt_bias"][i]),
        "a_log": pad_lanes(p["ssd_a_log"][i]),
        "d_vec": row(jnp.repeat(p["ssd_d"][i], HEADDIM)),
        "ssd_norm": row(p["ssd_norm"][i]),
        "e_fwd": _head_expansion(0), "e_bwd": _head_expansion(HEADS),
        "sgu_norm_w": row(p["sgu_norm_w"][i]), "sgu_norm_b": row(p["sgu_norm_b"][i]),
        "sgu_w": p["sgu_w"][i].astype(BF16),
        "sgu_bt": jnp.transpose(p["sgu_b"][i]).astype(F32),
        "w_branch_ssd": p["w_branch_ssd"][i].astype(BF16),
        "w_branch_sgu": p["w_branch_sgu"][i].astype(BF16),
        "w_out": p["w_out"][i].astype(BF16),
        "ffn_w_up": p["ffn_w_up"][i].astype(BF16),
        "ffn_conv_w": p["ffn_conv_w"][i].reshape(9, 2 * D_FF).astype(F32),
        "ffn_conv_b": row(p["ffn_conv_b"][i]),
        "ffn_w_down": p["ffn_w_down"][i].astype(BF16),
    }


def kernel(x_prompt, x_sample, state_ssd_fwd, state_ssd_bwd, c, c_ctx, w_mod, b_mod, norm_mix_pre, norm_mix_post, norm_ffn_pre, norm_ffn_post, w_in, ssd_conv_w, ssd_conv_b, ssd_a_log, ssd_dt_bias, ssd_d, ssd_norm, sgu_norm_w, sgu_norm_b, sgu_w, sgu_b, w_branch_ssd, w_branch_sgu, w_out, ffn_w_up, ffn_conv_w, ffn_conv_b, ffn_w_down):
    params = dict(norm_mix_pre=norm_mix_pre, norm_mix_post=norm_mix_post, norm_ffn_pre=norm_ffn_pre,
                  norm_ffn_post=norm_ffn_post, w_in=w_in, ssd_conv_w=ssd_conv_w, ssd_conv_b=ssd_conv_b,
                  ssd_a_log=ssd_a_log, ssd_dt_bias=ssd_dt_bias, ssd_d=ssd_d, ssd_norm=ssd_norm,
                  sgu_norm_w=sgu_norm_w, sgu_norm_b=sgu_norm_b, sgu_w=sgu_w, sgu_b=sgu_b,
                  w_branch_ssd=w_branch_ssd, w_branch_sgu=w_branch_sgu, w_out=w_out, ffn_w_up=ffn_w_up,
                  ffn_conv_w=ffn_conv_w, ffn_conv_b=ffn_conv_b, ffn_w_down=ffn_w_down)
    depth = w_mod.shape[0]
    n_lat = c.shape[0]
    c_rows = jnp.concatenate([c_ctx[None, :], c, jnp.zeros((8 - 1 - n_lat, D_MODEL), F32)], axis=0)
    xp, xs = x_prompt, x_sample
    new_f, new_b = [], []
    for i in range(depth):
        consts = _layer_consts(i, params)
        mod = _mod_vectors(c_rows, w_mod[i], b_mod[i]).reshape(8, 6, D_MODEL)
        xp, hf, hb = _trunk_path(xp, mod[0:1], None, None, False, consts)
        new_f.append(hf.reshape(-1, HEADS, HEADDIM, STATE))
        new_b.append(hb.reshape(-1, HEADS, HEADDIM, STATE))
        xs, _, _ = _trunk_path(xs, mod[1:1 + n_lat],
                               state_ssd_fwd[:, i].reshape(n_lat, D_SSD, STATE),
                               state_ssd_bwd[:, i].reshape(n_lat, D_SSD, STATE), True, consts)
    return (xp, xs, jnp.stack(new_f, axis=1).astype(x_prompt.dtype), jnp.stack(new_b, axis=1).astype(x_prompt.dtype))
```
